```python
import math
import jax, jax.numpy as jnp
from jax import lax
import numpy as np

D_MODEL = 2048
BATCH = 4
SEQ = 2048
DEPTH = 2

CTX_LEN = 256
GRID_W = 64
HEAD_DIM = 128
A_HEADS = 8
A_KV_HEADS = 2
A_WIDTH = A_HEADS * HEAD_DIM
KV_WIDTH = A_KV_HEADS * HEAD_DIM
B_WIDTH = D_MODEL - A_WIDTH
B_CONV = 31
Q_BLOCK = 128
ROPE_THETA = 10000.0
AB_SPLITS = [A_WIDTH, A_WIDTH + KV_WIDTH, A_WIDTH + 2 * KV_WIDTH, A_WIDTH + 2 * KV_WIDTH + B_WIDTH]
AB_IN = A_WIDTH + 2 * KV_WIDTH + 2 * B_WIDTH
C_HEADS = D_MODEL // HEAD_DIM
NA_ROWS = 8
NA_COLS = 16
D_FF = ((8 * D_MODEL // 3 + 255) // 256) * 256
FFN_CONV = 3
ALPHA = (2 * DEPTH) ** 0.25
BETA = (8 * DEPTH) ** -0.25
LN_EPS = 1e-6

kernel_name = "hybrid_gqa_conformer_natten_dit_block"


def layer_norm(x, g, b):
    xf = x.astype(jnp.float32)
    mu = jnp.mean(xf, -1, keepdims=True)
    var = jnp.mean(jnp.square(xf - mu), -1, keepdims=True)
    return ((xf - mu) * lax.rsqrt(var + LN_EPS) * g.astype(jnp.float32) + b.astype(jnp.float32)).astype(x.dtype)


def rms_norm(x, g):
    xf = x.astype(jnp.float32)
    return (xf * lax.rsqrt(jnp.mean(xf * xf, -1, keepdims=True) + LN_EPS) * g.astype(jnp.float32)).astype(x.dtype)


def modulate(h, shift, scale):
    return h * (1 + scale) + shift


def dwconv(x, w, b):
    pad = w.shape[0] // 2
    out = lax.conv_general_dilated(x, w[:, None, :], window_strides=(1,), padding=[(pad, pad)],
                                   dimension_numbers=('NWC', 'WIO', 'NWC'),
                                   feature_group_count=x.shape[-1])
    return out + b


def rope_1d(x, ang):
    n = ang.shape[-1]
    cos = jnp.cos(ang)[:, None, :]
    sin = jnp.sin(ang)[:, None, :]
    x1, x2 = x[..., :n], x[..., n:]
    return jnp.concatenate([x1 * cos - x2 * sin, x2 * cos + x1 * sin], -1)


def rope_2d(x):
    T = x.shape[1]
    t = jnp.arange(T)
    n = HEAD_DIM // 4
    inv = ROPE_THETA ** (-jnp.arange(n, dtype=jnp.float32) / n)
    ang_r = (t // GRID_W).astype(jnp.float32)[:, None] * inv
    ang_c = (t % GRID_W).astype(jnp.float32)[:, None] * inv
    xf = x.astype(jnp.float32)
    h = HEAD_DIM // 2
    return jnp.concatenate([rope_1d(xf[..., :h], ang_r), rope_1d(xf[..., h:], ang_c)], -1).astype(x.dtype)


def attend(q, k, v):
    s = jnp.einsum('bqkgd,bskd->bkgqs', q, k).astype(jnp.float32) * (q.shape[-1] ** -0.5)
    p = jax.nn.softmax(s, -1).astype(v.dtype)
    return jnp.einsum('bkgqs,bskd->bqkgd', p, v)


def conformer_conv(u, g, dw_w, dw_b, n_g, n_b):
    h = u * jax.nn.sigmoid(g)
    h = dwconv(h, dw_w, dw_b)
    return jax.nn.silu(layer_norm(h, n_g, n_b))


def conv_ffn(h, w_up, dw_w, dw_b, w_down):
    z = dwconv(h @ w_up, dw_w, dw_b)
    a, g = jnp.split(z, 2, axis=-1)
    return (jax.nn.silu(g) * a) @ w_down


def mixer_ab(hx, hc, w_in, w_out, q_gain, k_gain, dw_w, dw_b, n_g, n_b, ctx_out):
    B, T, _ = hx.shape
    L = hc.shape[1]
    G = A_HEADS // A_KV_HEADS
    qx, kx, vx, ux, gx = jnp.split(hx @ w_in, AB_SPLITS, axis=-1)
    qx = rope_2d(rms_norm(qx.reshape(B, T, A_HEADS, HEAD_DIM), q_gain))
    kx = rope_2d(rms_norm(kx.reshape(B, T, A_KV_HEADS, HEAD_DIM), k_gain))
    vx = vx.reshape(B, T, A_KV_HEADS, HEAD_DIM)
    if ctx_out:
        qc, kc, vc, uc, gc = jnp.split(hc @ w_in, AB_SPLITS, axis=-1)
    else:
        kc, vc = jnp.split(hc @ w_in[:, AB_SPLITS[0]:AB_SPLITS[2]], 2, axis=-1)
    kc = rms_norm(kc.reshape(B, L, A_KV_HEADS, HEAD_DIM), k_gain)
    vc = vc.reshape(B, L, A_KV_HEADS, HEAD_DIM)
    k_all = jnp.concatenate([kx, kc], axis=1)
    v_all = jnp.concatenate([vx, vc], axis=1)
    qb = qx.reshape(B, T // Q_BLOCK, Q_BLOCK, A_KV_HEADS, G, HEAD_DIM).swapaxes(0, 1)
    ox = lax.map(lambda qi: attend(qi, k_all, v_all), qb)
    ox = ox.swapaxes(0, 1).reshape(B, T, A_WIDTH)
    cx = conformer_conv(ux, gx, dw_w, dw_b, n_g, n_b)
    yx = jnp.concatenate([ox, cx], axis=-1) @ w_out
    if not ctx_out:
        return yx, None
    qc = rms_norm(qc.reshape(B, L, A_HEADS, HEAD_DIM), q_gain).reshape(B, L, A_KV_HEADS, G, HEAD_DIM)
    oc = attend(qc, kc, vc).reshape(B, L, A_WIDTH)
    cc = conformer_conv(uc, gc, dw_w, dw_b, n_g, n_b)
    yc = jnp.concatenate([oc, cc], axis=-1) @ w_out
    return yx, yc


def neighbourhood_attn(q, k, v, kc, vc, rel_bias):
    B, T, H, D = q.shape
    rows = T // GRID_W
    wr = min(NA_ROWS, rows)
    wc = NA_COLS
    qg = q.reshape(B, rows, GRID_W, H, D)
    kg = k.reshape(B, rows, GRID_W, H, D)
    vg = v.reshape(B, rows, GRID_W, H, D)
    col = jnp.arange(GRID_W)
    col_start = jnp.clip(col - wc // 2, 0, GRID_W - wc)
    col_idx = col_start[:, None] + jnp.arange(wc)
    dc = col_idx - col[:, None] + (NA_COLS - 1)
    scale = D ** -0.5
    n_loc = wr * wc

    def row_block(args):
        r, qr = args
        r0 = jnp.clip(r - wr // 2, 0, rows - wr)
        kb = lax.dynamic_slice_in_dim(kg, r0, wr, axis=1)
        vb = lax.dynamic_slice_in_dim(vg, r0, wr, axis=1)
        kw = kb[:, :, col_idx]
        vw = vb[:, :, col_idx]
        s_loc = jnp.einsum('bqhd,brqwhd->bhqrw', qr, kw).astype(jnp.float32) * scale
        dr = r0 + jnp.arange(wr) - r + (NA_ROWS - 1)
        bias = rel_bias[:, dr[:, None, None], dc[None, :, :]]
        s_loc = s_loc + jnp.transpose(bias, (0, 2, 1, 3))[None].astype(jnp.float32)
        s_ctx = jnp.einsum('bqhd,bshd->bhqs', qr, kc).astype(jnp.float32) * scale
        s = jnp.concatenate([s_loc.reshape(B, H, GRID_W, n_loc), s_ctx], axis=-1)
        p = jax.nn.softmax(s, -1).astype(v.dtype)
        p_loc = p[..., :n_loc].reshape(B, H, GRID_W, wr, wc)
        return (jnp.einsum('bhqrw,brqwhd->bqhd', p_loc, vw)
                + jnp.einsum('bhqs,bshd->bqhd', p[..., n_loc:], vc))

    o = lax.map(row_block, (jnp.arange(rows), qg.swapaxes(0, 1)))
    return o.swapaxes(0, 1).reshape(B, T, H * D)


def mixer_c(hx, hc, w_in, w_out, rel_bias, ctx_out):
    B, T, _ = hx.shape
    L = hc.shape[1]
    qx, kx, vx = [a.reshape(B, T, C_HEADS, HEAD_DIM) for a in jnp.split(hx @ w_in, 3, axis=-1)]
    if ctx_out:
        qc, kc, vc = [a.reshape(B, L, C_HEADS, HEAD_DIM) for a in jnp.split(hc @ w_in, 3, axis=-1)]
    else:
        kc, vc = [a.reshape(B, L, C_HEADS, HEAD_DIM) for a in jnp.split(hc @ w_in[:, D_MODEL:], 2, axis=-1)]
    yx = neighbourhood_attn(qx, kx, vx, kc, vc, rel_bias) @ w_out
    if not ctx_out:
        return yx, None
    oc = attend(qc[:, :, :, None, :], kc, vc).reshape(B, L, D_MODEL)
    return yx, oc @ w_out


def setup_inputs(seed: int = 0) -> dict:
    key = jax.random.key(seed)
    ks = jax.random.split(key, 24)
    n_ab = (DEPTH + 1) // 2
    n_c = DEPTH // 2
    f32 = jnp.float32

    def nrm(k, shape, s):
        return jax.random.normal(k, shape, f32) * s

    d_in = D_MODEL ** -0.5
    ab_col_scale = jnp.ones((AB_IN,), f32).at[AB_SPLITS[1]:AB_SPLITS[2]].set(BETA)
    c_col_scale = jnp.ones((3 * D_MODEL,), f32).at[2 * D_MODEL:].set(BETA)
    return {
        "x": nrm(ks[0], (BATCH, SEQ, D_MODEL), 1.0),
        "c": nrm(ks[1], (BATCH, D_MODEL), 1.0),
        "ctx": nrm(ks[2], (BATCH, CTX_LEN, D_MODEL), 1.0),
        "c_ctx": nrm(ks[3], (D_MODEL,), 1.0),
        "ada_w": nrm(ks[4], (DEPTH, D_MODEL, 6 * D_MODEL), d_in),
        "ada_b": nrm(ks[5], (DEPTH, 6 * D_MODEL), 0.01),
        "post_ln_g": 1.0 + nrm(ks[6], (DEPTH, 2, D_MODEL), 0.05),
        "post_ln_b": nrm(ks[7], (DEPTH, 2, D_MODEL), 0.01),
        "ab_w_in": nrm(ks[8], (n_ab, D_MODEL, AB_IN), d_in) * ab_col_scale,
        "ab_w_out": nrm(ks[9], (n_ab, A_WIDTH + B_WIDTH, D_MODEL), (A_WIDTH + B_WIDTH) ** -0.5 * BETA),
        "ab_q_gain": 1.0 + nrm(ks[10], (n_ab, HEAD_DIM), 0.05),
        "ab_k_gain": 1.0 + nrm(ks[11], (n_ab, HEAD_DIM), 0.05),
        "ab_dw_w": nrm(ks[12], (n_ab, B_CONV, B_WIDTH), B_CONV ** -0.5),
        "ab_dw_b": nrm(ks[13], (n_ab, B_WIDTH), 0.01),
        "ab_norm_g": 1.0 + nrm(ks[14], (n_ab, B_WIDTH), 0.05),
        "ab_norm_b": nrm(ks[15], (n_ab, B_WIDTH), 0.01),
        "c_w_in": nrm(ks[16], (n_c, D_MODEL, 3 * D_MODEL), d_in) * c_col_scale,
        "c_w_out": nrm(ks[17], (n_c, D_MODEL, D_MODEL), d_in * BETA),
        "c_rel_bias": nrm(ks[18], (n_c, C_HEADS, 2 * NA_ROWS - 1, 2 * NA_COLS - 1), 0.1),
        "ffn_w_up": nrm(ks[19], (DEPTH, D_MODEL, 2 * D_FF), d_in),
        "ffn_dw_w": nrm(ks[20], (DEPTH, FFN_CONV, 2 * D_FF), FFN_CONV ** -0.5),
        "ffn_dw_b": nrm(ks[21], (DEPTH, 2 * D_FF), 0.01),
        "ffn_w_down": nrm(ks[22], (DEPTH, D_FF, D_MODEL), D_FF ** -0.5 * BETA),
    }


def reference(x, c, ctx, c_ctx, ada_w, ada_b, post_ln_g, post_ln_b,
              ab_w_in, ab_w_out, ab_q_gain, ab_k_gain, ab_dw_w, ab_dw_b, ab_norm_g, ab_norm_b,
              c_w_in, c_w_out, c_rel_bias,
              ffn_w_up, ffn_dw_w, ffn_dw_b, ffn_w_down):
    cond_x = jax.nn.silu(c)[:, None, :]
    cond_c = jax.nn.silu(c_ctx)[None, None, :]
    for i in range(DEPTH):
        ctx_out = i < DEPTH - 1
        j = i // 2
        mx = jnp.split(cond_x @ ada_w[i] + ada_b[i], 6, axis=-1)
        mc = jnp.split(cond_c @ ada_w[i] + ada_b[i], 6, axis=-1)
        hx = modulate(x, mx[0], mx[1])
        hc = modulate(ctx, mc[0], mc[1])
        if i % 2 == 0:
            yx, yc = mixer_ab(hx, hc, ab_w_in[j], ab_w_out[j], ab_q_gain[j], ab_k_gain[j],
                              ab_dw_w[j], ab_dw_b[j], ab_norm_g[j], ab_norm_b[j], ctx_out)
        else:
            yx, yc = mixer_c(hx, hc, c_w_in[j], c_w_out[j], c_rel_bias[j], ctx_out)
        x = layer_norm(ALPHA * x + mx[2] * yx, post_ln_g[i, 0], post_ln_b[i, 0])
        fx = conv_ffn(modulate(x, mx[3], mx[4]), ffn_w_up[i], ffn_dw_w[i], ffn_dw_b[i], ffn_w_down[i])
        x = layer_norm(ALPHA * x + mx[5] * fx, post_ln_g[i, 1], post_ln_b[i, 1])
        if ctx_out:
            ctx = layer_norm(ALPHA * ctx + mc[2] * yc, post_ln_g[i, 0], post_ln_b[i, 0])
            fc = conv_ffn(modulate(ctx, mc[3], mc[4]), ffn_w_up[i], ffn_dw_w[i], ffn_dw_b[i], ffn_w_down[i])
            ctx = layer_norm(ALPHA * ctx + mc[5] * fc, post_ln_g[i, 1], post_ln_b[i, 1])
    return x
```

```python
import functools

import jax
import jax.numpy as jnp
import numpy as np
from jax import lax
from jax.experimental import pallas as pl
from jax.experimental.pallas import tpu as pltpu

F32 = jnp.float32
BF16 = jnp.bfloat16

HEAD_DIM = 128
GRID_W = 64
A_HEADS = 8
A_KV_HEADS = 2
A_GROUP = A_HEADS // A_KV_HEADS
ROPE_THETA = 10000.0
NA_ROWS = 8
NA_COLS = 16
LN_EPS = 1e-6
NEG = -1e30

V7X_LANES = 128
V7X_BF16_SUBLANES = 16
V7X_VMEM_LIMIT = 56 * 1024 * 1024
COND_ROWS = 8

NA_QROWS = 4
NA_KROWS = NA_QROWS + NA_ROWS


def _pick(n, candidates):
    for t in candidates:
        if all(v % t == 0 for v in n):
            return t
    raise ValueError(f"no tile in {candidates} divides {n}")


def _params(*sem):
    return pltpu.CompilerParams(dimension_semantics=sem, vmem_limit_bytes=V7X_VMEM_LIMIT)


def _layer_norm(r, g, b):
    mu = jnp.mean(r, axis=-1, keepdims=True)
    d = r - mu
    var = jnp.mean(d * d, axis=-1, keepdims=True)
    return d * lax.rsqrt(var + LN_EPS) * g + b


def _sigmoid(v):
    return 1.0 / (1.0 + jnp.exp(-v))


def _dot(a, b):
    return jnp.dot(a, b, preferred_element_type=F32)


def _dot_t(a, b):
    return lax.dot_general(a, b, (((1,), (1,)), ((), ())), preferred_element_type=F32)


def _ada_kernel(cond_ref, w_ref, b_ref, o_ref):
    cnd = cond_ref[...]
    s = (cnd * _sigmoid(cnd)).astype(BF16)
    o_ref[0] = _dot(s, w_ref[0].astype(BF16)) + b_ref[0]


def _ada(cond, ada_w, ada_b):
    depth, d, n6 = ada_w.shape
    tn = _pick((n6,), (1024, 512, 256, 128))
    return pl.pallas_call(
        _ada_kernel,
        grid=(depth, n6 // tn),
        in_specs=[
            pl.BlockSpec((COND_ROWS, d), lambda l, j: (0, 0)),
            pl.BlockSpec((1, d, tn), lambda l, j: (l, 0, j)),
            pl.BlockSpec((1, 1, tn), lambda l, j: (l, 0, j)),
        ],
        out_specs=pl.BlockSpec((1, COND_ROWS, tn), lambda l, j: (l, 0, j)),
        out_shape=jax.ShapeDtypeStruct((depth, COND_ROWS, n6), F32),
        compiler_params=_params("arbitrary", "arbitrary"),
        name="ada_mod",
    )(cond, ada_w, ada_b.reshape(depth, 1, n6))


def _inproj_ab_kernel(x_ref, sh_ref, sc_ref, w_ref, qg_ref, kg_ref, rope_ref, qkv_ref, glu_ref, *, q_scale):
    hb = (x_ref[...] * (1.0 + sc_ref[0]) + sh_ref[0]).astype(BF16)
    cos, sin_lo, sin_hi = rope_ref[0], rope_ref[1], rope_ref[2]
    a_w = A_HEADS * HEAD_DIM
    kv_w = A_KV_HEADS * HEAD_DIM
    b_w = glu_ref.shape[1]

    def norm_rope(z, gain):
        zn = z * lax.rsqrt(jnp.mean(z * z, axis=-1, keepdims=True) + LN_EPS) * gain
        quarter = HEAD_DIM // 4
        return (zn * cos + pltpu.roll(zn, HEAD_DIM - quarter, 1) * sin_lo
                + pltpu.roll(zn, quarter, 1) * sin_hi)

    q_gain = qg_ref[...] * q_scale
    heads_per_dot = 4
    for c0 in range(0, a_w, heads_per_dot * HEAD_DIM):
        z = _dot(hb, w_ref[:, c0:c0 + heads_per_dot * HEAD_DIM])
        for hh in range(heads_per_dot):
            cs = slice(c0 + hh * HEAD_DIM, c0 + (hh + 1) * HEAD_DIM)
            qkv_ref[:, cs] = norm_rope(z[:, hh * HEAD_DIM:(hh + 1) * HEAD_DIM], q_gain).astype(BF16)
    z = _dot(hb, w_ref[:, a_w:a_w + 2 * kv_w])
    for hh in range(A_KV_HEADS):
        cs = slice(a_w + hh * HEAD_DIM, a_w + (hh + 1) * HEAD_DIM)
        qkv_ref[:, cs] = norm_rope(z[:, hh * HEAD_DIM:(hh + 1) * HEAD_DIM], kg_ref[...]).astype(BF16)
    qkv_ref[:, a_w + kv_w:a_w + 2 * kv_w] = z[:, kv_w:].astype(BF16)
    u0 = a_w + 2 * kv_w
    glu_cols = 512 if b_w % 512 == 0 else b_w
    for c0 in range(0, b_w, glu_cols):
        u = _dot(hb, w_ref[:, u0 + c0:u0 + c0 + glu_cols])
        g = _dot(hb, w_ref[:, u0 + b_w + c0:u0 + b_w + c0 + glu_cols])
        glu_ref[:, c0:c0 + glu_cols] = u * _sigmoid(g)


def _inproj_ab(xc, mod3, mod_base, w_in, q_gain, k_gain, rope, *, tm, n_lat_tiles, seq, nbatch):
    n, d = xc.shape
    n_in = w_in.shape[1]
    a_w, kv_w = A_HEADS * HEAD_DIM, A_KV_HEADS * HEAD_DIM
    b_w = (n_in - a_w - 2 * kv_w) // 2
    tiles_per_seq = seq // tm

    def cond_row(i):
        return jnp.where(i < n_lat_tiles, (i * tm) // seq, nbatch)

    def rope_blk(i):
        return jnp.where(i < n_lat_tiles, i % tiles_per_seq, tiles_per_seq)

    return pl.pallas_call(
        functools.partial(_inproj_ab_kernel, q_scale=HEAD_DIM ** -0.5),
        grid=(n // tm,),
        in_specs=[
            pl.BlockSpec((tm, d), lambda i: (i, 0)),
            pl.BlockSpec((1, 1, d), lambda i: (mod_base + cond_row(i) * 6 + 0, 0, 0)),
            pl.BlockSpec((1, 1, d), lambda i: (mod_base + cond_row(i) * 6 + 1, 0, 0)),
            pl.BlockSpec((d, n_in), lambda i: (0, 0), pipeline_mode=pl.Buffered(1)),
            pl.BlockSpec((1, HEAD_DIM), lambda i: (0, 0)),
            pl.BlockSpec((1, HEAD_DIM), lambda i: (0, 0)),
            pl.BlockSpec((3, tm, HEAD_DIM), lambda i: (0, rope_blk(i), 0)),
        ],
        out_specs=[
            pl.BlockSpec((tm, a_w + 2 * kv_w), lambda i: (i, 0)),
            pl.BlockSpec((tm, b_w), lambda i: (i, 0)),
        ],
        out_shape=[
            jax.ShapeDtypeStruct((n, a_w + 2 * kv_w), BF16),
            jax.ShapeDtypeStruct((n, b_w), F32),
        ],
        compiler_params=_params("arbitrary"),
        name="inproj_ab",
    )(xc, mod3, mod3, w_in, q_gain, k_gain, rope)


def _rope_tables(seq, tm):
    t = jnp.arange(seq)
    nfreq = HEAD_DIM // 4
    inv = ROPE_THETA ** (-jnp.arange(nfreq, dtype=F32) / nfreq)
    ang_r = (t // GRID_W).astype(F32)[:, None] * inv
    ang_c = (t % GRID_W).astype(F32)[:, None] * inv
    ang = jnp.concatenate([ang_r, ang_r, ang_c, ang_c], axis=-1)
    cos, sin = jnp.cos(ang), jnp.sin(ang)
    low = (np.arange(HEAD_DIM) // nfreq) % 2 == 0
    sin_lo = jnp.where(low, -sin, 0.0)
    sin_hi = jnp.where(low, 0.0, sin)
    tab = jnp.stack([cos, sin_lo, sin_hi])
    ident = jnp.stack([jnp.ones((tm, HEAD_DIM), F32), jnp.zeros((tm, HEAD_DIM), F32), jnp.zeros((tm, HEAD_DIM), F32)])
    return jnp.concatenate([tab, ident], axis=1)


def _softmax_pv(q, parts):
    ss = [_dot_t(q, k_ref[...]) for k_ref, _ in parts]
    m = functools.reduce(jnp.maximum, [jnp.max(s, axis=-1, keepdims=True) for s in ss])
    ps = [jnp.exp(s - m) for s in ss]
    l = functools.reduce(jnp.add, [jnp.sum(p, axis=-1, keepdims=True) for p in ps])
    o = functools.reduce(jnp.add, [_dot(p.astype(BF16), v_ref[...]) for p, (_, v_ref) in zip(ps, parts)])
    return o / l


def _gqa_kernel(q_ref, kx_ref, vx_ref, kc_ref, vc_ref, o_ref, *, n_lat_tiles):
    qi = pl.program_id(1)

    def run(parts):
        for g in range(A_GROUP):
            cs = slice(g * HEAD_DIM, (g + 1) * HEAD_DIM)
            o_ref[:, cs] = _softmax_pv(q_ref[:, cs], parts).astype(BF16)

    @pl.when(qi < n_lat_tiles)
    def _():
        run([(kx_ref, vx_ref), (kc_ref, vc_ref)])

    @pl.when(qi >= n_lat_tiles)
    def _():
        run([(kc_ref, vc_ref)])


def _gqa(qkv, *, nbatch, seq, ctx_len):
    n = qkv.shape[0]
    tq = _pick((seq, ctx_len), (256, 128))
    n_lat_tiles = nbatch * seq // tq
    lat_per_seq, ctx_per_seq = seq // tq, ctx_len // tq
    a_w = A_HEADS * HEAD_DIM
    gw = A_GROUP * HEAD_DIM
    k_col0 = a_w // HEAD_DIM
    v_col0 = k_col0 + A_KV_HEADS

    def batch(qi):
        return jnp.where(qi < n_lat_tiles, qi // lat_per_seq, (qi - n_lat_tiles) // ctx_per_seq)

    def lat_batch(qi):
        return jnp.minimum(qi // lat_per_seq, nbatch - 1)

    ctx_blk0 = nbatch * seq // ctx_len
    return pl.pallas_call(
        functools.partial(_gqa_kernel, n_lat_tiles=n_lat_tiles),
        grid=(A_KV_HEADS, n // tq),
        in_specs=[
            pl.BlockSpec((tq, gw), lambda h, qi: (qi, h)),
            pl.BlockSpec((seq, HEAD_DIM), lambda h, qi: (lat_batch(qi), k_col0 + h)),
            pl.BlockSpec((seq, HEAD_DIM), lambda h, qi: (lat_batch(qi), v_col0 + h)),
            pl.BlockSpec((ctx_len, HEAD_DIM), lambda h, qi: (ctx_blk0 + batch(qi), k_col0 + h)),
            pl.BlockSpec((ctx_len, HEAD_DIM), lambda h, qi: (ctx_blk0 + batch(qi), v_col0 + h)),
        ],
        out_specs=pl.BlockSpec((tq, gw), lambda h, qi: (qi, h)),
        out_shape=jax.ShapeDtypeStruct((n, a_w), BF16),
        compiler_params=_params("arbitrary", "arbitrary"),
        name="gqa_attn",
    )(qkv, qkv, qkv, qkv, qkv)


def _conv_kernel(x_ref, xp_ref, xn_ref, dw_ref, db_ref, g_ref, b_ref, o_ref, buf_ref, y_ref, *,
                 tt, halo, n_lat_tiles, lat_per_seq, ctx_per_seq):
    i = pl.program_id(0)
    taps = dw_ref.shape[0]
    pad = taps // 2
    idx = jnp.where(i < n_lat_tiles, i % lat_per_seq, (i - n_lat_tiles) % ctx_per_seq)
    per_seq = jnp.where(i < n_lat_tiles, lat_per_seq, ctx_per_seq)
    buf_ref[0:halo, :] = jnp.where(idx > 0, xp_ref[...], 0.0)
    buf_ref[halo:halo + tt, :] = x_ref[...]
    buf_ref[halo + tt:, :] = jnp.where(idx < per_seq - 1, xn_ref[...], 0.0)
    nchunk = x_ref.shape[1] // V7X_LANES

    def chunk(c, carry):
        cs = pl.ds(pl.multiple_of(c * V7X_LANES, V7X_LANES), V7X_LANES)
        acc = jnp.broadcast_to(db_ref[:, cs], (tt, V7X_LANES))
        for k in range(taps):
            acc = acc + buf_ref[pl.ds(halo - pad + k, tt), cs] * dw_ref[pl.ds(k, 1), cs]
        y_ref[:, cs] = acc
        return carry

    lax.fori_loop(0, nchunk, chunk, 0)
    yn = _layer_norm(y_ref[...], g_ref[...], b_ref[...])
    o_ref[...] = (yn * _sigmoid(yn)).astype(BF16)


def _conformer_conv(glu, dw_w, dw_b, n_g, n_b, *, nbatch, seq, ctx_len):
    n, cw = glu.shape
    tt = _pick((seq, ctx_len), (256, 128))
    halo = 16
    assert dw_w.shape[0] // 2 <= halo
    n_lat_tiles = nbatch * seq // tt
    hb = tt // halo
    nhalo = n // halo
    return pl.pallas_call(
        functools.partial(_conv_kernel, tt=tt, halo=halo, n_lat_tiles=n_lat_tiles,
                          lat_per_seq=seq // tt, ctx_per_seq=ctx_len // tt),
        grid=(n // tt,),
        in_specs=[
            pl.BlockSpec((tt, cw), lambda i: (i, 0)),
            pl.BlockSpec((halo, cw), lambda i: (jnp.maximum(i * hb - 1, 0), 0)),
            pl.BlockSpec((halo, cw), lambda i: (jnp.minimum((i + 1) * hb, nhalo - 1), 0)),
            pl.BlockSpec(dw_w.shape, lambda i: (0, 0)),
            pl.BlockSpec((1, cw), lambda i: (0, 0)),
            pl.BlockSpec((1, cw), lambda i: (0, 0)),
            pl.BlockSpec((1, cw), lambda i: (0, 0)),
        ],
        out_specs=pl.BlockSpec((tt, cw), lambda i: (i, 0)),
        out_shape=jax.ShapeDtypeStruct((n, cw), BF16),
        scratch_shapes=[pltpu.VMEM((tt + 2 * halo, cw), F32), pltpu.VMEM((tt, cw), F32)],
        compiler_params=_params("arbitrary"),
        name="conformer_conv",
    )(glu, glu, glu, dw_w, dw_b.reshape(1, cw), n_g.reshape(1, cw), n_b.reshape(1, cw))


def _outproj_kernel(*refs, n_parts, alpha):
    part_refs = refs[:n_parts]
    w_ref, x_ref, gate_ref, sh_ref, sc_ref, lg_ref, lb_ref, x1_ref, h2_ref = refs[n_parts:]
    y = None
    k0 = 0
    for p_ref in part_refs:
        kw = p_ref.shape[1]
        t = _dot(p_ref[...], w_ref[k0:k0 + kw, :])
        y = t if y is None else y + t
        k0 += kw
    x1 = _layer_norm(alpha * x_ref[...] + gate_ref[0] * y, lg_ref[...], lb_ref[...])
    x1_ref[...] = x1
    h2_ref[...] = (x1 * (1.0 + sc_ref[0]) + sh_ref[0]).astype(BF16)


def _outproj(parts, w_out, xc, mod3, mod_base, ln_g, ln_b, *, n_out, tm, n_lat_tiles, seq, nbatch, alpha):
    d = xc.shape[1]

    def cond_row(i):
        return jnp.where(i < n_lat_tiles, (i * tm) // seq, nbatch)

    def mod_spec(k):
        return pl.BlockSpec((1, 1, d), lambda i: (mod_base + cond_row(i) * 6 + k, 0, 0))

    return pl.pallas_call(
        functools.partial(_outproj_kernel, n_parts=len(parts), alpha=alpha),
        grid=(n_out // tm,),
        in_specs=[pl.BlockSpec((tm, p.shape[1]), lambda i: (i, 0)) for p in parts] + [
            pl.BlockSpec(w_out.shape, lambda i: (0, 0), pipeline_mode=pl.Buffered(1)),
            pl.BlockSpec((tm, d), lambda i: (i, 0)),
            mod_spec(2), mod_spec(3), mod_spec(4),
            pl.BlockSpec((1, d), lambda i: (0, 0)),
            pl.BlockSpec((1, d), lambda i: (0, 0)),
        ],
        out_specs=[pl.BlockSpec((tm, d), lambda i: (i, 0)), pl.BlockSpec((tm, d), lambda i: (i, 0))],
        out_shape=[jax.ShapeDtypeStruct((n_out, d), F32), jax.ShapeDtypeStruct((n_out, d), BF16)],
        compiler_params=_params("arbitrary"),
        name="outproj_ln",
    )(*parts, w_out, xc, mod3, mod3, mod3, ln_g.reshape(1, d), ln_b.reshape(1, d))


def _ffn_kernel(h_ref, hp_ref, hn_ref, wa_ref, wg_ref, dwa_ref, dwg_ref, dba_ref, dbg_ref, wd_ref,
                x_ref, gate_ref, lg_ref, lb_ref, o_ref, acc_ref, *, tm, halo, seq, ctx_len, n_lat_tiles, alpha):
    i = pl.program_id(0)
    c = pl.program_id(1)
    seqlen = jnp.where(i < n_lat_tiles, seq, ctx_len)
    row = lax.broadcasted_iota(jnp.int32, (tm, 1), 0) + i * tm
    pos = row & (seqlen - 1)
    has_prev = pos != 0
    has_next = pos != seqlen - 1
    hext = jnp.concatenate([hp_ref[...], h_ref[...], hn_ref[...]], axis=0)
    rows_ext = tm + 2 * halo

    def branch(w_ref, dw_ref, db_ref):
        z = _dot(hext, w_ref[...])
        z_prev = jnp.where(has_prev, pltpu.roll(z, 1, 0)[halo:halo + tm], 0.0)
        z_next = jnp.where(has_next, pltpu.roll(z, rows_ext - 1, 0)[halo:halo + tm], 0.0)
        return z_prev * dw_ref[0:1, :] + z[halo:halo + tm] * dw_ref[1:2, :] + z_next * dw_ref[2:3, :] + db_ref[...]

    a = branch(wa_ref, dwa_ref, dba_ref)
    g = branch(wg_ref, dwg_ref, dbg_ref)
    contrib = _dot((g * _sigmoid(g) * a).astype(BF16), wd_ref[...])

    @pl.when(c == 0)
    def _():
        acc_ref[...] = contrib

    @pl.when(c > 0)
    def _():
        acc_ref[...] += contrib

    @pl.when(c == pl.num_programs(1) - 1)
    def _():
        o_ref[...] = _layer_norm(alpha * x_ref[...] + gate_ref[0] * acc_ref[...], lg_ref[...], lb_ref[...])


def _ffn(h2, x1, w_up, dw_w, dw_b, w_down, mod3, mod_base, ln_g, ln_b, *, tm, n_lat_tiles, seq, ctx_len,
         nbatch, alpha):
    n_out, d = x1.shape
    d_ff = w_down.shape[0]
    tf = _pick((d_ff,), (512, 256, 128))
    nf = d_ff // tf
    halo = V7X_BF16_SUBLANES
    hb = tm // halo
    nhalo = h2.shape[0] // halo
    assert dw_w.shape[0] == 3

    def cond_row(i):
        return jnp.where(i < n_lat_tiles, (i * tm) // seq, nbatch)

    dw_b2 = dw_b.reshape(1, 2 * d_ff)
    return pl.pallas_call(
        functools.partial(_ffn_kernel, tm=tm, halo=halo, seq=seq, ctx_len=ctx_len, n_lat_tiles=n_lat_tiles,
                          alpha=alpha),
        grid=(n_out // tm, nf),
        in_specs=[
            pl.BlockSpec((tm, d), lambda i, c: (i, 0)),
            pl.BlockSpec((halo, d), lambda i, c: (jnp.maximum(i * hb - 1, 0), 0)),
            pl.BlockSpec((halo, d), lambda i, c: (jnp.minimum((i + 1) * hb, nhalo - 1), 0)),
            pl.BlockSpec((d, tf), lambda i, c: (0, c)),
            pl.BlockSpec((d, tf), lambda i, c: (0, nf + c)),
            pl.BlockSpec((3, tf), lambda i, c: (0, c)),
            pl.BlockSpec((3, tf), lambda i, c: (0, nf + c)),
            pl.BlockSpec((1, tf), lambda i, c: (0, c)),
            pl.BlockSpec((1, tf), lambda i, c: (0, nf + c)),
            pl.BlockSpec((tf, d), lambda i, c: (c, 0)),
            pl.BlockSpec((tm, d), lambda i, c: (i, 0)),
            pl.BlockSpec((1, 1, d), lambda i, c: (mod_base + cond_row(i) * 6 + 5, 0, 0)),
            pl.BlockSpec((1, d), lambda i, c: (0, 0)),
            pl.BlockSpec((1, d), lambda i, c: (0, 0)),
        ],
        out_specs=pl.BlockSpec((tm, d), lambda i, c: (i, 0)),
        out_shape=jax.ShapeDtypeStruct((n_out, d), F32),
        scratch_shapes=[pltpu.VMEM((tm, d), F32)],
        compiler_params=_params("arbitrary", "arbitrary"),
        name="conv_ffn",
    )(h2, h2, h2, w_up, w_up, dw_w, dw_w, dw_b2, dw_b2, w_down, x1, mod3, ln_g.reshape(1, d), ln_b.reshape(1, d))


def _inproj_c_kernel(x_ref, sh_ref, sc_ref, w_ref, o_ref, *, q_scale):
    hb = (x_ref[...] * (1.0 + sc_ref[0]) + sh_ref[0]).astype(BF16)
    d = x_ref.shape[1]
    cols = 512
    for c0 in range(0, o_ref.shape[1], cols):
        z = _dot(hb, w_ref[:, c0:c0 + cols])
        if c0 < d:
            z = z * q_scale
        o_ref[:, c0:c0 + cols] = z.astype(BF16)


def _inproj_c(xc, mod3, mod_base, w_in, *, tm, n_lat_tiles, seq, nbatch):
    n, d = xc.shape
    n_in = w_in.shape[1]
    assert d % 512 == 0 and n_in == 3 * d

    def cond_row(i):
        return jnp.where(i < n_lat_tiles, (i * tm) // seq, nbatch)

    return pl.pallas_call(
        functools.partial(_inproj_c_kernel, q_scale=HEAD_DIM ** -0.5),
        grid=(n // tm,),
        in_specs=[
            pl.BlockSpec((tm, d), lambda i: (i, 0)),
            pl.BlockSpec((1, 1, d), lambda i: (mod_base + cond_row(i) * 6 + 0, 0, 0)),
            pl.BlockSpec((1, 1, d), lambda i: (mod_base + cond_row(i) * 6 + 1, 0, 0)),
            pl.BlockSpec((d, n_in), lambda i: (0, 0), pipeline_mode=pl.Buffered(1)),
        ],
        out_specs=pl.BlockSpec((tm, n_in), lambda i: (i, 0)),
        out_shape=jax.ShapeDtypeStruct((n, n_in), BF16),
        compiler_params=_params("arbitrary"),
        name="inproj_c",
    )(xc, mod3, mod3, w_in)


def _na_bias_table(rel_bias):
    nheads = rel_bias.shape[0]
    qc = np.arange(GRID_W)[:, None]
    kc = np.arange(GRID_W)[None, :]
    cstart = np.clip(qc - NA_COLS // 2, 0, GRID_W - NA_COLS)
    valid = (kc >= cstart) & (kc < cstart + NA_COLS)
    dc = np.clip(kc - qc + NA_COLS - 1, 0, 2 * NA_COLS - 2)
    toep = jnp.where(valid[None, None], rel_bias[:, :, dc], NEG)
    neg = jnp.full((nheads, 1, GRID_W, GRID_W), NEG, F32)
    ext = jnp.concatenate([neg, toep, neg], axis=1)
    return jnp.concatenate([ext[:, :2 * NA_ROWS], ext[:, 1:]], axis=-1)


def _natten_kernel(q_ref, k_ref, v_ref, kc_ref, vc_ref, bias_ref, o_ref, *, rows):
    nq = NA_QROWS * GRID_W
    nk = NA_KROWS * GRID_W
    left = lax.broadcasted_iota(jnp.int32, (GRID_W, 2 * GRID_W), 1) < GRID_W
    for blk in range(rows // NA_QROWS):
        ks = min(max(NA_QROWS * blk - NA_ROWS // 2, 0), rows - NA_KROWS)
        q = q_ref[blk * nq:(blk + 1) * nq, :]
        s_loc = _dot_t(q, k_ref[ks * GRID_W:ks * GRID_W + nk, :])
        bias_rows = []
        for qi in range(NA_QROWS):
            qr = NA_QROWS * blk + qi
            r0 = min(max(qr - NA_ROWS // 2, 0), rows - NA_ROWS)
            slabs = []
            for j in range(NA_KROWS // 2):
                kr = ks + 2 * j
                ok0 = r0 <= kr < r0 + NA_ROWS
                ok1 = r0 <= kr + 1 < r0 + NA_ROWS
                if not (ok0 or ok1):
                    slabs.append(jnp.full((GRID_W, 2 * GRID_W), NEG, F32))
                    continue
                slab = bias_ref[0, kr - qr + NA_ROWS]
                if not ok1:
                    slab = jnp.where(left, slab, NEG)
                elif not ok0:
                    slab = jnp.where(left, NEG, slab)
                slabs.append(slab)
            bias_rows.append(jnp.concatenate(slabs, axis=1))
        s_loc = s_loc + jnp.concatenate(bias_rows, axis=0)
        s_ctx = _dot_t(q, kc_ref[...])
        m = jnp.maximum(jnp.max(s_loc, axis=-1, keepdims=True), jnp.max(s_ctx, axis=-1, keepdims=True))
        p_loc = jnp.exp(s_loc - m)
        p_ctx = jnp.exp(s_ctx - m)
        l = jnp.sum(p_loc, axis=-1, keepdims=True) + jnp.sum(p_ctx, axis=-1, keepdims=True)
        o = (_dot(p_loc.astype(BF16), v_ref[ks * GRID_W:ks * GRID_W + nk, :])
             + _dot(p_ctx.astype(BF16), vc_ref[...]))
        o_ref[blk * nq:(blk + 1) * nq, :] = (o / l).astype(BF16)


def _natten(qkv, bias_tab, *, nbatch, seq, ctx_len, d):
    rows = seq // GRID_W
    nheads = d // HEAD_DIM
    assert rows % NA_QROWS == 0 and rows >= NA_KROWS
    ctx_blk0 = nbatch * seq // ctx_len
    return pl.pallas_call(
        functools.partial(_natten_kernel, rows=rows),
        grid=(nheads, nbatch),
        in_specs=[
            pl.BlockSpec((seq, HEAD_DIM), lambda h, b: (b, h)),
            pl.BlockSpec((seq, HEAD_DIM), lambda h, b: (b, nheads + h)),
            pl.BlockSpec((seq, HEAD_DIM), lambda h, b: (b, 2 * nheads + h)),
            pl.BlockSpec((ctx_len, HEAD_DIM), lambda h, b: (ctx_blk0 + b, nheads + h)),
            pl.BlockSpec((ctx_len, HEAD_DIM), lambda h, b: (ctx_blk0 + b, 2 * nheads + h)),
            pl.BlockSpec((1,) + bias_tab.shape[1:], lambda h, b: (h, 0, 0, 0)),
        ],
        out_specs=pl.BlockSpec((seq, HEAD_DIM), lambda h, b: (b, h)),
        out_shape=jax.ShapeDtypeStruct((nbatch * seq, d), BF16),
        compiler_params=_params("arbitrary", "arbitrary"),
        name="natten",
    )(qkv, qkv, qkv, qkv, qkv, bias_tab)


def kernel(x, c, ctx, c_ctx, ada_w, ada_b, post_ln_g, post_ln_b, ab_w_in, ab_w_out, ab_q_gain, ab_k_gain,
           ab_dw_w, ab_dw_b, ab_norm_g, ab_norm_b, c_w_in, c_w_out, c_rel_bias, ffn_w_up, ffn_dw_w, ffn_dw_b,
           ffn_w_down):
    nbatch, seq, d = x.shape
    ctx_len = ctx.shape[1]
    depth = ada_w.shape[0]
    assert depth == 2 and nbatch + 1 <= COND_ROWS and nbatch * ctx_len <= seq
    assert seq % GRID_W == 0 and seq & (seq - 1) == 0 and ctx_len & (ctx_len - 1) == 0 and ctx_len <= seq
    nx, nc = nbatch * seq, nbatch * ctx_len
    alpha = (2 * depth) ** 0.25
    tm = _pick((seq, nc), (512, 256))
    n_lat_tiles = nx // tm
    geo = dict(tm=tm, n_lat_tiles=n_lat_tiles, seq=seq, nbatch=nbatch)

    cond = jnp.concatenate([c, c_ctx[None], jnp.zeros((COND_ROWS - nbatch - 1, d), F32)], axis=0)
    mod3 = _ada(cond, ada_w, ada_b).reshape(depth * COND_ROWS * 6, 1, d)
    xc = jnp.concatenate([x.reshape(nx, d), ctx.reshape(nc, d)], axis=0)

    mod_base = 0
    qkv, glu = _inproj_ab(xc, mod3, mod_base, ab_w_in[0].astype(BF16), ab_q_gain[0][None], ab_k_gain[0][None],
                          _rope_tables(seq, tm), **geo)
    attn = _gqa(qkv, nbatch=nbatch, seq=seq, ctx_len=ctx_len)
    conv = _conformer_conv(glu, ab_dw_w[0], ab_dw_b[0], ab_norm_g[0], ab_norm_b[0],
                           nbatch=nbatch, seq=seq, ctx_len=ctx_len)
    x1, h2 = _outproj([attn, conv], ab_w_out[0].astype(BF16), xc, mod3, mod_base, post_ln_g[0, 0],
                      post_ln_b[0, 0], n_out=nx + nc, alpha=alpha, **geo)
    xc = _ffn(h2, x1, ffn_w_up[0].astype(BF16), ffn_dw_w[0], ffn_dw_b[0], ffn_w_down[0].astype(BF16), mod3,
              mod_base, post_ln_g[0, 1], post_ln_b[0, 1], ctx_len=ctx_len, alpha=alpha, **geo)

    mod_base = COND_ROWS * 6
    qkv = _inproj_c(xc, mod3, mod_base, c_w_in[0].astype(BF16), **geo)
    attn = _natten(qkv, _na_bias_table(c_rel_bias[0]), nbatch=nbatch, seq=seq, ctx_len=ctx_len, d=d)
    x1, h2 = _outproj([attn], c_w_out[0].astype(BF16), xc, mod3, mod_base, post_ln_g[1, 0], post_ln_b[1, 0],
                      n_out=nx, alpha=alpha, **geo)
    out = _ffn(h2, x1, ffn_w_up[1].astype(BF16), ffn_dw_w[1], ffn_dw_b[1], ffn_w_down[1].astype(BF16), mod3,
               mod_base, post_ln_g[1, 1], post_ln_b[1, 1], ctx_len=ctx_len, alpha=alpha, **geo)
    return out.reshape(nbatch, seq, d)
```

```python
import functools

import jax
import jax.numpy as jnp
import numpy as np
from jax import lax
from jax.experimental import pallas as pl
from jax.experimental.pallas import tpu as pltpu

F32 = jnp.float32
BF16 = jnp.bfloat16

HEAD_DIM = 128
GRID_W = 64
A_HEADS = 8
A_KV_HEADS = 2
A_GROUP = A_HEADS // A_KV_HEADS
ROPE_THETA = 10000.0
NA_ROWS = 8
NA_COLS = 16
LN_EPS = 1e-6
NEG = -1e30

V7X_LANES = 128
V7X_BF16_SUBLANES = 16
V7X_VMEM_LIMIT = 56 * 1024 * 1024
COND_ROWS = 8

NA_QROWS = 4
NA_KROWS = NA_QROWS + NA_ROWS


def _pick(n, candidates):
    for t in candidates:
        if all(v % t == 0 for v in n):
            return t
    raise ValueError(f"no tile in {candidates} divides {n}")


def _params(*sem, flags=None):
    return pltpu.CompilerParams(dimension_semantics=sem, vmem_limit_bytes=V7X_VMEM_LIMIT, flags=flags)


def _layer_norm(r, g, b):
    mu = jnp.mean(r, axis=-1, keepdims=True)
    d = r - mu
    var = jnp.mean(d * d, axis=-1, keepdims=True)
    return d * lax.rsqrt(var + LN_EPS) * g + b


def _sigmoid(v):
    return 1.0 / (1.0 + jnp.exp(-v))


def _dot(a, b):
    return jnp.dot(a, b, preferred_element_type=F32)


def _dot_t(a, b):
    return lax.dot_general(a, b, (((1,), (1,)), ((), ())), preferred_element_type=F32)


def _ada_kernel(cond_ref, w_ref, b_ref, o_ref):
    cnd = cond_ref[...]
    s = (cnd * _sigmoid(cnd)).astype(BF16)
    o_ref[0] = _dot(s, w_ref[0].astype(BF16)) + b_ref[0]


def _ada(cond, ada_w, ada_b):
    depth, d, n6 = ada_w.shape
    tn = _pick((n6,), (1024, 512, 256, 128))
    return pl.pallas_call(
        _ada_kernel,
        grid=(depth, n6 // tn),
        in_specs=[
            pl.BlockSpec((COND_ROWS, d), lambda l, j: (0, 0)),
            pl.BlockSpec((1, d, tn), lambda l, j: (l, 0, j)),
            pl.BlockSpec((1, 1, tn), lambda l, j: (l, 0, j)),
        ],
        out_specs=pl.BlockSpec((1, COND_ROWS, tn), lambda l, j: (l, 0, j)),
        out_shape=jax.ShapeDtypeStruct((depth, COND_ROWS, n6), F32),
        compiler_params=_params("arbitrary", "arbitrary"),
        name="ada_mod",
    )(cond, ada_w, ada_b.reshape(depth, 1, n6))


def _inproj_ab_kernel(x_ref, sh_ref, sc_ref, w_ref, qg_ref, kg_ref, rope_ref, qkv_ref, glu_ref, *, q_scale):
    hb = (x_ref[...] * (1.0 + sc_ref[0]) + sh_ref[0]).astype(BF16)
    cos, sin_lo, sin_hi = rope_ref[0], rope_ref[1], rope_ref[2]
    a_w = A_HEADS * HEAD_DIM
    kv_w = A_KV_HEADS * HEAD_DIM
    b_w = glu_ref.shape[1]

    def norm_rope(z, gain):
        zn = z * lax.rsqrt(jnp.mean(z * z, axis=-1, keepdims=True) + LN_EPS) * gain
        quarter = HEAD_DIM // 4
        return (zn * cos + pltpu.roll(zn, HEAD_DIM - quarter, 1) * sin_lo
                + pltpu.roll(zn, quarter, 1) * sin_hi)

    q_gain = qg_ref[...] * q_scale
    heads_per_dot = 4
    for c0 in range(0, a_w, heads_per_dot * HEAD_DIM):
        z = _dot(hb, w_ref[:, c0:c0 + heads_per_dot * HEAD_DIM])
        for hh in range(heads_per_dot):
            cs = slice(c0 + hh * HEAD_DIM, c0 + (hh + 1) * HEAD_DIM)
            qkv_ref[:, cs] = norm_rope(z[:, hh * HEAD_DIM:(hh + 1) * HEAD_DIM], q_gain).astype(BF16)
    z = _dot(hb, w_ref[:, a_w:a_w + 2 * kv_w])
    for hh in range(A_KV_HEADS):
        cs = slice(a_w + hh * HEAD_DIM, a_w + (hh + 1) * HEAD_DIM)
        qkv_ref[:, cs] = norm_rope(z[:, hh * HEAD_DIM:(hh + 1) * HEAD_DIM], kg_ref[...]).astype(BF16)
    qkv_ref[:, a_w + kv_w:a_w + 2 * kv_w] = z[:, kv_w:].astype(BF16)
    u0 = a_w + 2 * kv_w
    glu_cols = 512 if b_w % 512 == 0 else b_w
    for c0 in range(0, b_w, glu_cols):
        u = _dot(hb, w_ref[:, u0 + c0:u0 + c0 + glu_cols])
        g = _dot(hb, w_ref[:, u0 + b_w + c0:u0 + b_w + c0 + glu_cols])
        glu_ref[:, c0:c0 + glu_cols] = u * _sigmoid(g)


def _inproj_ab(xc, mod3, mod_base, w_in, q_gain, k_gain, rope, *, tm, n_lat_tiles, seq, nbatch):
    n, d = xc.shape
    n_in = w_in.shape[1]
    a_w, kv_w = A_HEADS * HEAD_DIM, A_KV_HEADS * HEAD_DIM
    b_w = (n_in - a_w - 2 * kv_w) // 2
    tiles_per_seq = seq // tm

    def cond_row(i):
        return jnp.where(i < n_lat_tiles, (i * tm) // seq, nbatch)

    def rope_blk(i):
        return jnp.where(i < n_lat_tiles, i % tiles_per_seq, tiles_per_seq)

    return pl.pallas_call(
        functools.partial(_inproj_ab_kernel, q_scale=HEAD_DIM ** -0.5),
        grid=(n // tm,),
        in_specs=[
            pl.BlockSpec((tm, d), lambda i: (i, 0)),
            pl.BlockSpec((1, 1, d), lambda i: (mod_base + cond_row(i) * 6 + 0, 0, 0)),
            pl.BlockSpec((1, 1, d), lambda i: (mod_base + cond_row(i) * 6 + 1, 0, 0)),
            pl.BlockSpec((d, n_in), lambda i: (0, 0), pipeline_mode=pl.Buffered(1)),
            pl.BlockSpec((1, HEAD_DIM), lambda i: (0, 0)),
            pl.BlockSpec((1, HEAD_DIM), lambda i: (0, 0)),
            pl.BlockSpec((3, tm, HEAD_DIM), lambda i: (0, rope_blk(i), 0)),
        ],
        out_specs=[
            pl.BlockSpec((tm, a_w + 2 * kv_w), lambda i: (i, 0)),
            pl.BlockSpec((tm, b_w), lambda i: (i, 0)),
        ],
        out_shape=[
            jax.ShapeDtypeStruct((n, a_w + 2 * kv_w), BF16),
            jax.ShapeDtypeStruct((n, b_w), F32),
        ],
        compiler_params=_params("arbitrary"),
        name="inproj_ab",
    )(xc, mod3, mod3, w_in, q_gain, k_gain, rope)


def _rope_tables(seq, tm):
    t = jnp.arange(seq)
    nfreq = HEAD_DIM // 4
    inv = ROPE_THETA ** (-jnp.arange(nfreq, dtype=F32) / nfreq)
    ang_r = (t // GRID_W).astype(F32)[:, None] * inv
    ang_c = (t % GRID_W).astype(F32)[:, None] * inv
    ang = jnp.concatenate([ang_r, ang_r, ang_c, ang_c], axis=-1)
    cos, sin = jnp.cos(ang), jnp.sin(ang)
    low = (np.arange(HEAD_DIM) // nfreq) % 2 == 0
    sin_lo = jnp.where(low, -sin, 0.0)
    sin_hi = jnp.where(low, 0.0, sin)
    tab = jnp.stack([cos, sin_lo, sin_hi])
    ident = jnp.stack([jnp.ones((tm, HEAD_DIM), F32), jnp.zeros((tm, HEAD_DIM), F32), jnp.zeros((tm, HEAD_DIM), F32)])
    return jnp.concatenate([tab, ident], axis=1)


def _softmax_pv(q, parts):
    ss = [_dot_t(q, k_ref[...]) for k_ref, _ in parts]
    m = functools.reduce(jnp.maximum, [jnp.max(s, axis=-1, keepdims=True) for s in ss])
    ps = [jnp.exp(s - m) for s in ss]
    l = functools.reduce(jnp.add, [jnp.sum(p, axis=-1, keepdims=True) for p in ps])
    o = functools.reduce(jnp.add, [_dot(p.astype(BF16), v_ref[...]) for p, (_, v_ref) in zip(ps, parts)])
    return o / l


def _gqa_kernel(q_ref, kx_ref, vx_ref, kc_ref, vc_ref, o_ref, *, n_lat_tiles):
    qi = pl.program_id(1)

    def run(parts):
        for g in range(A_GROUP):
            cs = slice(g * HEAD_DIM, (g + 1) * HEAD_DIM)
            o_ref[:, cs] = _softmax_pv(q_ref[:, cs], parts).astype(BF16)

    @pl.when(qi < n_lat_tiles)
    def _():
        run([(kx_ref, vx_ref), (kc_ref, vc_ref)])

    @pl.when(qi >= n_lat_tiles)
    def _():
        run([(kc_ref, vc_ref)])


def _gqa(qkv, *, nbatch, seq, ctx_len):
    n = qkv.shape[0]
    tq = _pick((seq, ctx_len), (256, 128))
    n_lat_tiles = nbatch * seq // tq
    lat_per_seq, ctx_per_seq = seq // tq, ctx_len // tq
    a_w = A_HEADS * HEAD_DIM
    gw = A_GROUP * HEAD_DIM
    k_col0 = a_w // HEAD_DIM
    v_col0 = k_col0 + A_KV_HEADS

    def batch(qi):
        return jnp.where(qi < n_lat_tiles, qi // lat_per_seq, (qi - n_lat_tiles) // ctx_per_seq)

    def lat_batch(qi):
        return jnp.minimum(qi // lat_per_seq, nbatch - 1)

    ctx_blk0 = nbatch * seq // ctx_len
    return pl.pallas_call(
        functools.partial(_gqa_kernel, n_lat_tiles=n_lat_tiles),
        grid=(A_KV_HEADS, n // tq),
        in_specs=[
            pl.BlockSpec((tq, gw), lambda h, qi: (qi, h)),
            pl.BlockSpec((seq, HEAD_DIM), lambda h, qi: (lat_batch(qi), k_col0 + h)),
            pl.BlockSpec((seq, HEAD_DIM), lambda h, qi: (lat_batch(qi), v_col0 + h)),
            pl.BlockSpec((ctx_len, HEAD_DIM), lambda h, qi: (ctx_blk0 + batch(qi), k_col0 + h)),
            pl.BlockSpec((ctx_len, HEAD_DIM), lambda h, qi: (ctx_blk0 + batch(qi), v_col0 + h)),
        ],
        out_specs=pl.BlockSpec((tq, gw), lambda h, qi: (qi, h)),
        out_shape=jax.ShapeDtypeStruct((n, a_w), BF16),
        compiler_params=_params("arbitrary", "arbitrary"),
        name="gqa_attn",
    )(qkv, qkv, qkv, qkv, qkv)


def _conv_kernel(x_ref, xp_ref, xn_ref, dw_ref, db_ref, g_ref, b_ref, o_ref, buf_ref, y_ref, *,
                 tt, halo, n_lat_tiles, lat_per_seq, ctx_per_seq):
    i = pl.program_id(0)
    taps = dw_ref.shape[0]
    pad = taps // 2
    idx = jnp.where(i < n_lat_tiles, i % lat_per_seq, (i - n_lat_tiles) % ctx_per_seq)
    per_seq = jnp.where(i < n_lat_tiles, lat_per_seq, ctx_per_seq)
    buf_ref[0:halo, :] = jnp.where(idx > 0, xp_ref[...], 0.0)
    buf_ref[halo:halo + tt, :] = x_ref[...]
    buf_ref[halo + tt:, :] = jnp.where(idx < per_seq - 1, xn_ref[...], 0.0)
    nchunk = x_ref.shape[1] // V7X_LANES

    def chunk(c, carry):
        cs = pl.ds(pl.multiple_of(c * V7X_LANES, V7X_LANES), V7X_LANES)
        acc = jnp.broadcast_to(db_ref[:, cs], (tt, V7X_LANES))
        for k in range(taps):
            acc = acc + buf_ref[pl.ds(halo - pad + k, tt), cs] * dw_ref[pl.ds(k, 1), cs]
        y_ref[:, cs] = acc
        return carry

    lax.fori_loop(0, nchunk, chunk, 0)
    yn = _layer_norm(y_ref[...], g_ref[...], b_ref[...])
    o_ref[...] = (yn * _sigmoid(yn)).astype(BF16)


def _conformer_conv(glu, dw_w, dw_b, n_g, n_b, *, nbatch, seq, ctx_len):
    n, cw = glu.shape
    tt = _pick((seq, ctx_len), (256, 128))
    halo = 16
    assert dw_w.shape[0] // 2 <= halo
    n_lat_tiles = nbatch * seq // tt
    hb = tt // halo
    nhalo = n // halo
    return pl.pallas_call(
        functools.partial(_conv_kernel, tt=tt, halo=halo, n_lat_tiles=n_lat_tiles,
                          lat_per_seq=seq // tt, ctx_per_seq=ctx_len // tt),
        grid=(n // tt,),
        in_specs=[
            pl.BlockSpec((tt, cw), lambda i: (i, 0)),
            pl.BlockSpec((halo, cw), lambda i: (jnp.maximum(i * hb - 1, 0), 0)),
            pl.BlockSpec((halo, cw), lambda i: (jnp.minimum((i + 1) * hb, nhalo - 1), 0)),
            pl.BlockSpec(dw_w.shape, lambda i: (0, 0)),
            pl.BlockSpec((1, cw), lambda i: (0, 0)),
            pl.BlockSpec((1, cw), lambda i: (0, 0)),
            pl.BlockSpec((1, cw), lambda i: (0, 0)),
        ],
        out_specs=pl.BlockSpec((tt, cw), lambda i: (i, 0)),
        out_shape=jax.ShapeDtypeStruct((n, cw), BF16),
        scratch_shapes=[pltpu.VMEM((tt + 2 * halo, cw), F32), pltpu.VMEM((tt, cw), F32)],
        compiler_params=_params("arbitrary"),
        name="conformer_conv",
    )(glu, glu, glu, dw_w, dw_b.reshape(1, cw), n_g.reshape(1, cw), n_b.reshape(1, cw))


def _outproj_kernel(*refs, n_parts, alpha):
    part_refs = refs[:n_parts]
    w_ref, x_ref, gate_ref, sh_ref, sc_ref, lg_ref, lb_ref, x1_ref, h2_ref = refs[n_parts:]
    y = None
    k0 = 0
    for p_ref in part_refs:
        kw = p_ref.shape[1]
        t = _dot(p_ref[...], w_ref[k0:k0 + kw, :])
        y = t if y is None else y + t
        k0 += kw
    x1 = _layer_norm(alpha * x_ref[...] + gate_ref[0] * y, lg_ref[...], lb_ref[...])
    x1_ref[...] = x1
    h2_ref[...] = (x1 * (1.0 + sc_ref[0]) + sh_ref[0]).astype(BF16)


def _outproj(parts, w_out, xc, mod3, mod_base, ln_g, ln_b, *, n_out, tm, n_lat_tiles, seq, nbatch, alpha):
    d = xc.shape[1]

    def cond_row(i):
        return jnp.where(i < n_lat_tiles, (i * tm) // seq, nbatch)

    def mod_spec(k):
        return pl.BlockSpec((1, 1, d), lambda i: (mod_base + cond_row(i) * 6 + k, 0, 0))

    return pl.pallas_call(
        functools.partial(_outproj_kernel, n_parts=len(parts), alpha=alpha),
        grid=(n_out // tm,),
        in_specs=[pl.BlockSpec((tm, p.shape[1]), lambda i: (i, 0)) for p in parts] + [
            pl.BlockSpec(w_out.shape, lambda i: (0, 0), pipeline_mode=pl.Buffered(1)),
            pl.BlockSpec((tm, d), lambda i: (i, 0)),
            mod_spec(2), mod_spec(3), mod_spec(4),
            pl.BlockSpec((1, d), lambda i: (0, 0)),
            pl.BlockSpec((1, d), lambda i: (0, 0)),
        ],
        out_specs=[pl.BlockSpec((tm, d), lambda i: (i, 0)), pl.BlockSpec((tm, d), lambda i: (i, 0))],
        out_shape=[jax.ShapeDtypeStruct((n_out, d), F32), jax.ShapeDtypeStruct((n_out, d), BF16)],
        compiler_params=_params("arbitrary"),
        name="outproj_ln",
    )(*parts, w_out, xc, mod3, mod3, mod3, ln_g.reshape(1, d), ln_b.reshape(1, d))


def _ffn_kernel(h_ref, hp_ref, hn_ref, wa_ref, wg_ref, dwa_ref, dwg_ref, dba_ref, dbg_ref, wd_ref,
                x_ref, gate_ref, lg_ref, lb_ref, o_ref, hext_ref, za0_ref, za1_ref, zg0_ref, zg1_ref, *,
                tm, halo, sub, nf, seq, ctx_len, n_lat_tiles, alpha):
    za_refs = (za0_ref, za1_ref)
    zg_refs = (zg0_ref, zg1_ref)
    i = pl.program_id(0)
    c = pl.program_id(1)
    is_lat = i < n_lat_tiles
    row = lax.broadcasted_iota(jnp.int32, (tm, 1), 0) + i * tm
    pos = row & (ctx_len - 1)
    has_prev = pos != 0
    has_next = pos != ctx_len - 1

    chunks = [slice(s0, s0 + sub) for s0 in range(0, wa_ref.shape[1], sub)]

    def up_a(slot, cs):
        za_refs[slot][:, cs] = _dot(hext_ref[...], wa_ref[:, cs])

    def up_g(slot, cs):
        zg_refs[slot][:, cs] = _dot(hext_ref[...], wg_ref[:, cs])

    def conv(z_ref, dw_ref, db_ref, cs, masked):
        z_prev = z_ref[halo - 1:halo - 1 + tm, cs]
        z_next = z_ref[halo + 1:halo + 1 + tm, cs]
        if masked:
            z_prev = jnp.where(has_prev, z_prev, 0.0)
            z_next = jnp.where(has_next, z_next, 0.0)
        return (z_prev * dw_ref[0:1, cs] + z_ref[halo:halo + tm, cs] * dw_ref[1:2, cs] + z_next * dw_ref[2:3, cs]
                + db_ref[:, cs])

    def step(up_slot, down_slot, masked=False):
        if up_slot is not None:
            for cs in chunks:
                up_a(up_slot, cs)
                up_g(up_slot, cs)
        if down_slot is not None:
            contrib = None
            for cs in chunks:
                a = conv(za_refs[down_slot], dwa_ref, dba_ref, cs, masked)
                g = conv(zg_refs[down_slot], dwg_ref, dbg_ref, cs, masked)
                t = _dot((g * _sigmoid(g) * a).astype(BF16), wd_ref[cs, :])
                contrib = t if contrib is None else contrib + t
            o_ref[...] += contrib

    @pl.when(c == 0)
    def _():
        starts_seq = is_lat & ((i * tm) % seq == 0)
        ends_seq = is_lat & (((i + 1) * tm) % seq == 0)
        zero_halo = jnp.zeros(hp_ref.shape, hp_ref.dtype)
        hext_ref[0:halo, :] = jnp.where(starts_seq, zero_halo, hp_ref[...])
        hext_ref[halo:halo + tm, :] = h_ref[...]
        hext_ref[halo + tm:, :] = jnp.where(ends_seq, zero_halo, hn_ref[...])
        o_ref[...] = jnp.zeros_like(o_ref)
        step(0, None)

    for masked in (False, True):
        for slot in (0, 1):
            @pl.when((c >= 1) & (c < nf) & (c % 2 == slot) & (is_lat != masked))
            def _():
                step(slot, 1 - slot, masked)

        @pl.when((c == nf) & (is_lat != masked))
        def _():
            step(None, (nf - 1) % 2, masked)

    @pl.when(c == nf)
    def _():
        o_ref[...] = _layer_norm(alpha * x_ref[...] + gate_ref[0] * o_ref[...], lg_ref[...], lb_ref[...])


def _ffn(h2, x1, w_up, dw_w, dw_b, w_down, mod3, mod_base, ln_g, ln_b, *, tm, n_lat_tiles, seq, ctx_len,
         nbatch, alpha):
    n_out, d = x1.shape
    d_ff = w_down.shape[0]
    tf = _pick((d_ff,), (512, 256, 128))
    nf = d_ff // tf
    halo = V7X_BF16_SUBLANES
    hb = tm // halo
    nhalo = h2.shape[0] // halo
    assert dw_w.shape[0] == 3

    def cond_row(i):
        return jnp.where(i < n_lat_tiles, (i * tm) // seq, nbatch)

    dw_b2 = dw_b.reshape(1, 2 * d_ff)

    def up(c):
        return jnp.minimum(c, nf - 1)

    def down(c):
        return jnp.maximum(c - 1, 0)

    return pl.pallas_call(
        functools.partial(_ffn_kernel, tm=tm, halo=halo, sub=min(tf, 256), nf=nf, seq=seq, ctx_len=ctx_len,
                          n_lat_tiles=n_lat_tiles, alpha=alpha),
        grid=(n_out // tm, nf + 1),
        in_specs=[
            pl.BlockSpec((tm, d), lambda i, c: (i, 0)),
            pl.BlockSpec((halo, d), lambda i, c: (jnp.maximum(i * hb - 1, 0), 0)),
            pl.BlockSpec((halo, d), lambda i, c: (jnp.minimum((i + 1) * hb, nhalo - 1), 0)),
            pl.BlockSpec((d, tf), lambda i, c: (0, up(c))),
            pl.BlockSpec((d, tf), lambda i, c: (0, nf + up(c))),
            pl.BlockSpec((3, tf), lambda i, c: (0, down(c))),
            pl.BlockSpec((3, tf), lambda i, c: (0, nf + down(c))),
            pl.BlockSpec((1, tf), lambda i, c: (0, down(c))),
            pl.BlockSpec((1, tf), lambda i, c: (0, nf + down(c))),
            pl.BlockSpec((tf, d), lambda i, c: (down(c), 0)),
            pl.BlockSpec((tm, d), lambda i, c: (i, 0)),
            pl.BlockSpec((1, 1, d), lambda i, c: (mod_base + cond_row(i) * 6 + 5, 0, 0)),
            pl.BlockSpec((1, d), lambda i, c: (0, 0)),
            pl.BlockSpec((1, d), lambda i, c: (0, 0)),
        ],
        out_specs=pl.BlockSpec((tm, d), lambda i, c: (i, 0)),
        out_shape=jax.ShapeDtypeStruct((n_out, d), F32),
        scratch_shapes=[pltpu.VMEM((tm + 2 * halo, d), BF16)] + [pltpu.VMEM((tm + 2 * halo, tf), F32)] * 4,
        compiler_params=_params("arbitrary", "arbitrary"),
        name="conv_ffn",
    )(h2, h2, h2, w_up, w_up, dw_w, dw_w, dw_b2, dw_b2, w_down, x1, mod3, ln_g.reshape(1, d), ln_b.reshape(1, d))


def _inproj_c_kernel(x_ref, sh_ref, sc_ref, w_ref, o_ref, *, q_scale):
    hb = (x_ref[...] * (1.0 + sc_ref[0]) + sh_ref[0]).astype(BF16)
    d = x_ref.shape[1]
    cols = 512
    for c0 in range(0, o_ref.shape[1], cols):
        z = _dot(hb, w_ref[:, c0:c0 + cols])
        if c0 < d:
            z = z * q_scale
        o_ref[:, c0:c0 + cols] = z.astype(BF16)


def _inproj_c(xc, mod3, mod_base, w_in, *, tm, n_lat_tiles, seq, nbatch):
    n, d = xc.shape
    n_in = w_in.shape[1]
    assert d % 512 == 0 and n_in == 3 * d

    def cond_row(i):
        return jnp.where(i < n_lat_tiles, (i * tm) // seq, nbatch)

    return pl.pallas_call(
        functools.partial(_inproj_c_kernel, q_scale=HEAD_DIM ** -0.5),
        grid=(n // tm,),
        in_specs=[
            pl.BlockSpec((tm, d), lambda i: (i, 0)),
            pl.BlockSpec((1, 1, d), lambda i: (mod_base + cond_row(i) * 6 + 0, 0, 0)),
            pl.BlockSpec((1, 1, d), lambda i: (mod_base + cond_row(i) * 6 + 1, 0, 0)),
            pl.BlockSpec((d, n_in), lambda i: (0, 0), pipeline_mode=pl.Buffered(1)),
        ],
        out_specs=pl.BlockSpec((tm, n_in), lambda i: (i, 0)),
        out_shape=jax.ShapeDtypeStruct((n, n_in), BF16),
        compiler_params=_params("arbitrary"),
        name="inproj_c",
    )(xc, mod3, mod3, w_in)


def _na_bias_table(rel_bias):
    nheads = rel_bias.shape[0]
    qc = np.arange(GRID_W)[:, None]
    kc = np.arange(GRID_W)[None, :]
    cstart = np.clip(qc - NA_COLS // 2, 0, GRID_W - NA_COLS)
    valid = (kc >= cstart) & (kc < cstart + NA_COLS)
    dc = np.clip(kc - qc + NA_COLS - 1, 0, 2 * NA_COLS - 2)
    toep = jnp.where(valid[None, None], rel_bias[:, :, dc], NEG)
    neg = jnp.full((nheads, 1, GRID_W, GRID_W), NEG, F32)
    ext = jnp.concatenate([neg, toep, neg], axis=1)
    return jnp.concatenate([ext[:, :2 * NA_ROWS], ext[:, 1:]], axis=-1)


def _natten_kernel(q_ref, k_ref, v_ref, kc_ref, vc_ref, bias_ref, o_ref, *, rows):
    nq = NA_QROWS * GRID_W
    nk = NA_KROWS * GRID_W
    left = lax.broadcasted_iota(jnp.int32, (GRID_W, 2 * GRID_W), 1) < GRID_W
    for blk in range(rows // NA_QROWS):
        ks = min(max(NA_QROWS * blk - NA_ROWS // 2, 0), rows - NA_KROWS)
        q = q_ref[blk * nq:(blk + 1) * nq, :]
        s_loc = _dot_t(q, k_ref[ks * GRID_W:ks * GRID_W + nk, :])
        bias_rows = []
        for qi in range(NA_QROWS):
            qr = NA_QROWS * blk + qi
            r0 = min(max(qr - NA_ROWS // 2, 0), rows - NA_ROWS)
            slabs = []
            for j in range(NA_KROWS // 2):
                kr = ks + 2 * j
                ok0 = r0 <= kr < r0 + NA_ROWS
                ok1 = r0 <= kr + 1 < r0 + NA_ROWS
                if not (ok0 or ok1):
                    slabs.append(jnp.full((GRID_W, 2 * GRID_W), NEG, F32))
                    continue
                slab = bias_ref[0, kr - qr + NA_ROWS]
                if not ok1:
                    slab = jnp.where(left, slab, NEG)
                elif not ok0:
                    slab = jnp.where(left, NEG, slab)
                slabs.append(slab)
            bias_rows.append(jnp.concatenate(slabs, axis=1))
        s_loc = s_loc + jnp.concatenate(bias_rows, axis=0)
        s_ctx = _dot_t(q, kc_ref[...])
        m = jnp.maximum(jnp.max(s_loc, axis=-1, keepdims=True), jnp.max(s_ctx, axis=-1, keepdims=True))
        p_loc = jnp.exp(s_loc - m)
        p_ctx = jnp.exp(s_ctx - m)
        l = jnp.sum(p_loc, axis=-1, keepdims=True) + jnp.sum(p_ctx, axis=-1, keepdims=True)
        o = (_dot(p_loc.astype(BF16), v_ref[ks * GRID_W:ks * GRID_W + nk, :])
             + _dot(p_ctx.astype(BF16), vc_ref[...]))
        o_ref[blk * nq:(blk + 1) * nq, :] = (o / l).astype(BF16)


def _natten(qkv, bias_tab, *, nbatch, seq, ctx_len, d):
    rows = seq // GRID_W
    nheads = d // HEAD_DIM
    assert rows % NA_QROWS == 0 and rows >= NA_KROWS
    ctx_blk0 = nbatch * seq // ctx_len
    return pl.pallas_call(
        functools.partial(_natten_kernel, rows=rows),
        grid=(nheads, nbatch),
        in_specs=[
            pl.BlockSpec((seq, HEAD_DIM), lambda h, b: (b, h)),
            pl.BlockSpec((seq, HEAD_DIM), lambda h, b: (b, nheads + h)),
            pl.BlockSpec((seq, HEAD_DIM), lambda h, b: (b, 2 * nheads + h)),
            pl.BlockSpec((ctx_len, HEAD_DIM), lambda h, b: (ctx_blk0 + b, nheads + h)),
            pl.BlockSpec((ctx_len, HEAD_DIM), lambda h, b: (ctx_blk0 + b, 2 * nheads + h)),
            pl.BlockSpec((1,) + bias_tab.shape[1:], lambda h, b: (h, 0, 0, 0)),
        ],
        out_specs=pl.BlockSpec((seq, HEAD_DIM), lambda h, b: (b, h)),
        out_shape=jax.ShapeDtypeStruct((nbatch * seq, d), BF16),
        compiler_params=_params("arbitrary", "arbitrary"),
        name="natten",
    )(qkv, qkv, qkv, qkv, qkv, bias_tab)


def kernel(x, c, ctx, c_ctx, ada_w, ada_b, post_ln_g, post_ln_b, ab_w_in, ab_w_out, ab_q_gain, ab_k_gain,
           ab_dw_w, ab_dw_b, ab_norm_g, ab_norm_b, c_w_in, c_w_out, c_rel_bias, ffn_w_up, ffn_dw_w, ffn_dw_b,
           ffn_w_down):
    nbatch, seq, d = x.shape
    ctx_len = ctx.shape[1]
    depth = ada_w.shape[0]
    assert depth == 2 and nbatch + 1 <= COND_ROWS and nbatch * ctx_len <= seq
    assert seq % GRID_W == 0 and seq & (seq - 1) == 0 and ctx_len & (ctx_len - 1) == 0 and ctx_len <= seq
    nx, nc = nbatch * seq, nbatch * ctx_len
    alpha = (2 * depth) ** 0.25
    tm = _pick((seq, nc), (512, 256))
    n_lat_tiles = nx // tm
    geo = dict(tm=tm, n_lat_tiles=n_lat_tiles, seq=seq, nbatch=nbatch)

    cond = jnp.concatenate([c, c_ctx[None], jnp.zeros((COND_ROWS - nbatch - 1, d), F32)], axis=0)
    mod3 = _ada(cond, ada_w, ada_b).reshape(depth * COND_ROWS * 6, 1, d)
    xc = jnp.concatenate([x.reshape(nx, d), ctx.reshape(nc, d)], axis=0)

    mod_base = 0
    qkv, glu = _inproj_ab(xc, mod3, mod_base, ab_w_in[0].astype(BF16), ab_q_gain[0][None], ab_k_gain[0][None],
                          _rope_tables(seq, tm), **geo)
    attn = _gqa(qkv, nbatch=nbatch, seq=seq, ctx_len=ctx_len)
    conv = _conformer_conv(glu, ab_dw_w[0], ab_dw_b[0], ab_norm_g[0], ab_norm_b[0],
                           nbatch=nbatch, seq=seq, ctx_len=ctx_len)
    x1, h2 = _outproj([attn, conv], ab_w_out[0].astype(BF16), xc, mod3, mod_base, post_ln_g[0, 0],
                      post_ln_b[0, 0], n_out=nx + nc, alpha=alpha, **geo)
    xc = _ffn(h2, x1, ffn_w_up[0].astype(BF16), ffn_dw_w[0], ffn_dw_b[0], ffn_w_down[0].astype(BF16), mod3,
              mod_base, post_ln_g[0, 1], post_ln_b[0, 1], ctx_len=ctx_len, alpha=alpha, **geo)

    mod_base = COND_ROWS * 6
    qkv = _inproj_c(xc, mod3, mod_base, c_w_in[0].astype(BF16), **geo)
    attn = _natten(qkv, _na_bias_table(c_rel_bias[0]), nbatch=nbatch, seq=seq, ctx_len=ctx_len, d=d)
    x1, h2 = _outproj([attn], c_w_out[0].astype(BF16), xc, mod3, mod_base, post_ln_g[1, 0], post_ln_b[1, 0],
                      n_out=nx, alpha=alpha, **geo)
    out = _ffn(h2, x1, ffn_w_up[1].astype(BF16), ffn_dw_w[1], ffn_dw_b[1], ffn_w_down[1].astype(BF16), mod3,
               mod_base, post_ln_g[1, 1], post_ln_b[1, 1], ctx_len=ctx_len, alpha=alpha, **geo)
    return out.reshape(nbatch, seq, d)
```

```python
import functools

import jax
import jax.numpy as jnp
import numpy as np
from jax import lax
from jax.experimental import pallas as pl
from jax.experimental.pallas import tpu as pltpu

F32 = jnp.float32
BF16 = jnp.bfloat16

HEAD_DIM = 128
GRID_W = 64
A_HEADS = 8
A_KV_HEADS = 2
A_GROUP = A_HEADS // A_KV_HEADS
ROPE_THETA = 10000.0
NA_ROWS = 8
NA_COLS = 16
LN_EPS = 1e-6
NEG = -1e30

V7X_LANES = 128
V7X_F32_SUBLANES = 8
V7X_BF16_SUBLANES = 16
V7X_VMEM_LIMIT = 56 * 1024 * 1024
COND_ROWS = 8

NA_QROWS = 4
NA_KROWS = NA_QROWS + NA_ROWS


def _pick(n, candidates):
    for t in candidates:
        if all(v % t == 0 for v in n):
            return t
    raise ValueError(f"no tile in {candidates} divides {n}")


def _params(*sem, flags=None):
    return pltpu.CompilerParams(dimension_semantics=sem, vmem_limit_bytes=V7X_VMEM_LIMIT, flags=flags)


def _layer_norm(r, g, b):
    mu = jnp.mean(r, axis=-1, keepdims=True)
    d = r - mu
    var = jnp.mean(d * d, axis=-1, keepdims=True)
    return d * lax.rsqrt(var + LN_EPS) * g + b


def _sigmoid(v):
    return 1.0 / (1.0 + jnp.exp(-v))


def _dot(a, b):
    return jnp.dot(a, b, preferred_element_type=F32)


def _dot_t(a, b):
    return lax.dot_general(a, b, (((1,), (1,)), ((), ())), preferred_element_type=F32)


def _ada_kernel(cond_ref, w_ref, b_ref, o_ref):
    cnd = cond_ref[...]
    s = (cnd * _sigmoid(cnd)).astype(BF16)
    o_ref[0] = _dot(s, w_ref[0].astype(BF16)) + b_ref[0]


def _ada(cond, ada_w, ada_b):
    depth, d, n6 = ada_w.shape
    tn = _pick((n6,), (1024, 512, 256, 128))
    return pl.pallas_call(
        _ada_kernel,
        grid=(depth, n6 // tn),
        in_specs=[
            pl.BlockSpec((COND_ROWS, d), lambda l, j: (0, 0)),
            pl.BlockSpec((1, d, tn), lambda l, j: (l, 0, j)),
            pl.BlockSpec((1, 1, tn), lambda l, j: (l, 0, j)),
        ],
        out_specs=pl.BlockSpec((1, COND_ROWS, tn), lambda l, j: (l, 0, j)),
        out_shape=jax.ShapeDtypeStruct((depth, COND_ROWS, n6), F32),
        compiler_params=_params("arbitrary", "arbitrary"),
        name="ada_mod",
    )(cond, ada_w, ada_b.reshape(depth, 1, n6))


def _inproj_ab_kernel(x_ref, c_ref, sh_ref, sc_ref, w_ref, qg_ref, kg_ref, rope_ref, qkv_ref, glu_ref, hb_ref, *,
                      q_scale, n_lat_tiles):
    is_lat = pl.program_id(0) < n_lat_tiles
    for src_ref, cond in ((x_ref, is_lat), (c_ref, jnp.logical_not(is_lat))):
        @pl.when(cond)
        def _():
            hb_ref[...] = (src_ref[...] * (1.0 + sc_ref[0]) + sh_ref[0]).astype(BF16)
    hb = hb_ref[...]
    cos, sin_lo, sin_hi = rope_ref[0], rope_ref[1], rope_ref[2]
    a_w = A_HEADS * HEAD_DIM
    kv_w = A_KV_HEADS * HEAD_DIM
    b_w = glu_ref.shape[1]

    def norm_rope(z, gain):
        zn = z * lax.rsqrt(jnp.mean(z * z, axis=-1, keepdims=True) + LN_EPS) * gain
        quarter = HEAD_DIM // 4
        return (zn * cos + pltpu.roll(zn, HEAD_DIM - quarter, 1) * sin_lo
                + pltpu.roll(zn, quarter, 1) * sin_hi)

    q_gain = qg_ref[...] * q_scale
    heads_per_dot = 4
    for c0 in range(0, a_w, heads_per_dot * HEAD_DIM):
        z = _dot(hb, w_ref[:, c0:c0 + heads_per_dot * HEAD_DIM])
        for hh in range(heads_per_dot):
            cs = slice(c0 + hh * HEAD_DIM, c0 + (hh + 1) * HEAD_DIM)
            qkv_ref[:, cs] = norm_rope(z[:, hh * HEAD_DIM:(hh + 1) * HEAD_DIM], q_gain).astype(BF16)
    z = _dot(hb, w_ref[:, a_w:a_w + 2 * kv_w])
    for hh in range(A_KV_HEADS):
        cs = slice(a_w + hh * HEAD_DIM, a_w + (hh + 1) * HEAD_DIM)
        qkv_ref[:, cs] = norm_rope(z[:, hh * HEAD_DIM:(hh + 1) * HEAD_DIM], kg_ref[...]).astype(BF16)
    qkv_ref[:, a_w + kv_w:a_w + 2 * kv_w] = z[:, kv_w:].astype(BF16)
    u0 = a_w + 2 * kv_w
    glu_cols = 512 if b_w % 512 == 0 else b_w
    for c0 in range(0, b_w, glu_cols):
        u = _dot(hb, w_ref[:, u0 + c0:u0 + c0 + glu_cols])
        g = _dot(hb, w_ref[:, u0 + b_w + c0:u0 + b_w + c0 + glu_cols])
        glu_ref[:, c0:c0 + glu_cols] = u * _sigmoid(g)


def _inproj_ab(x2d, c2d, mod3, mod_base, w_in, q_gain, k_gain, rope, *, tm, n_lat_tiles, seq, nbatch):
    d = x2d.shape[1]
    n = x2d.shape[0] + c2d.shape[0]
    n_in = w_in.shape[1]
    a_w, kv_w = A_HEADS * HEAD_DIM, A_KV_HEADS * HEAD_DIM
    b_w = (n_in - a_w - 2 * kv_w) // 2
    tiles_per_seq = seq // tm

    def cond_row(i):
        return jnp.where(i < n_lat_tiles, (i * tm) // seq, nbatch)

    def rope_blk(i):
        return jnp.where(i < n_lat_tiles, i % tiles_per_seq, tiles_per_seq)

    return pl.pallas_call(
        functools.partial(_inproj_ab_kernel, q_scale=HEAD_DIM ** -0.5, n_lat_tiles=n_lat_tiles),
        grid=(n // tm,),
        in_specs=[
            pl.BlockSpec((tm, d), lambda i: (jnp.minimum(i, n_lat_tiles - 1), 0)),
            pl.BlockSpec((tm, d), lambda i: (jnp.maximum(i - n_lat_tiles, 0), 0)),
            pl.BlockSpec((1, 1, d), lambda i: (mod_base + cond_row(i) * 6 + 0, 0, 0)),
            pl.BlockSpec((1, 1, d), lambda i: (mod_base + cond_row(i) * 6 + 1, 0, 0)),
            pl.BlockSpec((d, n_in), lambda i: (0, 0), pipeline_mode=pl.Buffered(1)),
            pl.BlockSpec((1, HEAD_DIM), lambda i: (0, 0)),
            pl.BlockSpec((1, HEAD_DIM), lambda i: (0, 0)),
            pl.BlockSpec((3, tm, HEAD_DIM), lambda i: (0, rope_blk(i), 0)),
        ],
        out_specs=[
            pl.BlockSpec((tm, a_w + 2 * kv_w), lambda i: (i, 0)),
            pl.BlockSpec((tm, b_w), lambda i: (i, 0)),
        ],
        out_shape=[
            jax.ShapeDtypeStruct((n, a_w + 2 * kv_w), BF16),
            jax.ShapeDtypeStruct((n, b_w), F32),
        ],
        scratch_shapes=[pltpu.VMEM((tm, d), BF16)],
        compiler_params=_params("arbitrary"),
        name="inproj_ab",
    )(x2d, c2d, mod3, mod3, w_in, q_gain, k_gain, rope)


def _rope_tables(seq, tm):
    t = jnp.arange(seq)
    nfreq = HEAD_DIM // 4
    inv = ROPE_THETA ** (-jnp.arange(nfreq, dtype=F32) / nfreq)
    ang_r = (t // GRID_W).astype(F32)[:, None] * inv
    ang_c = (t % GRID_W).astype(F32)[:, None] * inv
    ang = jnp.concatenate([ang_r, ang_r, ang_c, ang_c], axis=-1)
    cos, sin = jnp.cos(ang), jnp.sin(ang)
    low = (np.arange(HEAD_DIM) // nfreq) % 2 == 0
    sin_lo = jnp.where(low, -sin, 0.0)
    sin_hi = jnp.where(low, 0.0, sin)
    tab = jnp.stack([cos, sin_lo, sin_hi])
    ident = jnp.stack([jnp.ones((tm, HEAD_DIM), F32), jnp.zeros((tm, HEAD_DIM), F32), jnp.zeros((tm, HEAD_DIM), F32)])
    return jnp.concatenate([tab, ident], axis=1)


def _softmax_pv(q, parts):
    ss = [_dot_t(q, k_ref[...]) for k_ref, _ in parts]
    m = functools.reduce(jnp.maximum, [jnp.max(s, axis=-1, keepdims=True) for s in ss])
    ps = [jnp.exp(s - m) for s in ss]
    l = functools.reduce(jnp.add, [jnp.sum(p, axis=-1, keepdims=True) for p in ps])
    o = functools.reduce(jnp.add, [_dot(p.astype(BF16), v_ref[...]) for p, (_, v_ref) in zip(ps, parts)])
    return o / l


def _gqa_kernel(q_ref, kx_ref, vx_ref, kc_ref, vc_ref, o_ref, *, n_lat_tiles):
    qi = pl.program_id(1)

    def run(parts):
        for g in range(A_GROUP):
            cs = slice(g * HEAD_DIM, (g + 1) * HEAD_DIM)
            o_ref[:, cs] = _softmax_pv(q_ref[:, cs], parts).astype(BF16)

    @pl.when(qi < n_lat_tiles)
    def _():
        run([(kx_ref, vx_ref), (kc_ref, vc_ref)])

    @pl.when(qi >= n_lat_tiles)
    def _():
        run([(kc_ref, vc_ref)])


def _gqa(qkv, *, nbatch, seq, ctx_len):
    n = qkv.shape[0]
    tq = _pick((seq, ctx_len), (256, 128))
    n_lat_tiles = nbatch * seq // tq
    lat_per_seq, ctx_per_seq = seq // tq, ctx_len // tq
    a_w = A_HEADS * HEAD_DIM
    gw = A_GROUP * HEAD_DIM
    k_col0 = a_w // HEAD_DIM
    v_col0 = k_col0 + A_KV_HEADS

    def batch(qi):
        return jnp.where(qi < n_lat_tiles, qi // lat_per_seq, (qi - n_lat_tiles) // ctx_per_seq)

    def lat_batch(qi):
        return jnp.minimum(qi // lat_per_seq, nbatch - 1)

    ctx_blk0 = nbatch * seq // ctx_len
    return pl.pallas_call(
        functools.partial(_gqa_kernel, n_lat_tiles=n_lat_tiles),
        grid=(A_KV_HEADS, n // tq),
        in_specs=[
            pl.BlockSpec((tq, gw), lambda h, qi: (qi, h)),
            pl.BlockSpec((seq, HEAD_DIM), lambda h, qi: (lat_batch(qi), k_col0 + h)),
            pl.BlockSpec((seq, HEAD_DIM), lambda h, qi: (lat_batch(qi), v_col0 + h)),
            pl.BlockSpec((ctx_len, HEAD_DIM), lambda h, qi: (ctx_blk0 + batch(qi), k_col0 + h)),
            pl.BlockSpec((ctx_len, HEAD_DIM), lambda h, qi: (ctx_blk0 + batch(qi), v_col0 + h)),
        ],
        out_specs=pl.BlockSpec((tq, gw), lambda h, qi: (qi, h)),
        out_shape=jax.ShapeDtypeStruct((n, a_w), BF16),
        compiler_params=_params("arbitrary", "arbitrary"),
        name="gqa_attn",
    )(qkv, qkv, qkv, qkv, qkv)


def _conv_kernel(x_ref, xp_ref, xn_ref, dw_ref, db_ref, g_ref, b_ref, o_ref, buf_ref, y_ref, shift_ref, *,
                 tt, halo, n_lat_tiles, lat_per_seq, ctx_per_seq):
    i = pl.program_id(0)
    taps = dw_ref.shape[0]
    pad = taps // 2
    idx = jnp.where(i < n_lat_tiles, i % lat_per_seq, (i - n_lat_tiles) % ctx_per_seq)
    per_seq = jnp.where(i < n_lat_tiles, lat_per_seq, ctx_per_seq)
    buf_ref[0:halo, :] = jnp.where(idx > 0, xp_ref[...], 0.0)
    buf_ref[halo:halo + tt, :] = x_ref[...]
    buf_ref[halo + tt:, :] = jnp.where(idx < per_seq - 1, xn_ref[...], 0.0)
    nchunk = x_ref.shape[1] // V7X_LANES

    rows = tt + 2 * halo

    def chunk(c, carry):
        cs = pl.ds(pl.multiple_of(c * V7X_LANES, V7X_LANES), V7X_LANES)
        xb = buf_ref[:, cs]
        for s in range(1, V7X_F32_SUBLANES):
            shift_ref[s] = pltpu.roll(xb, rows - s, 0)
        acc = jnp.broadcast_to(db_ref[:, cs], (tt, V7X_LANES))
        for k in range(taps):
            off = halo - pad + k
            base, s = off - off % V7X_F32_SUBLANES, off % V7X_F32_SUBLANES
            win = buf_ref[pl.ds(base, tt), cs] if s == 0 else shift_ref[s, base:base + tt, :]
            acc = acc + win * dw_ref[pl.ds(k, 1), cs]
        y_ref[:, cs] = acc
        return carry

    lax.fori_loop(0, nchunk, chunk, 0)
    yn = _layer_norm(y_ref[...], g_ref[...], b_ref[...])
    o_ref[...] = (yn * _sigmoid(yn)).astype(BF16)


def _conformer_conv(glu, dw_w, dw_b, n_g, n_b, *, nbatch, seq, ctx_len):
    n, cw = glu.shape
    tt = _pick((seq, ctx_len), (256, 128))
    halo = 16
    assert dw_w.shape[0] // 2 <= halo
    n_lat_tiles = nbatch * seq // tt
    hb = tt // halo
    nhalo = n // halo
    return pl.pallas_call(
        functools.partial(_conv_kernel, tt=tt, halo=halo, n_lat_tiles=n_lat_tiles,
                          lat_per_seq=seq // tt, ctx_per_seq=ctx_len // tt),
        grid=(n // tt,),
        in_specs=[
            pl.BlockSpec((tt, cw), lambda i: (i, 0)),
            pl.BlockSpec((halo, cw), lambda i: (jnp.maximum(i * hb - 1, 0), 0)),
            pl.BlockSpec((halo, cw), lambda i: (jnp.minimum((i + 1) * hb, nhalo - 1), 0)),
            pl.BlockSpec(dw_w.shape, lambda i: (0, 0)),
            pl.BlockSpec((1, cw), lambda i: (0, 0)),
            pl.BlockSpec((1, cw), lambda i: (0, 0)),
            pl.BlockSpec((1, cw), lambda i: (0, 0)),
        ],
        out_specs=pl.BlockSpec((tt, cw), lambda i: (i, 0)),
        out_shape=jax.ShapeDtypeStruct((n, cw), BF16),
        scratch_shapes=[pltpu.VMEM((tt + 2 * halo, cw), F32), pltpu.VMEM((tt, cw), F32),
                        pltpu.VMEM((V7X_F32_SUBLANES, tt + 2 * halo, V7X_LANES), F32)],
        compiler_params=_params("arbitrary"),
        name="conformer_conv",
    )(glu, glu, glu, dw_w, dw_b.reshape(1, cw), n_g.reshape(1, cw), n_b.reshape(1, cw))


def _outproj_kernel(*refs, n_parts, n_res, n_lat_tiles, alpha):
    part_refs = refs[:n_parts]
    res_refs = refs[n_parts + 1:n_parts + 1 + n_res]
    w_ref = refs[n_parts]
    gate_ref, sh_ref, sc_ref, lg_ref, lb_ref, x1_ref, h2_ref = refs[n_parts + 1 + n_res:]
    y = None
    k0 = 0
    for p_ref in part_refs:
        kw = p_ref.shape[1]
        t = _dot(p_ref[...], w_ref[k0:k0 + kw, :])
        y = t if y is None else y + t
        k0 += kw

    def finish(x_ref):
        x1 = _layer_norm(alpha * x_ref[...] + gate_ref[0] * y, lg_ref[...], lb_ref[...])
        x1_ref[...] = x1
        h2_ref[...] = (x1 * (1.0 + sc_ref[0]) + sh_ref[0]).astype(BF16)

    if n_res == 1:
        finish(res_refs[0])
    else:
        is_lat = pl.program_id(0) < n_lat_tiles
        pl.when(is_lat)(lambda: finish(res_refs[0]))
        pl.when(jnp.logical_not(is_lat))(lambda: finish(res_refs[1]))


def _outproj(parts, w_out, residuals, mod3, mod_base, ln_g, ln_b, *, n_out, tm, n_lat_tiles, seq, nbatch, alpha):
    d = residuals[0].shape[1]
    if len(residuals) == 1:
        res_specs = [pl.BlockSpec((tm, d), lambda i: (i, 0))]
    else:
        res_specs = [pl.BlockSpec((tm, d), lambda i: (jnp.minimum(i, n_lat_tiles - 1), 0)),
                     pl.BlockSpec((tm, d), lambda i: (jnp.maximum(i - n_lat_tiles, 0), 0))]

    def cond_row(i):
        return jnp.where(i < n_lat_tiles, (i * tm) // seq, nbatch)

    def mod_spec(k):
        return pl.BlockSpec((1, 1, d), lambda i: (mod_base + cond_row(i) * 6 + k, 0, 0))

    return pl.pallas_call(
        functools.partial(_outproj_kernel, n_parts=len(parts), n_res=len(residuals), n_lat_tiles=n_lat_tiles,
                          alpha=alpha),
        grid=(n_out // tm,),
        in_specs=[pl.BlockSpec((tm, p.shape[1]), lambda i: (i, 0)) for p in parts] + [
            pl.BlockSpec(w_out.shape, lambda i: (0, 0), pipeline_mode=pl.Buffered(1))] + res_specs + [
            mod_spec(2), mod_spec(3), mod_spec(4),
            pl.BlockSpec((1, d), lambda i: (0, 0)),
            pl.BlockSpec((1, d), lambda i: (0, 0)),
        ],
        out_specs=[pl.BlockSpec((tm, d), lambda i: (i, 0)), pl.BlockSpec((tm, d), lambda i: (i, 0))],
        out_shape=[jax.ShapeDtypeStruct((n_out, d), F32), jax.ShapeDtypeStruct((n_out, d), BF16)],
        compiler_params=_params("arbitrary"),
        name="outproj_ln",
    )(*parts, w_out, *residuals, mod3, mod3, mod3, ln_g.reshape(1, d), ln_b.reshape(1, d))


def _ffn_kernel(h_ref, hp_ref, hn_ref, wa_ref, wg_ref, dwa_ref, dwg_ref, dba_ref, dbg_ref, wd_ref,
                x_ref, gate_ref, lg_ref, lb_ref, o_ref, hext_ref, za0_ref, za1_ref, zg0_ref, zg1_ref, *,
                tm, halo, sub, nf, seq, ctx_len, n_lat_tiles, alpha):
    za_refs = (za0_ref, za1_ref)
    zg_refs = (zg0_ref, zg1_ref)
    i = pl.program_id(0)
    c = pl.program_id(1)
    is_lat = i < n_lat_tiles
    row = lax.broadcasted_iota(jnp.int32, (tm, 1), 0) + i * tm
    pos = row & (ctx_len - 1)
    has_prev = pos != 0
    has_next = pos != ctx_len - 1

    chunks = [slice(s0, s0 + sub) for s0 in range(0, wa_ref.shape[1], sub)]

    def up_a(slot, cs):
        za_refs[slot][:, cs] = _dot(hext_ref[...], wa_ref[:, cs])

    def up_g(slot, cs):
        zg_refs[slot][:, cs] = _dot(hext_ref[...], wg_ref[:, cs])

    def conv(z_ref, dw_ref, db_ref, cs, masked):
        z_prev = z_ref[halo - 1:halo - 1 + tm, cs]
        z_next = z_ref[halo + 1:halo + 1 + tm, cs]
        if masked:
            z_prev = jnp.where(has_prev, z_prev, 0.0)
            z_next = jnp.where(has_next, z_next, 0.0)
        return (z_prev * dw_ref[0:1, cs] + z_ref[halo:halo + tm, cs] * dw_ref[1:2, cs] + z_next * dw_ref[2:3, cs]
                + db_ref[:, cs])

    def step(up_slot, down_slot, masked=False):
        if up_slot is not None:
            for cs in chunks:
                up_a(up_slot, cs)
                up_g(up_slot, cs)
        if down_slot is not None:
            contrib = None
            for cs in chunks:
                a = conv(za_refs[down_slot], dwa_ref, dba_ref, cs, masked)
                g = conv(zg_refs[down_slot], dwg_ref, dbg_ref, cs, masked)
                t = _dot((g * _sigmoid(g) * a).astype(BF16), wd_ref[cs, :])
                contrib = t if contrib is None else contrib + t
            o_ref[...] += contrib

    @pl.when(c == 0)
    def _():
        starts_seq = is_lat & ((i * tm) % seq == 0)
        ends_seq = is_lat & (((i + 1) * tm) % seq == 0)
        zero_halo = jnp.zeros(hp_ref.shape, hp_ref.dtype)
        hext_ref[0:halo, :] = jnp.where(starts_seq, zero_halo, hp_ref[...])
        hext_ref[halo:halo + tm, :] = h_ref[...]
        hext_ref[halo + tm:, :] = jnp.where(ends_seq, zero_halo, hn_ref[...])
        o_ref[...] = jnp.zeros_like(o_ref)
        step(0, None)

    for masked in (False, True):
        for slot in (0, 1):
            @pl.when((c >= 1) & (c < nf) & (c % 2 == slot) & (is_lat != masked))
            def _():
                step(slot, 1 - slot, masked)

        @pl.when((c == nf) & (is_lat != masked))
        def _():
            step(None, (nf - 1) % 2, masked)

    @pl.when(c == nf)
    def _():
        o_ref[...] = _layer_norm(alpha * x_ref[...] + gate_ref[0] * o_ref[...], lg_ref[...], lb_ref[...])


def _ffn(h2, x1, layer, w_up, dw_w, dw_b, w_down, mod3, mod_base, ln_g, ln_b, *, tm, n_lat_tiles, seq, ctx_len,
         nbatch, alpha):
    n_out, d = x1.shape
    depth, d_ff = w_down.shape[:2]
    tf = _pick((d_ff,), (512, 256, 128))
    nf = d_ff // tf
    halo = V7X_BF16_SUBLANES
    hb = tm // halo
    nhalo = h2.shape[0] // halo
    ntiles = n_out // tm
    assert dw_w.shape[1] == 3

    def cond_row(i):
        return jnp.where(i < n_lat_tiles, (i * tm) // seq, nbatch)

    dw_b3 = dw_b.reshape(depth, 1, 2 * d_ff)

    def up(c):
        return jnp.minimum(c, nf - 1)

    def down(c):
        return jnp.maximum(c - 1, 0)

    mid = nf // 2

    def h_tile(i, c):
        return jnp.where(c < mid, i, jnp.minimum(i + 1, ntiles - 1))

    def x_tile(i, c):
        return jnp.where(c < mid, jnp.maximum(i - 1, 0), i)

    return pl.pallas_call(
        functools.partial(_ffn_kernel, tm=tm, halo=halo, sub=min(tf, 256), nf=nf, seq=seq, ctx_len=ctx_len,
                          n_lat_tiles=n_lat_tiles, alpha=alpha),
        grid=(ntiles, nf + 1),
        in_specs=[
            pl.BlockSpec((tm, d), lambda i, c: (h_tile(i, c), 0)),
            pl.BlockSpec((halo, d), lambda i, c: (jnp.maximum(i * hb - 1, 0), 0)),
            pl.BlockSpec((halo, d), lambda i, c: (jnp.minimum((i + 1) * hb, nhalo - 1), 0)),
            pl.BlockSpec((None, d, tf), lambda i, c: (layer, 0, up(c))),
            pl.BlockSpec((None, d, tf), lambda i, c: (layer, 0, nf + up(c))),
            pl.BlockSpec((None, 3, tf), lambda i, c: (layer, 0, down(c))),
            pl.BlockSpec((None, 3, tf), lambda i, c: (layer, 0, nf + down(c))),
            pl.BlockSpec((None, 1, tf), lambda i, c: (layer, 0, down(c))),
            pl.BlockSpec((None, 1, tf), lambda i, c: (layer, 0, nf + down(c))),
            pl.BlockSpec((None, tf, d), lambda i, c: (layer, down(c), 0)),
            pl.BlockSpec((tm, d), lambda i, c: (x_tile(i, c), 0)),
            pl.BlockSpec((1, 1, d), lambda i, c: (mod_base + cond_row(i) * 6 + 5, 0, 0)),
            pl.BlockSpec((1, d), lambda i, c: (0, 0)),
            pl.BlockSpec((1, d), lambda i, c: (0, 0)),
        ],
        out_specs=pl.BlockSpec((tm, d), lambda i, c: (i, 0)),
        out_shape=jax.ShapeDtypeStruct((n_out, d), F32),
        scratch_shapes=[pltpu.VMEM((tm + 2 * halo, d), BF16)] + [pltpu.VMEM((tm + 2 * halo, tf), F32)] * 4,
        compiler_params=_params("arbitrary", "arbitrary"),
        name="conv_ffn",
    )(h2, h2, h2, w_up, w_up, dw_w, dw_w, dw_b3, dw_b3, w_down, x1, mod3, ln_g.reshape(1, d), ln_b.reshape(1, d))


def _inproj_c_kernel(x_ref, sh_ref, sc_ref, w_ref, o_ref, *, q_scale):
    hb = (x_ref[...] * (1.0 + sc_ref[0]) + sh_ref[0]).astype(BF16)
    d = x_ref.shape[1]
    cols = 512
    for c0 in range(0, o_ref.shape[1], cols):
        z = _dot(hb, w_ref[:, c0:c0 + cols])
        if c0 < d:
            z = z * q_scale
        o_ref[:, c0:c0 + cols] = z.astype(BF16)


def _inproj_c(xc, mod3, mod_base, w_in, *, tm, n_lat_tiles, seq, nbatch):
    n, d = xc.shape
    n_in = w_in.shape[1]
    assert d % 512 == 0 and n_in == 3 * d

    def cond_row(i):
        return jnp.where(i < n_lat_tiles, (i * tm) // seq, nbatch)

    return pl.pallas_call(
        functools.partial(_inproj_c_kernel, q_scale=HEAD_DIM ** -0.5),
        grid=(n // tm,),
        in_specs=[
            pl.BlockSpec((tm, d), lambda i: (i, 0)),
            pl.BlockSpec((1, 1, d), lambda i: (mod_base + cond_row(i) * 6 + 0, 0, 0)),
            pl.BlockSpec((1, 1, d), lambda i: (mod_base + cond_row(i) * 6 + 1, 0, 0)),
            pl.BlockSpec((d, n_in), lambda i: (0, 0), pipeline_mode=pl.Buffered(1)),
        ],
        out_specs=pl.BlockSpec((tm, n_in), lambda i: (i, 0)),
        out_shape=jax.ShapeDtypeStruct((n, n_in), BF16),
        compiler_params=_params("arbitrary"),
        name="inproj_c",
    )(xc, mod3, mod3, w_in)


def _na_bias_table(rel_bias):
    nheads, nrow, ncol = rel_bias.shape
    qc = np.arange(GRID_W)[:, None]
    kc = np.arange(GRID_W)[None, :]
    cstart = np.clip(qc - NA_COLS // 2, 0, GRID_W - NA_COLS)
    valid = (kc >= cstart) & (kc < cstart + NA_COLS)
    period = 2 * GRID_W - 1
    lead = GRID_W - NA_COLS - 1
    padded = jnp.pad(rel_bias, ((0, 0), (0, 0), (lead, period - lead - ncol)), constant_values=NEG)
    tiled = jnp.broadcast_to(padded[:, :, None, :], (nheads, nrow, GRID_W, period))
    skew = tiled.reshape(nheads, nrow, GRID_W * period)[:, :, :GRID_W * (period - 1)]
    toep = skew.reshape(nheads, nrow, GRID_W, period - 1)[:, :, :, GRID_W - 2:]
    toep = jnp.where(valid[None, None], toep, NEG)
    neg = jnp.full((nheads, 1, GRID_W, GRID_W), NEG, F32)
    ext = jnp.concatenate([neg, toep, neg], axis=1)
    return jnp.concatenate([ext[:, :2 * NA_ROWS], ext[:, 1:]], axis=-1)


def _natten_kernel(q_ref, k_ref, v_ref, kc_ref, vc_ref, bias_ref, o_ref, *, rows):
    nq = NA_QROWS * GRID_W
    nk = NA_KROWS * GRID_W
    left = lax.broadcasted_iota(jnp.int32, (GRID_W, 2 * GRID_W), 1) < GRID_W
    for blk in range(rows // NA_QROWS):
        ks = min(max(NA_QROWS * blk - NA_ROWS // 2, 0), rows - NA_KROWS)
        q = q_ref[blk * nq:(blk + 1) * nq, :]
        s_loc = _dot_t(q, k_ref[ks * GRID_W:ks * GRID_W + nk, :])
        bias_rows = []
        for qi in range(NA_QROWS):
            qr = NA_QROWS * blk + qi
            r0 = min(max(qr - NA_ROWS // 2, 0), rows - NA_ROWS)
            slabs = []
            for j in range(NA_KROWS // 2):
                kr = ks + 2 * j
                ok0 = r0 <= kr < r0 + NA_ROWS
                ok1 = r0 <= kr + 1 < r0 + NA_ROWS
                if not (ok0 or ok1):
                    slabs.append(jnp.full((GRID_W, 2 * GRID_W), NEG, F32))
                    continue
                slab = bias_ref[0, kr - qr + NA_ROWS]
                if not ok1:
                    slab = jnp.where(left, slab, NEG)
                elif not ok0:
                    slab = jnp.where(left, NEG, slab)
                slabs.append(slab)
            bias_rows.append(jnp.concatenate(slabs, axis=1))
        s_loc = s_loc + jnp.concatenate(bias_rows, axis=0)
        s_ctx = _dot_t(q, kc_ref[...])
        m = jnp.maximum(jnp.max(s_loc, axis=-1, keepdims=True), jnp.max(s_ctx, axis=-1, keepdims=True))
        p_loc = jnp.exp(s_loc - m)
        p_ctx = jnp.exp(s_ctx - m)
        l = jnp.sum(p_loc, axis=-1, keepdims=True) + jnp.sum(p_ctx, axis=-1, keepdims=True)
        o = (_dot(p_loc.astype(BF16), v_ref[ks * GRID_W:ks * GRID_W + nk, :])
             + _dot(p_ctx.astype(BF16), vc_ref[...]))
        o_ref[blk * nq:(blk + 1) * nq, :] = (o / l).astype(BF16)


def _natten(qkv, bias_tab, *, nbatch, seq, ctx_len, d):
    rows = seq // GRID_W
    nheads = d // HEAD_DIM
    assert rows % NA_QROWS == 0 and rows >= NA_KROWS
    ctx_blk0 = nbatch * seq // ctx_len
    return pl.pallas_call(
        functools.partial(_natten_kernel, rows=rows),
        grid=(nheads, nbatch),
        in_specs=[
            pl.BlockSpec((seq, HEAD_DIM), lambda h, b: (b, h)),
            pl.BlockSpec((seq, HEAD_DIM), lambda h, b: (b, nheads + h)),
            pl.BlockSpec((seq, HEAD_DIM), lambda h, b: (b, 2 * nheads + h)),
            pl.BlockSpec((ctx_len, HEAD_DIM), lambda h, b: (ctx_blk0 + b, nheads + h)),
            pl.BlockSpec((ctx_len, HEAD_DIM), lambda h, b: (ctx_blk0 + b, 2 * nheads + h)),
            pl.BlockSpec((1,) + bias_tab.shape[1:], lambda h, b: (h, 0, 0, 0)),
        ],
        out_specs=pl.BlockSpec((seq, HEAD_DIM), lambda h, b: (b, h)),
        out_shape=jax.ShapeDtypeStruct((nbatch * seq, d), BF16),
        compiler_params=_params("arbitrary", "arbitrary"),
        name="natten",
    )(qkv, qkv, qkv, qkv, qkv, bias_tab)


def kernel(x, c, ctx, c_ctx, ada_w, ada_b, post_ln_g, post_ln_b, ab_w_in, ab_w_out, ab_q_gain, ab_k_gain,
           ab_dw_w, ab_dw_b, ab_norm_g, ab_norm_b, c_w_in, c_w_out, c_rel_bias, ffn_w_up, ffn_dw_w, ffn_dw_b,
           ffn_w_down):
    nbatch, seq, d = x.shape
    ctx_len = ctx.shape[1]
    depth = ada_w.shape[0]
    assert depth == 2 and nbatch + 1 <= COND_ROWS and nbatch * ctx_len <= seq
    assert seq % GRID_W == 0 and seq & (seq - 1) == 0 and ctx_len & (ctx_len - 1) == 0 and ctx_len <= seq
    nx, nc = nbatch * seq, nbatch * ctx_len
    alpha = (2 * depth) ** 0.25
    tm = _pick((seq, nc), (512, 256))
    n_lat_tiles = nx // tm
    geo = dict(tm=tm, n_lat_tiles=n_lat_tiles, seq=seq, nbatch=nbatch)

    cond = jnp.concatenate([c, c_ctx[None], jnp.zeros((COND_ROWS - nbatch - 1, d), F32)], axis=0)
    mod3 = _ada(cond, ada_w, ada_b).reshape(depth * COND_ROWS * 6, 1, d)
    x2d, c2d = x.reshape(nx, d), ctx.reshape(nc, d)
    ffn_w = (ffn_w_up.astype(BF16), ffn_dw_w, ffn_dw_b, ffn_w_down.astype(BF16))

    mod_base = 0
    qkv, glu = _inproj_ab(x2d, c2d, mod3, mod_base, ab_w_in[0].astype(BF16), ab_q_gain[0][None],
                          ab_k_gain[0][None], _rope_tables(seq, tm), **geo)
    attn = _gqa(qkv, nbatch=nbatch, seq=seq, ctx_len=ctx_len)
    conv = _conformer_conv(glu, ab_dw_w[0], ab_dw_b[0], ab_norm_g[0], ab_norm_b[0],
                           nbatch=nbatch, seq=seq, ctx_len=ctx_len)
    x1, h2 = _outproj([attn, conv], ab_w_out[0].astype(BF16), [x2d, c2d], mod3, mod_base, post_ln_g[0, 0],
                      post_ln_b[0, 0], n_out=nx + nc, alpha=alpha, **geo)
    xc = _ffn(h2, x1, 0, *ffn_w, mod3, mod_base, post_ln_g[0, 1], post_ln_b[0, 1], ctx_len=ctx_len, alpha=alpha,
              **geo)

    mod_base = COND_ROWS * 6
    qkv = _inproj_c(xc, mod3, mod_base, c_w_in[0].astype(BF16), **geo)
    attn = _natten(qkv, _na_bias_table(c_rel_bias[0]), nbatch=nbatch, seq=seq, ctx_len=ctx_len, d=d)
    x1, h2 = _outproj([attn], c_w_out[0].astype(BF16), [xc], mod3, mod_base, post_ln_g[1, 0], post_ln_b[1, 0],
                      n_out=nx, alpha=alpha, **geo)
    out = _ffn(h2, x1, 1, *ffn_w, mod3, mod_base, post_ln_g[1, 1], post_ln_b[1, 1], ctx_len=ctx_len, alpha=alpha,
               **geo)
    return out.reshape(nbatch, seq, d)
```

```python
import functools

import jax
import jax.numpy as jnp
import numpy as np
from jax import lax
from jax.experimental import pallas as pl
from jax.experimental.pallas import tpu as pltpu

F32 = jnp.float32
BF16 = jnp.bfloat16

HEAD_DIM = 128
GRID_W = 64
A_HEADS = 8
A_KV_HEADS = 2
A_GROUP = A_HEADS // A_KV_HEADS
ROPE_THETA = 10000.0
NA_ROWS = 8
NA_COLS = 16
LN_EPS = 1e-6
NEG = -1e30
LOG2E = 1.4426950408889634
ATTN_Q_SCALE = HEAD_DIM ** -0.5 * LOG2E

V7X_LANES = 128
V7X_F32_SUBLANES = 8
V7X_BF16_SUBLANES = 16
V7X_VMEM_LIMIT = 56 * 1024 * 1024
COND_ROWS = 8

NA_QROWS = 4
NA_KROWS = NA_QROWS + NA_ROWS


def _pick(n, candidates):
    for t in candidates:
        if all(v % t == 0 for v in n):
            return t
    raise ValueError(f"no tile in {candidates} divides {n}")


def _params(*sem, flags=None):
    return pltpu.CompilerParams(dimension_semantics=sem, vmem_limit_bytes=V7X_VMEM_LIMIT, flags=flags)


def _layer_norm(r, g, b):
    mu = jnp.mean(r, axis=-1, keepdims=True)
    d = r - mu
    var = jnp.mean(d * d, axis=-1, keepdims=True)
    return d * lax.rsqrt(var + LN_EPS) * g + b


def _sigmoid(v):
    return 1.0 / (1.0 + jnp.exp(-v))


def _dot(a, b):
    return jnp.dot(a, b, preferred_element_type=F32)


def _dot_t(a, b):
    return lax.dot_general(a, b, (((1,), (1,)), ((), ())), preferred_element_type=F32)


def _ada_kernel(cond_ref, w_ref, b_ref, o_ref):
    cnd = cond_ref[...]
    s = (cnd * _sigmoid(cnd)).astype(BF16)
    o_ref[0] = _dot(s, w_ref[0].astype(BF16)) + b_ref[0]


def _ada(cond, ada_w, ada_b):
    depth, d, n6 = ada_w.shape
    tn = _pick((n6,), (1024, 512, 256, 128))
    return pl.pallas_call(
        _ada_kernel,
        grid=(depth, n6 // tn),
        in_specs=[
            pl.BlockSpec((COND_ROWS, d), lambda l, j: (0, 0)),
            pl.BlockSpec((1, d, tn), lambda l, j: (l, 0, j)),
            pl.BlockSpec((1, 1, tn), lambda l, j: (l, 0, j)),
        ],
        out_specs=pl.BlockSpec((1, COND_ROWS, tn), lambda l, j: (l, 0, j)),
        out_shape=jax.ShapeDtypeStruct((depth, COND_ROWS, n6), F32),
        compiler_params=_params("arbitrary", "arbitrary"),
        name="ada_mod",
    )(cond, ada_w, ada_b.reshape(depth, 1, n6))


def _inproj_ab_kernel(x_ref, c_ref, sh_ref, sc_ref, w_ref, qg_ref, kg_ref, rope_ref, qkv_ref, glu_ref, hb_ref, *,
                      q_scale, n_lat_tiles):
    is_lat = pl.program_id(0) < n_lat_tiles
    for src_ref, cond in ((x_ref, is_lat), (c_ref, jnp.logical_not(is_lat))):
        @pl.when(cond)
        def _():
            hb_ref[...] = (src_ref[...] * (1.0 + sc_ref[0]) + sh_ref[0]).astype(BF16)
    hb = hb_ref[...]
    cos, sin_lo, sin_hi = rope_ref[0], rope_ref[1], rope_ref[2]
    a_w = A_HEADS * HEAD_DIM
    kv_w = A_KV_HEADS * HEAD_DIM
    b_w = glu_ref.shape[1]

    def norm_rope(z, gain):
        zn = z * lax.rsqrt(jnp.mean(z * z, axis=-1, keepdims=True) + LN_EPS) * gain
        quarter = HEAD_DIM // 4
        return (zn * cos + pltpu.roll(zn, HEAD_DIM - quarter, 1) * sin_lo
                + pltpu.roll(zn, quarter, 1) * sin_hi)

    q_gain = qg_ref[...] * q_scale
    heads_per_dot = 4
    for c0 in range(0, a_w, heads_per_dot * HEAD_DIM):
        z = _dot(hb, w_ref[:, c0:c0 + heads_per_dot * HEAD_DIM])
        for hh in range(heads_per_dot):
            cs = slice(c0 + hh * HEAD_DIM, c0 + (hh + 1) * HEAD_DIM)
            qkv_ref[:, cs] = norm_rope(z[:, hh * HEAD_DIM:(hh + 1) * HEAD_DIM], q_gain).astype(BF16)
    z = _dot(hb, w_ref[:, a_w:a_w + 2 * kv_w])
    for hh in range(A_KV_HEADS):
        cs = slice(a_w + hh * HEAD_DIM, a_w + (hh + 1) * HEAD_DIM)
        qkv_ref[:, cs] = norm_rope(z[:, hh * HEAD_DIM:(hh + 1) * HEAD_DIM], kg_ref[...]).astype(BF16)
    qkv_ref[:, a_w + kv_w:a_w + 2 * kv_w] = z[:, kv_w:].astype(BF16)
    u0 = a_w + 2 * kv_w
    glu_cols = 512 if b_w % 512 == 0 else b_w
    for c0 in range(0, b_w, glu_cols):
        u = _dot(hb, w_ref[:, u0 + c0:u0 + c0 + glu_cols])
        g = _dot(hb, w_ref[:, u0 + b_w + c0:u0 + b_w + c0 + glu_cols])
        glu_ref[:, c0:c0 + glu_cols] = u * _sigmoid(g)


def _inproj_ab(x2d, c2d, mod3, mod_base, w_in, q_gain, k_gain, rope, *, tm, n_lat_tiles, seq, nbatch):
    d = x2d.shape[1]
    n = x2d.shape[0] + c2d.shape[0]
    n_in = w_in.shape[1]
    a_w, kv_w = A_HEADS * HEAD_DIM, A_KV_HEADS * HEAD_DIM
    b_w = (n_in - a_w - 2 * kv_w) // 2
    tiles_per_seq = seq // tm

    def cond_row(i):
        return jnp.where(i < n_lat_tiles, (i * tm) // seq, nbatch)

    def rope_blk(i):
        return jnp.where(i < n_lat_tiles, i % tiles_per_seq, tiles_per_seq)

    return pl.pallas_call(
        functools.partial(_inproj_ab_kernel, q_scale=ATTN_Q_SCALE, n_lat_tiles=n_lat_tiles),
        grid=(n // tm,),
        in_specs=[
            pl.BlockSpec((tm, d), lambda i: (jnp.minimum(i, n_lat_tiles - 1), 0)),
            pl.BlockSpec((tm, d), lambda i: (jnp.maximum(i - n_lat_tiles, 0), 0)),
            pl.BlockSpec((1, 1, d), lambda i: (mod_base + cond_row(i) * 6 + 0, 0, 0)),
            pl.BlockSpec((1, 1, d), lambda i: (mod_base + cond_row(i) * 6 + 1, 0, 0)),
            pl.BlockSpec((d, n_in), lambda i: (0, 0), pipeline_mode=pl.Buffered(1)),
            pl.BlockSpec((1, HEAD_DIM), lambda i: (0, 0)),
            pl.BlockSpec((1, HEAD_DIM), lambda i: (0, 0)),
            pl.BlockSpec((3, tm, HEAD_DIM), lambda i: (0, rope_blk(i), 0)),
        ],
        out_specs=[
            pl.BlockSpec((tm, a_w + 2 * kv_w), lambda i: (i, 0)),
            pl.BlockSpec((tm, b_w), lambda i: (i, 0)),
        ],
        out_shape=[
            jax.ShapeDtypeStruct((n, a_w + 2 * kv_w), BF16),
            jax.ShapeDtypeStruct((n, b_w), F32),
        ],
        scratch_shapes=[pltpu.VMEM((tm, d), BF16)],
        compiler_params=_params("arbitrary"),
        name="inproj_ab",
    )(x2d, c2d, mod3, mod3, w_in, q_gain, k_gain, rope)


def _rope_tables(seq, tm):
    t = jnp.arange(seq)
    nfreq = HEAD_DIM // 4
    inv = ROPE_THETA ** (-jnp.arange(nfreq, dtype=F32) / nfreq)
    ang_r = (t // GRID_W).astype(F32)[:, None] * inv
    ang_c = (t % GRID_W).astype(F32)[:, None] * inv
    ang = jnp.concatenate([ang_r, ang_r, ang_c, ang_c], axis=-1)
    cos, sin = jnp.cos(ang), jnp.sin(ang)
    low = (np.arange(HEAD_DIM) // nfreq) % 2 == 0
    sin_lo = jnp.where(low, -sin, 0.0)
    sin_hi = jnp.where(low, 0.0, sin)
    tab = jnp.stack([cos, sin_lo, sin_hi])
    ident = jnp.stack([jnp.ones((tm, HEAD_DIM), F32), jnp.zeros((tm, HEAD_DIM), F32), jnp.zeros((tm, HEAD_DIM), F32)])
    return jnp.concatenate([tab, ident], axis=1)


def _softmax_pv(q, parts):
    ss = [_dot_t(q, k_ref[...]) for k_ref, _ in parts]
    m = functools.reduce(jnp.maximum, [jnp.max(s, axis=-1, keepdims=True) for s in ss])
    ps = [jnp.exp2(s - m) for s in ss]
    l = functools.reduce(jnp.add, [jnp.sum(p, axis=-1, keepdims=True) for p in ps])
    o = functools.reduce(jnp.add, [_dot(p.astype(BF16), v_ref[...]) for p, (_, v_ref) in zip(ps, parts)])
    return o / l


def _gqa_kernel(q_ref, kx_ref, vx_ref, kc_ref, vc_ref, o_ref, *, n_lat_tiles):
    qi = pl.program_id(1)

    def run(parts):
        for g in range(A_GROUP):
            cs = slice(g * HEAD_DIM, (g + 1) * HEAD_DIM)
            o_ref[:, cs] = _softmax_pv(q_ref[:, cs], parts).astype(BF16)

    @pl.when(qi < n_lat_tiles)
    def _():
        run([(kx_ref, vx_ref), (kc_ref, vc_ref)])

    @pl.when(qi >= n_lat_tiles)
    def _():
        run([(kc_ref, vc_ref)])


def _gqa(qkv, *, nbatch, seq, ctx_len):
    n = qkv.shape[0]
    tq = _pick((seq, ctx_len), (256, 128))
    n_lat_tiles = nbatch * seq // tq
    lat_per_seq, ctx_per_seq = seq // tq, ctx_len // tq
    a_w = A_HEADS * HEAD_DIM
    gw = A_GROUP * HEAD_DIM
    k_col0 = a_w // HEAD_DIM
    v_col0 = k_col0 + A_KV_HEADS

    def batch(qi):
        return jnp.where(qi < n_lat_tiles, qi // lat_per_seq, (qi - n_lat_tiles) // ctx_per_seq)

    def lat_batch(qi):
        return jnp.minimum(qi // lat_per_seq, nbatch - 1)

    ctx_blk0 = nbatch * seq // ctx_len
    return pl.pallas_call(
        functools.partial(_gqa_kernel, n_lat_tiles=n_lat_tiles),
        grid=(A_KV_HEADS, n // tq),
        in_specs=[
            pl.BlockSpec((tq, gw), lambda h, qi: (qi, h)),
            pl.BlockSpec((seq, HEAD_DIM), lambda h, qi: (lat_batch(qi), k_col0 + h)),
            pl.BlockSpec((seq, HEAD_DIM), lambda h, qi: (lat_batch(qi), v_col0 + h)),
            pl.BlockSpec((ctx_len, HEAD_DIM), lambda h, qi: (ctx_blk0 + batch(qi), k_col0 + h)),
            pl.BlockSpec((ctx_len, HEAD_DIM), lambda h, qi: (ctx_blk0 + batch(qi), v_col0 + h)),
        ],
        out_specs=pl.BlockSpec((tq, gw), lambda h, qi: (qi, h)),
        out_shape=jax.ShapeDtypeStruct((n, a_w), BF16),
        compiler_params=_params("arbitrary", "arbitrary"),
        name="gqa_attn",
    )(qkv, qkv, qkv, qkv, qkv)


def _conv_kernel(x_ref, xp_ref, xn_ref, dw_ref, db_ref, g_ref, b_ref, o_ref, buf_ref, y_ref, shift_ref, *,
                 tt, halo, n_lat_tiles, lat_per_seq, ctx_per_seq):
    i = pl.program_id(0)
    taps = dw_ref.shape[0]
    pad = taps // 2
    idx = jnp.where(i < n_lat_tiles, i % lat_per_seq, (i - n_lat_tiles) % ctx_per_seq)
    per_seq = jnp.where(i < n_lat_tiles, lat_per_seq, ctx_per_seq)
    buf_ref[0:halo, :] = jnp.where(idx > 0, xp_ref[...], 0.0)
    buf_ref[halo:halo + tt, :] = x_ref[...]
    buf_ref[halo + tt:, :] = jnp.where(idx < per_seq - 1, xn_ref[...], 0.0)
    nchunk = x_ref.shape[1] // V7X_LANES

    rows = tt + 2 * halo

    def chunk(c, carry):
        cs = pl.ds(pl.multiple_of(c * V7X_LANES, V7X_LANES), V7X_LANES)
        xb = buf_ref[:, cs]
        for s in range(1, V7X_F32_SUBLANES):
            shift_ref[s] = pltpu.roll(xb, rows - s, 0)
        acc = jnp.broadcast_to(db_ref[:, cs], (tt, V7X_LANES))
        for k in range(taps):
            off = halo - pad + k
            base, s = off - off % V7X_F32_SUBLANES, off % V7X_F32_SUBLANES
            win = buf_ref[pl.ds(base, tt), cs] if s == 0 else shift_ref[s, base:base + tt, :]
            acc = acc + win * dw_ref[pl.ds(k, 1), cs]
        y_ref[:, cs] = acc
        return carry

    lax.fori_loop(0, nchunk, chunk, 0)
    yn = _layer_norm(y_ref[...], g_ref[...], b_ref[...])
    o_ref[...] = (yn * _sigmoid(yn)).astype(BF16)


def _conformer_conv(glu, dw_w, dw_b, n_g, n_b, *, nbatch, seq, ctx_len):
    n, cw = glu.shape
    tt = _pick((seq, ctx_len), (256, 128))
    halo = 16
    assert dw_w.shape[0] // 2 <= halo
    n_lat_tiles = nbatch * seq // tt
    hb = tt // halo
    nhalo = n // halo
    return pl.pallas_call(
        functools.partial(_conv_kernel, tt=tt, halo=halo, n_lat_tiles=n_lat_tiles,
                          lat_per_seq=seq // tt, ctx_per_seq=ctx_len // tt),
        grid=(n // tt,),
        in_specs=[
            pl.BlockSpec((tt, cw), lambda i: (i, 0)),
            pl.BlockSpec((halo, cw), lambda i: (jnp.maximum(i * hb - 1, 0), 0)),
            pl.BlockSpec((halo, cw), lambda i: (jnp.minimum((i + 1) * hb, nhalo - 1), 0)),
            pl.BlockSpec(dw_w.shape, lambda i: (0, 0)),
            pl.BlockSpec((1, cw), lambda i: (0, 0)),
            pl.BlockSpec((1, cw), lambda i: (0, 0)),
            pl.BlockSpec((1, cw), lambda i: (0, 0)),
        ],
        out_specs=pl.BlockSpec((tt, cw), lambda i: (i, 0)),
        out_shape=jax.ShapeDtypeStruct((n, cw), BF16),
        scratch_shapes=[pltpu.VMEM((tt + 2 * halo, cw), F32), pltpu.VMEM((tt, cw), F32),
                        pltpu.VMEM((V7X_F32_SUBLANES, tt + 2 * halo, V7X_LANES), F32)],
        compiler_params=_params("arbitrary"),
        name="conformer_conv",
    )(glu, glu, glu, dw_w, dw_b.reshape(1, cw), n_g.reshape(1, cw), n_b.reshape(1, cw))


def _outproj_kernel(*refs, n_parts, n_res, n_lat_tiles, alpha):
    part_refs = refs[:n_parts]
    res_refs = refs[n_parts + 1:n_parts + 1 + n_res]
    w_ref = refs[n_parts]
    gate_ref, sh_ref, sc_ref, lg_ref, lb_ref, x1_ref, h2_ref = refs[n_parts + 1 + n_res:]
    y = None
    k0 = 0
    for p_ref in part_refs:
        kw = p_ref.shape[1]
        t = _dot(p_ref[...], w_ref[k0:k0 + kw, :])
        y = t if y is None else y + t
        k0 += kw

    def finish(x_ref):
        x1 = _layer_norm(alpha * x_ref[...] + gate_ref[0] * y, lg_ref[...], lb_ref[...])
        x1_ref[...] = x1
        h2_ref[...] = (x1 * (1.0 + sc_ref[0]) + sh_ref[0]).astype(BF16)

    if n_res == 1:
        finish(res_refs[0])
    else:
        is_lat = pl.program_id(0) < n_lat_tiles
        pl.when(is_lat)(lambda: finish(res_refs[0]))
        pl.when(jnp.logical_not(is_lat))(lambda: finish(res_refs[1]))


def _outproj(parts, w_out, residuals, mod3, mod_base, ln_g, ln_b, *, n_out, tm, n_lat_tiles, seq, nbatch, alpha):
    d = residuals[0].shape[1]
    if len(residuals) == 1:
        res_specs = [pl.BlockSpec((tm, d), lambda i: (i, 0))]
    else:
        res_specs = [pl.BlockSpec((tm, d), lambda i: (jnp.minimum(i, n_lat_tiles - 1), 0)),
                     pl.BlockSpec((tm, d), lambda i: (jnp.maximum(i - n_lat_tiles, 0), 0))]

    def cond_row(i):
        return jnp.where(i < n_lat_tiles, (i * tm) // seq, nbatch)

    def mod_spec(k):
        return pl.BlockSpec((1, 1, d), lambda i: (mod_base + cond_row(i) * 6 + k, 0, 0))

    return pl.pallas_call(
        functools.partial(_outproj_kernel, n_parts=len(parts), n_res=len(residuals), n_lat_tiles=n_lat_tiles,
                          alpha=alpha),
        grid=(n_out // tm,),
        in_specs=[pl.BlockSpec((tm, p.shape[1]), lambda i: (i, 0)) for p in parts] + [
            pl.BlockSpec(w_out.shape, lambda i: (0, 0), pipeline_mode=pl.Buffered(1))] + res_specs + [
            mod_spec(2), mod_spec(3), mod_spec(4),
            pl.BlockSpec((1, d), lambda i: (0, 0)),
            pl.BlockSpec((1, d), lambda i: (0, 0)),
        ],
        out_specs=[pl.BlockSpec((tm, d), lambda i: (i, 0)), pl.BlockSpec((tm, d), lambda i: (i, 0))],
        out_shape=[jax.ShapeDtypeStruct((n_out, d), F32), jax.ShapeDtypeStruct((n_out, d), BF16)],
        compiler_params=_params("arbitrary"),
        name="outproj_ln",
    )(*parts, w_out, *residuals, mod3, mod3, mod3, ln_g.reshape(1, d), ln_b.reshape(1, d))


def _ffn_kernel(h_ref, hp_ref, hn_ref, wa_ref, wg_ref, dwa_ref, dwg_ref, dba_ref, dbg_ref, wd_ref,
                x_ref, gate_ref, lg_ref, lb_ref, o_ref,
                hext_ref, za0_ref, za1_ref, zg0_ref, zg1_ref, acc_ref, *,
                tm, halo, sub, nf, seq, ctx_len, n_lat_tiles, alpha):
    za_refs = (za0_ref, za1_ref)
    zg_refs = (zg0_ref, zg1_ref)
    i = pl.program_id(0)
    c = pl.program_id(1)
    is_lat = i < n_lat_tiles
    row = lax.broadcasted_iota(jnp.int32, (tm, 1), 0) + i * tm
    pos = row & (ctx_len - 1)
    has_prev = pos != 0
    has_next = pos != ctx_len - 1

    chunks = [slice(s0, s0 + sub) for s0 in range(0, wa_ref.shape[1], sub)]

    def up_a(slot, cs):
        za_refs[slot][:, cs] = _dot(hext_ref[...], wa_ref[:, cs])

    def up_g(slot, cs):
        zg_refs[slot][:, cs] = _dot(hext_ref[...], wg_ref[:, cs])

    def conv(z_ref, dw_ref, db_ref, cs, masked):
        z_prev = z_ref[halo - 1:halo - 1 + tm, cs]
        z_next = z_ref[halo + 1:halo + 1 + tm, cs]
        if masked:
            z_prev = jnp.where(has_prev, z_prev, 0.0)
            z_next = jnp.where(has_next, z_next, 0.0)
        return (z_prev * dw_ref[0:1, cs] + z_ref[halo:halo + tm, cs] * dw_ref[1:2, cs] + z_next * dw_ref[2:3, cs]
                + db_ref[:, cs])

    def step(up_slot, down_slot, masked=False):
        if up_slot is not None:
            for cs in chunks:
                up_a(up_slot, cs)
                up_g(up_slot, cs)
        if down_slot is not None:
            contrib = None
            for cs in chunks:
                a = conv(za_refs[down_slot], dwa_ref, dba_ref, cs, masked)
                g = conv(zg_refs[down_slot], dwg_ref, dbg_ref, cs, masked)
                t = _dot_t(wd_ref[:, cs], (g * _sigmoid(g) * a).astype(BF16))
                contrib = t if contrib is None else contrib + t
            acc_ref[...] += contrib

    @pl.when(c == 0)
    def _():
        starts_seq = is_lat & ((i * tm) % seq == 0)
        ends_seq = is_lat & (((i + 1) * tm) % seq == 0)
        zero_halo = jnp.zeros(hp_ref.shape, hp_ref.dtype)
        hext_ref[0:halo, :] = jnp.where(starts_seq, zero_halo, hp_ref[...])
        hext_ref[halo:halo + tm, :] = h_ref[...]
        hext_ref[halo + tm:, :] = jnp.where(ends_seq, zero_halo, hn_ref[...])
        acc_ref[...] = jnp.zeros_like(acc_ref)
        step(0, None)

    for masked in (False, True):
        for slot in (0, 1):
            @pl.when((c >= 1) & (c < nf) & (c % 2 == slot) & (is_lat != masked))
            def _():
                step(slot, 1 - slot, masked)

        @pl.when((c == nf) & (is_lat != masked))
        def _():
            step(None, (nf - 1) % 2, masked)

    @pl.when(c == nf)
    def _():
        o_ref[...] = _layer_norm(alpha * x_ref[...] + gate_ref[0] * acc_ref[...].T, lg_ref[...], lb_ref[...])


def _ffn(h2, x1, layer, w_up, dw_w, dw_b, w_down, mod3, mod_base, ln_g, ln_b, *, tm, n_lat_tiles, seq, ctx_len,
         nbatch, alpha):
    n_out, d = x1.shape
    depth, _, d_ff = w_down.shape
    tf = _pick((d_ff,), (512, 256, 128))
    nf = d_ff // tf
    halo = V7X_BF16_SUBLANES
    hb = tm // halo
    nhalo = h2.shape[0] // halo
    ntiles = n_out // tm
    assert dw_w.shape[1] == 3

    def cond_row(i):
        return jnp.where(i < n_lat_tiles, (i * tm) // seq, nbatch)

    dw_b3 = dw_b.reshape(depth, 1, 2 * d_ff)

    def up(c):
        return jnp.minimum(c, nf - 1)

    def down(c):
        return jnp.maximum(c - 1, 0)

    mid = nf // 2

    def h_tile(i, c):
        return jnp.where(c < mid, i, jnp.minimum(i + 1, ntiles - 1))

    def x_tile(i, c):
        return jnp.where(c < mid, jnp.maximum(i - 1, 0), i)

    return pl.pallas_call(
        functools.partial(_ffn_kernel, tm=tm, halo=halo, sub=min(tf, 256), nf=nf, seq=seq, ctx_len=ctx_len,
                          n_lat_tiles=n_lat_tiles, alpha=alpha),
        grid=(ntiles, nf + 1),
        in_specs=[
            pl.BlockSpec((tm, d), lambda i, c: (h_tile(i, c), 0)),
            pl.BlockSpec((halo, d), lambda i, c: (jnp.maximum(i * hb - 1, 0), 0)),
            pl.BlockSpec((halo, d), lambda i, c: (jnp.minimum((i + 1) * hb, nhalo - 1), 0)),
            pl.BlockSpec((None, d, tf), lambda i, c: (layer, 0, up(c))),
            pl.BlockSpec((None, d, tf), lambda i, c: (layer, 0, nf + up(c))),
            pl.BlockSpec((None, 3, tf), lambda i, c: (layer, 0, down(c))),
            pl.BlockSpec((None, 3, tf), lambda i, c: (layer, 0, nf + down(c))),
            pl.BlockSpec((None, 1, tf), lambda i, c: (layer, 0, down(c))),
            pl.BlockSpec((None, 1, tf), lambda i, c: (layer, 0, nf + down(c))),
            pl.BlockSpec((None, d, tf), lambda i, c: (layer, 0, down(c))),
            pl.BlockSpec((tm, d), lambda i, c: (x_tile(i, c), 0)),
            pl.BlockSpec((1, 1, d), lambda i, c: (mod_base + cond_row(i) * 6 + 5, 0, 0)),
            pl.BlockSpec((1, d), lambda i, c: (0, 0)),
            pl.BlockSpec((1, d), lambda i, c: (0, 0)),
        ],
        out_specs=pl.BlockSpec((tm, d), lambda i, c: (i, 0)),
        out_shape=jax.ShapeDtypeStruct((n_out, d), F32),
        scratch_shapes=([pltpu.VMEM((tm + 2 * halo, d), BF16)] + [pltpu.VMEM((tm + 2 * halo, tf), F32)] * 4
                        + [pltpu.VMEM((d, tm), F32)]),
        compiler_params=_params("arbitrary", "arbitrary"),
        name="conv_ffn",
    )(h2, h2, h2, w_up, w_up, dw_w, dw_w, dw_b3, dw_b3, w_down, x1, mod3, ln_g.reshape(1, d), ln_b.reshape(1, d))


def _inproj_c_kernel(x_ref, sh_ref, sc_ref, w_ref, o_ref, *, q_scale):
    hb = (x_ref[...] * (1.0 + sc_ref[0]) + sh_ref[0]).astype(BF16)
    d = x_ref.shape[1]
    cols = 512
    for c0 in range(0, o_ref.shape[1], cols):
        z = _dot(hb, w_ref[:, c0:c0 + cols])
        if c0 < d:
            z = z * q_scale
        o_ref[:, c0:c0 + cols] = z.astype(BF16)


def _inproj_c(xc, mod3, mod_base, w_in, *, tm, n_lat_tiles, seq, nbatch):
    n, d = xc.shape
    n_in = w_in.shape[1]
    assert d % 512 == 0 and n_in == 3 * d

    def cond_row(i):
        return jnp.where(i < n_lat_tiles, (i * tm) // seq, nbatch)

    return pl.pallas_call(
        functools.partial(_inproj_c_kernel, q_scale=ATTN_Q_SCALE),
        grid=(n // tm,),
        in_specs=[
            pl.BlockSpec((tm, d), lambda i: (i, 0)),
            pl.BlockSpec((1, 1, d), lambda i: (mod_base + cond_row(i) * 6 + 0, 0, 0)),
            pl.BlockSpec((1, 1, d), lambda i: (mod_base + cond_row(i) * 6 + 1, 0, 0)),
            pl.BlockSpec((d, n_in), lambda i: (0, 0), pipeline_mode=pl.Buffered(1)),
        ],
        out_specs=pl.BlockSpec((tm, n_in), lambda i: (i, 0)),
        out_shape=jax.ShapeDtypeStruct((n, n_in), BF16),
        compiler_params=_params("arbitrary"),
        name="inproj_c",
    )(xc, mod3, mod3, w_in)


def _na_bias_table(rel_bias):
    nheads, nrow, ncol = rel_bias.shape
    qc = np.arange(GRID_W)[:, None]
    kc = np.arange(GRID_W)[None, :]
    cstart = np.clip(qc - NA_COLS // 2, 0, GRID_W - NA_COLS)
    valid = (kc >= cstart) & (kc < cstart + NA_COLS)
    period = 2 * GRID_W - 1
    lead = GRID_W - NA_COLS - 1
    padded = jnp.pad(rel_bias, ((0, 0), (0, 0), (lead, period - lead - ncol)), constant_values=NEG)
    tiled = jnp.broadcast_to(padded[:, :, None, :], (nheads, nrow, GRID_W, period))
    skew = tiled.reshape(nheads, nrow, GRID_W * period)[:, :, :GRID_W * (period - 1)]
    toep = skew.reshape(nheads, nrow, GRID_W, period - 1)[:, :, :, GRID_W - 2:]
    toep = jnp.where(valid[None, None], toep * LOG2E, NEG)
    neg = jnp.full((nheads, 1, GRID_W, GRID_W), NEG, F32)
    ext = jnp.concatenate([neg, toep, neg], axis=1)
    return jnp.concatenate([ext[:, :2 * NA_ROWS], ext[:, 1:]], axis=-1)


def _natten_kernel(q_ref, k_ref, v_ref, kc_ref, vc_ref, bias_ref, o_ref, *, rows):
    nq = NA_QROWS * GRID_W
    nk = NA_KROWS * GRID_W
    left = lax.broadcasted_iota(jnp.int32, (GRID_W, 2 * GRID_W), 1) < GRID_W
    for blk in range(rows // NA_QROWS):
        ks = min(max(NA_QROWS * blk - NA_ROWS // 2, 0), rows - NA_KROWS)
        q = q_ref[blk * nq:(blk + 1) * nq, :]
        s_loc = _dot_t(q, k_ref[ks * GRID_W:ks * GRID_W + nk, :])
        bias_rows = []
        for qi in range(NA_QROWS):
            qr = NA_QROWS * blk + qi
            r0 = min(max(qr - NA_ROWS // 2, 0), rows - NA_ROWS)
            slabs = []
            for j in range(NA_KROWS // 2):
                kr = ks + 2 * j
                ok0 = r0 <= kr < r0 + NA_ROWS
                ok1 = r0 <= kr + 1 < r0 + NA_ROWS
                if not (ok0 or ok1):
                    slabs.append(jnp.full((GRID_W, 2 * GRID_W), NEG, F32))
                    continue
                slab = bias_ref[0, kr - qr + NA_ROWS]
                if not ok1:
                    slab = jnp.where(left, slab, NEG)
                elif not ok0:
                    slab = jnp.where(left, NEG, slab)
                slabs.append(slab)
            bias_rows.append(jnp.concatenate(slabs, axis=1))
        s_loc = s_loc + jnp.concatenate(bias_rows, axis=0)
        s_ctx = _dot_t(q, kc_ref[...])
        m = jnp.maximum(jnp.max(s_loc, axis=-1, keepdims=True), jnp.max(s_ctx, axis=-1, keepdims=True))
        p_loc = jnp.exp2(s_loc - m)
        p_ctx = jnp.exp2(s_ctx - m)
        l = jnp.sum(p_loc, axis=-1, keepdims=True) + jnp.sum(p_ctx, axis=-1, keepdims=True)
        o = (_dot(p_loc.astype(BF16), v_ref[ks * GRID_W:ks * GRID_W + nk, :])
             + _dot(p_ctx.astype(BF16), vc_ref[...]))
        o_ref[blk * nq:(blk + 1) * nq, :] = (o / l).astype(BF16)


def _natten(qkv, bias_tab, *, nbatch, seq, ctx_len, d):
    rows = seq // GRID_W
    nheads = d // HEAD_DIM
    assert rows % NA_QROWS == 0 and rows >= NA_KROWS
    ctx_blk0 = nbatch * seq // ctx_len
    return pl.pallas_call(
        functools.partial(_natten_kernel, rows=rows),
        grid=(nheads, nbatch),
        in_specs=[
            pl.BlockSpec((seq, HEAD_DIM), lambda h, b: (b, h)),
            pl.BlockSpec((seq, HEAD_DIM), lambda h, b: (b, nheads + h)),
            pl.BlockSpec((seq, HEAD_DIM), lambda h, b: (b, 2 * nheads + h)),
            pl.BlockSpec((ctx_len, HEAD_DIM), lambda h, b: (ctx_blk0 + b, nheads + h)),
            pl.BlockSpec((ctx_len, HEAD_DIM), lambda h, b: (ctx_blk0 + b, 2 * nheads + h)),
            pl.BlockSpec((1,) + bias_tab.shape[1:], lambda h, b: (h, 0, 0, 0)),
        ],
        out_specs=pl.BlockSpec((seq, HEAD_DIM), lambda h, b: (b, h)),
        out_shape=jax.ShapeDtypeStruct((nbatch * seq, d), BF16),
        compiler_params=_params("arbitrary", "arbitrary"),
        name="natten",
    )(qkv, qkv, qkv, qkv, qkv, bias_tab)


def kernel(x, c, ctx, c_ctx, ada_w, ada_b, post_ln_g, post_ln_b, ab_w_in, ab_w_out, ab_q_gain, ab_k_gain,
           ab_dw_w, ab_dw_b, ab_norm_g, ab_norm_b, c_w_in, c_w_out, c_rel_bias, ffn_w_up, ffn_dw_w, ffn_dw_b,
           ffn_w_down):
    nbatch, seq, d = x.shape
    ctx_len = ctx.shape[1]
    depth = ada_w.shape[0]
    assert depth == 2 and nbatch + 1 <= COND_ROWS and nbatch * ctx_len <= seq
    assert seq % GRID_W == 0 and seq & (seq - 1) == 0 and ctx_len & (ctx_len - 1) == 0 and ctx_len <= seq
    nx, nc = nbatch * seq, nbatch * ctx_len
    alpha = (2 * depth) ** 0.25
    tm = _pick((seq, nc), (512, 256))
    n_lat_tiles = nx // tm
    geo = dict(tm=tm, n_lat_tiles=n_lat_tiles, seq=seq, nbatch=nbatch)

    cond = jnp.concatenate([c, c_ctx[None], jnp.zeros((COND_ROWS - nbatch - 1, d), F32)], axis=0)
    mod3 = _ada(cond, ada_w, ada_b).reshape(depth * COND_ROWS * 6, 1, d)
    x2d, c2d = x.reshape(nx, d), ctx.reshape(nc, d)
    ffn_w = (ffn_w_up.astype(BF16), ffn_dw_w, ffn_dw_b, jnp.swapaxes(ffn_w_down, 1, 2).astype(BF16))

    mod_base = 0
    qkv, glu = _inproj_ab(x2d, c2d, mod3, mod_base, ab_w_in[0].astype(BF16), ab_q_gain[0][None],
                          ab_k_gain[0][None], _rope_tables(seq, tm), **geo)
    attn = _gqa(qkv, nbatch=nbatch, seq=seq, ctx_len=ctx_len)
    conv = _conformer_conv(glu, ab_dw_w[0], ab_dw_b[0], ab_norm_g[0], ab_norm_b[0],
                           nbatch=nbatch, seq=seq, ctx_len=ctx_len)
    x1, h2 = _outproj([attn, conv], ab_w_out[0].astype(BF16), [x2d, c2d], mod3, mod_base, post_ln_g[0, 0],
                      post_ln_b[0, 0], n_out=nx + nc, alpha=alpha, **geo)
    xc = _ffn(h2, x1, 0, *ffn_w, mod3, mod_base, post_ln_g[0, 1], post_ln_b[0, 1], ctx_len=ctx_len, alpha=alpha,
              **geo)

    mod_base = COND_ROWS * 6
    qkv = _inproj_c(xc, mod3, mod_base, c_w_in[0].astype(BF16), **geo)
    attn = _natten(qkv, _na_bias_table(c_rel_bias[0]), nbatch=nbatch, seq=seq, ctx_len=ctx_len, d=d)
    x1, h2 = _outproj([attn], c_w_out[0].astype(BF16), [xc], mod3, mod_base, post_ln_g[1, 0], post_ln_b[1, 0],
                      n_out=nx, alpha=alpha, **geo)
    out = _ffn(h2, x1, 1, *ffn_w, mod3, mod_base, post_ln_g[1, 1], post_ln_b[1, 1], ctx_len=ctx_len, alpha=alpha,
               **geo)
    return out.reshape(nbatch, seq, d)
```

```python
import functools

import jax
import jax.numpy as jnp
import numpy as np
from jax import lax
from jax.experimental import pallas as pl
from jax.experimental.pallas import tpu as pltpu

F32 = jnp.float32
BF16 = jnp.bfloat16

HEAD_DIM = 128
GRID_W = 64
A_HEADS = 8
A_KV_HEADS = 2
A_GROUP = A_HEADS // A_KV_HEADS
ROPE_THETA = 10000.0
NA_ROWS = 8
NA_COLS = 16
LN_EPS = 1e-6
NEG = -1e30
LOG2E = 1.4426950408889634
ATTN_Q_SCALE = HEAD_DIM ** -0.5 * LOG2E

V7X_LANES = 128
V7X_F32_SUBLANES = 8
V7X_BF16_SUBLANES = 16
V7X_VMEM_LIMIT = 56 * 1024 * 1024
COND_ROWS = 8

NA_QROWS = 4
NA_KROWS = NA_QROWS + NA_ROWS


def _pick(n, candidates):
    for t in candidates:
        if all(v % t == 0 for v in n):
            return t
    raise ValueError(f"no tile in {candidates} divides {n}")


def _params(*sem, flags=None):
    return pltpu.CompilerParams(dimension_semantics=sem, vmem_limit_bytes=V7X_VMEM_LIMIT, flags=flags)


def _layer_norm(r, g, b):
    mu = jnp.mean(r, axis=-1, keepdims=True)
    d = r - mu
    var = jnp.mean(d * d, axis=-1, keepdims=True)
    return d * lax.rsqrt(var + LN_EPS) * g + b


def _sigmoid(v):
    return 1.0 / (1.0 + jnp.exp(-v))


def _dot(a, b):
    return jnp.dot(a, b, preferred_element_type=F32)


def _dot_t(a, b):
    return lax.dot_general(a, b, (((1,), (1,)), ((), ())), preferred_element_type=F32)


def _ada_kernel(cond_ref, w_ref, b_ref, o_ref):
    cnd = cond_ref[...]
    s = (cnd * _sigmoid(cnd)).astype(BF16)
    o_ref[0] = _dot(s, w_ref[0].astype(BF16)) + b_ref[0]


def _ada(cond, ada_w, ada_b):
    depth, d, n6 = ada_w.shape
    tn = _pick((n6,), (1024, 512, 256, 128))
    return pl.pallas_call(
        _ada_kernel,
        grid=(depth, n6 // tn),
        in_specs=[
            pl.BlockSpec((COND_ROWS, d), lambda l, j: (0, 0)),
            pl.BlockSpec((1, d, tn), lambda l, j: (l, 0, j)),
            pl.BlockSpec((1, 1, tn), lambda l, j: (l, 0, j)),
        ],
        out_specs=pl.BlockSpec((1, COND_ROWS, tn), lambda l, j: (l, 0, j)),
        out_shape=jax.ShapeDtypeStruct((depth, COND_ROWS, n6), F32),
        compiler_params=_params("arbitrary", "arbitrary"),
        name="ada_mod",
    )(cond, ada_w, ada_b.reshape(depth, 1, n6))


def _inproj_ab_kernel(x_ref, c_ref, sh_ref, sc_ref, w_ref, qg_ref, kg_ref, rope_ref, qkv_ref, glu_ref, hb_ref, *,
                      q_scale, n_lat_tiles):
    is_lat = pl.program_id(0) < n_lat_tiles
    for src_ref, cond in ((x_ref, is_lat), (c_ref, jnp.logical_not(is_lat))):
        @pl.when(cond)
        def _():
            hb_ref[...] = (src_ref[...] * (1.0 + sc_ref[0]) + sh_ref[0]).astype(BF16)
    hb = hb_ref[...]
    cos, sin_lo, sin_hi = rope_ref[0], rope_ref[1], rope_ref[2]
    a_w = A_HEADS * HEAD_DIM
    kv_w = A_KV_HEADS * HEAD_DIM
    b_w = glu_ref.shape[1]

    def norm_rope(z, gain):
        zn = z * lax.rsqrt(jnp.mean(z * z, axis=-1, keepdims=True) + LN_EPS) * gain
        quarter = HEAD_DIM // 4
        return (zn * cos + pltpu.roll(zn, HEAD_DIM - quarter, 1) * sin_lo
                + pltpu.roll(zn, quarter, 1) * sin_hi)

    q_gain = qg_ref[...] * q_scale
    heads_per_dot = 4
    for c0 in range(0, a_w, heads_per_dot * HEAD_DIM):
        z = _dot(hb, w_ref[:, c0:c0 + heads_per_dot * HEAD_DIM])
        for hh in range(heads_per_dot):
            cs = slice(c0 + hh * HEAD_DIM, c0 + (hh + 1) * HEAD_DIM)
            qkv_ref[:, cs] = norm_rope(z[:, hh * HEAD_DIM:(hh + 1) * HEAD_DIM], q_gain).astype(BF16)
    z = _dot(hb, w_ref[:, a_w:a_w + 2 * kv_w])
    for hh in range(A_KV_HEADS):
        cs = slice(a_w + hh * HEAD_DIM, a_w + (hh + 1) * HEAD_DIM)
        qkv_ref[:, cs] = norm_rope(z[:, hh * HEAD_DIM:(hh + 1) * HEAD_DIM], kg_ref[...]).astype(BF16)
    qkv_ref[:, a_w + kv_w:a_w + 2 * kv_w] = z[:, kv_w:].astype(BF16)
    u0 = a_w + 2 * kv_w
    glu_cols = 512 if b_w % 512 == 0 else b_w
    for c0 in range(0, b_w, glu_cols):
        u = _dot(hb, w_ref[:, u0 + c0:u0 + c0 + glu_cols])
        g = _dot(hb, w_ref[:, u0 + b_w + c0:u0 + b_w + c0 + glu_cols])
        glu_ref[:, c0:c0 + glu_cols] = u * _sigmoid(g)


def _inproj_ab(x2d, c2d, mod3, mod_base, w_in, q_gain, k_gain, rope, *, tm, n_lat_tiles, seq, nbatch):
    d = x2d.shape[1]
    n = x2d.shape[0] + c2d.shape[0]
    n_in = w_in.shape[1]
    a_w, kv_w = A_HEADS * HEAD_DIM, A_KV_HEADS * HEAD_DIM
    b_w = (n_in - a_w - 2 * kv_w) // 2
    tiles_per_seq = seq // tm

    def cond_row(i):
        return jnp.where(i < n_lat_tiles, (i * tm) // seq, nbatch)

    def rope_blk(i):
        return jnp.where(i < n_lat_tiles, i % tiles_per_seq, tiles_per_seq)

    return pl.pallas_call(
        functools.partial(_inproj_ab_kernel, q_scale=ATTN_Q_SCALE, n_lat_tiles=n_lat_tiles),
        grid=(n // tm,),
        in_specs=[
            pl.BlockSpec((tm, d), lambda i: (jnp.minimum(i, n_lat_tiles - 1), 0)),
            pl.BlockSpec((tm, d), lambda i: (jnp.maximum(i - n_lat_tiles, 0), 0)),
            pl.BlockSpec((1, 1, d), lambda i: (mod_base + cond_row(i) * 6 + 0, 0, 0)),
            pl.BlockSpec((1, 1, d), lambda i: (mod_base + cond_row(i) * 6 + 1, 0, 0)),
            pl.BlockSpec((d, n_in), lambda i: (0, 0), pipeline_mode=pl.Buffered(1)),
            pl.BlockSpec((1, HEAD_DIM), lambda i: (0, 0)),
            pl.BlockSpec((1, HEAD_DIM), lambda i: (0, 0)),
            pl.BlockSpec((3, tm, HEAD_DIM), lambda i: (0, rope_blk(i), 0)),
        ],
        out_specs=[
            pl.BlockSpec((tm, a_w + 2 * kv_w), lambda i: (i, 0)),
            pl.BlockSpec((tm, b_w), lambda i: (i, 0)),
        ],
        out_shape=[
            jax.ShapeDtypeStruct((n, a_w + 2 * kv_w), BF16),
            jax.ShapeDtypeStruct((n, b_w), F32),
        ],
        scratch_shapes=[pltpu.VMEM((tm, d), BF16)],
        compiler_params=_params("arbitrary"),
        name="inproj_ab",
    )(x2d, c2d, mod3, mod3, w_in, q_gain, k_gain, rope)


def _rope_tables(seq, tm):
    t = jnp.arange(seq)
    nfreq = HEAD_DIM // 4
    inv = ROPE_THETA ** (-jnp.arange(nfreq, dtype=F32) / nfreq)
    ang_r = (t // GRID_W).astype(F32)[:, None] * inv
    ang_c = (t % GRID_W).astype(F32)[:, None] * inv
    ang = jnp.concatenate([ang_r, ang_r, ang_c, ang_c], axis=-1)
    cos, sin = jnp.cos(ang), jnp.sin(ang)
    low = (np.arange(HEAD_DIM) // nfreq) % 2 == 0
    sin_lo = jnp.where(low, -sin, 0.0)
    sin_hi = jnp.where(low, 0.0, sin)
    tab = jnp.stack([cos, sin_lo, sin_hi])
    ident = jnp.stack([jnp.ones((tm, HEAD_DIM), F32), jnp.zeros((tm, HEAD_DIM), F32), jnp.zeros((tm, HEAD_DIM), F32)])
    return jnp.concatenate([tab, ident], axis=1)


def _softmax_pv(q, parts):
    ss = [_dot_t(q, k_ref[...]) for k_ref, _ in parts]
    m = functools.reduce(jnp.maximum, [jnp.max(s, axis=-1, keepdims=True) for s in ss])
    ps = [jnp.exp2(s - m) for s in ss]
    l = functools.reduce(jnp.add, [jnp.sum(p, axis=-1, keepdims=True) for p in ps])
    o = functools.reduce(jnp.add, [_dot(p.astype(BF16), v_ref[...]) for p, (_, v_ref) in zip(ps, parts)])
    return o / l


def _gqa_kernel(q_ref, kx_ref, vx_ref, kc_ref, vc_ref, o_ref, *, n_lat_tiles):
    qi = pl.program_id(1)

    def run(parts):
        for g in range(A_GROUP):
            cs = slice(g * HEAD_DIM, (g + 1) * HEAD_DIM)
            o_ref[:, cs] = _softmax_pv(q_ref[:, cs], parts).astype(BF16)

    @pl.when(qi < n_lat_tiles)
    def _():
        run([(kx_ref, vx_ref), (kc_ref, vc_ref)])

    @pl.when(qi >= n_lat_tiles)
    def _():
        run([(kc_ref, vc_ref)])


def _gqa(qkv, *, nbatch, seq, ctx_len):
    n = qkv.shape[0]
    tq = _pick((seq, ctx_len), (256, 128))
    n_lat_tiles = nbatch * seq // tq
    lat_per_seq, ctx_per_seq = seq // tq, ctx_len // tq
    a_w = A_HEADS * HEAD_DIM
    gw = A_GROUP * HEAD_DIM
    k_col0 = a_w // HEAD_DIM
    v_col0 = k_col0 + A_KV_HEADS

    def batch(qi):
        return jnp.where(qi < n_lat_tiles, qi // lat_per_seq, (qi - n_lat_tiles) // ctx_per_seq)

    def lat_batch(qi):
        return jnp.minimum(qi // lat_per_seq, nbatch - 1)

    ctx_blk0 = nbatch * seq // ctx_len
    return pl.pallas_call(
        functools.partial(_gqa_kernel, n_lat_tiles=n_lat_tiles),
        grid=(A_KV_HEADS, n // tq),
        in_specs=[
            pl.BlockSpec((tq, gw), lambda h, qi: (qi, h)),
            pl.BlockSpec((seq, HEAD_DIM), lambda h, qi: (lat_batch(qi), k_col0 + h)),
            pl.BlockSpec((seq, HEAD_DIM), lambda h, qi: (lat_batch(qi), v_col0 + h)),
            pl.BlockSpec((ctx_len, HEAD_DIM), lambda h, qi: (ctx_blk0 + batch(qi), k_col0 + h)),
            pl.BlockSpec((ctx_len, HEAD_DIM), lambda h, qi: (ctx_blk0 + batch(qi), v_col0 + h)),
        ],
        out_specs=pl.BlockSpec((tq, gw), lambda h, qi: (qi, h)),
        out_shape=jax.ShapeDtypeStruct((n, a_w), BF16),
        compiler_params=_params("arbitrary", "arbitrary"),
        name="gqa_attn",
    )(qkv, qkv, qkv, qkv, qkv)


def _conv_kernel(x_ref, xp_ref, xn_ref, dw_ref, db_ref, g_ref, b_ref, o_ref, buf_ref, y_ref, shift_ref, *,
                 tt, halo, n_lat_tiles, lat_per_seq, ctx_per_seq):
    i = pl.program_id(0)
    taps = dw_ref.shape[0]
    pad = taps // 2
    idx = jnp.where(i < n_lat_tiles, i % lat_per_seq, (i - n_lat_tiles) % ctx_per_seq)
    per_seq = jnp.where(i < n_lat_tiles, lat_per_seq, ctx_per_seq)
    buf_ref[0:halo, :] = jnp.where(idx > 0, xp_ref[...], 0.0)
    buf_ref[halo:halo + tt, :] = x_ref[...]
    buf_ref[halo + tt:, :] = jnp.where(idx < per_seq - 1, xn_ref[...], 0.0)
    nchunk = x_ref.shape[1] // V7X_LANES

    rows = tt + 2 * halo

    def chunk(c, carry):
        cs = pl.ds(pl.multiple_of(c * V7X_LANES, V7X_LANES), V7X_LANES)
        xb = buf_ref[:, cs]
        for s in range(1, V7X_F32_SUBLANES):
            shift_ref[s] = pltpu.roll(xb, rows - s, 0)
        acc = jnp.broadcast_to(db_ref[:, cs], (tt, V7X_LANES))
        for k in range(taps):
            off = halo - pad + k
            base, s = off - off % V7X_F32_SUBLANES, off % V7X_F32_SUBLANES
            win = buf_ref[pl.ds(base, tt), cs] if s == 0 else shift_ref[s, base:base + tt, :]
            acc = acc + win * dw_ref[pl.ds(k, 1), cs]
        y_ref[:, cs] = acc
        return carry

    lax.fori_loop(0, nchunk, chunk, 0)
    yn = _layer_norm(y_ref[...], g_ref[...], b_ref[...])
    o_ref[...] = (yn * _sigmoid(yn)).astype(BF16)


def _conformer_conv(glu, dw_w, dw_b, n_g, n_b, *, nbatch, seq, ctx_len):
    n, cw = glu.shape
    tt = _pick((seq, ctx_len), (256, 128))
    halo = 16
    assert dw_w.shape[0] // 2 <= halo
    n_lat_tiles = nbatch * seq // tt
    hb = tt // halo
    nhalo = n // halo
    return pl.pallas_call(
        functools.partial(_conv_kernel, tt=tt, halo=halo, n_lat_tiles=n_lat_tiles,
                          lat_per_seq=seq // tt, ctx_per_seq=ctx_len // tt),
        grid=(n // tt,),
        in_specs=[
            pl.BlockSpec((tt, cw), lambda i: (i, 0)),
            pl.BlockSpec((halo, cw), lambda i: (jnp.maximum(i * hb - 1, 0), 0)),
            pl.BlockSpec((halo, cw), lambda i: (jnp.minimum((i + 1) * hb, nhalo - 1), 0)),
            pl.BlockSpec(dw_w.shape, lambda i: (0, 0)),
            pl.BlockSpec((1, cw), lambda i: (0, 0)),
            pl.BlockSpec((1, cw), lambda i: (0, 0)),
            pl.BlockSpec((1, cw), lambda i: (0, 0)),
        ],
        out_specs=pl.BlockSpec((tt, cw), lambda i: (i, 0)),
        out_shape=jax.ShapeDtypeStruct((n, cw), BF16),
        scratch_shapes=[pltpu.VMEM((tt + 2 * halo, cw), F32), pltpu.VMEM((tt, cw), F32),
                        pltpu.VMEM((V7X_F32_SUBLANES, tt + 2 * halo, V7X_LANES), F32)],
        compiler_params=_params("arbitrary"),
        name="conformer_conv",
    )(glu, glu, glu, dw_w, dw_b.reshape(1, cw), n_g.reshape(1, cw), n_b.reshape(1, cw))


def _outproj_kernel(*refs, n_parts, n_res, n_lat_tiles, alpha):
    part_refs = refs[:n_parts]
    res_refs = refs[n_parts + 1:n_parts + 1 + n_res]
    w_ref = refs[n_parts]
    gate_ref, sh_ref, sc_ref, lg_ref, lb_ref, x1_ref, h2_ref = refs[n_parts + 1 + n_res:]
    y = None
    k0 = 0
    for p_ref in part_refs:
        kw = p_ref.shape[1]
        t = _dot(p_ref[...], w_ref[k0:k0 + kw, :])
        y = t if y is None else y + t
        k0 += kw

    def finish(x_ref):
        x1 = _layer_norm(alpha * x_ref[...] + gate_ref[0] * y, lg_ref[...], lb_ref[...])
        x1_ref[...] = x1
        h2_ref[...] = (x1 * (1.0 + sc_ref[0]) + sh_ref[0]).astype(BF16)

    if n_res == 1:
        finish(res_refs[0])
    else:
        is_lat = pl.program_id(0) < n_lat_tiles
        pl.when(is_lat)(lambda: finish(res_refs[0]))
        pl.when(jnp.logical_not(is_lat))(lambda: finish(res_refs[1]))


def _outproj(parts, w_out, residuals, mod3, mod_base, ln_g, ln_b, *, n_out, tm, n_lat_tiles, seq, nbatch, alpha):
    d = residuals[0].shape[1]
    if len(residuals) == 1:
        res_specs = [pl.BlockSpec((tm, d), lambda i: (i, 0))]
    else:
        res_specs = [pl.BlockSpec((tm, d), lambda i: (jnp.minimum(i, n_lat_tiles - 1), 0)),
                     pl.BlockSpec((tm, d), lambda i: (jnp.maximum(i - n_lat_tiles, 0), 0))]

    def cond_row(i):
        return jnp.where(i < n_lat_tiles, (i * tm) // seq, nbatch)

    def mod_spec(k):
        return pl.BlockSpec((1, 1, d), lambda i: (mod_base + cond_row(i) * 6 + k, 0, 0))

    return pl.pallas_call(
        functools.partial(_outproj_kernel, n_parts=len(parts), n_res=len(residuals), n_lat_tiles=n_lat_tiles,
                          alpha=alpha),
        grid=(n_out // tm,),
        in_specs=[pl.BlockSpec((tm, p.shape[1]), lambda i: (i, 0)) for p in parts] + [
            pl.BlockSpec(w_out.shape, lambda i: (0, 0), pipeline_mode=pl.Buffered(1))] + res_specs + [
            mod_spec(2), mod_spec(3), mod_spec(4),
            pl.BlockSpec((1, d), lambda i: (0, 0)),
            pl.BlockSpec((1, d), lambda i: (0, 0)),
        ],
        out_specs=[pl.BlockSpec((tm, d), lambda i: (i, 0)), pl.BlockSpec((tm, d), lambda i: (i, 0))],
        out_shape=[jax.ShapeDtypeStruct((n_out, d), F32), jax.ShapeDtypeStruct((n_out, d), BF16)],
        compiler_params=_params("arbitrary"),
        name="outproj_ln",
    )(*parts, w_out, *residuals, mod3, mod3, mod3, ln_g.reshape(1, d), ln_b.reshape(1, d))


def _ffn_kernel(h_ref, hp_ref, hn_ref, wa_ref, wg_ref, dwa_ref, dwg_ref, dba_ref, dbg_ref, wd_ref,
                x_ref, gate_ref, lg_ref, lb_ref, o_ref,
                hext_ref, za0_ref, za1_ref, zg0_ref, zg1_ref, acc_ref, *,
                tm, halo, sub, nf, seq, ctx_len, n_lat_tiles, alpha):
    za_refs = (za0_ref, za1_ref)
    zg_refs = (zg0_ref, zg1_ref)
    i = pl.program_id(0)
    c = pl.program_id(1)
    is_lat = i < n_lat_tiles
    row = lax.broadcasted_iota(jnp.int32, (tm, 1), 0) + i * tm
    pos = row & (ctx_len - 1)
    has_prev = pos != 0
    has_next = pos != ctx_len - 1

    chunks = [slice(s0, s0 + sub) for s0 in range(0, wa_ref.shape[1], sub)]

    def up_proj(slot):
        za_refs[slot][...] = _dot(hext_ref[...], wa_ref[...])
        zg_refs[slot][...] = _dot(hext_ref[...], wg_ref[...])

    def conv(z_ref, dw_ref, db_ref, cs, masked):
        z_prev = z_ref[halo - 1:halo - 1 + tm, cs]
        z_next = z_ref[halo + 1:halo + 1 + tm, cs]
        if masked:
            z_prev = jnp.where(has_prev, z_prev, 0.0)
            z_next = jnp.where(has_next, z_next, 0.0)
        return (z_prev * dw_ref[0:1, cs] + z_ref[halo:halo + tm, cs] * dw_ref[1:2, cs] + z_next * dw_ref[2:3, cs]
                + db_ref[:, cs])

    def step(up_slot, down_slot, masked=False):
        if down_slot is not None:
            contrib = None
            for cs in chunks:
                a = conv(za_refs[down_slot], dwa_ref, dba_ref, cs, masked)
                g = conv(zg_refs[down_slot], dwg_ref, dbg_ref, cs, masked)
                t = _dot_t(wd_ref[:, cs], (g * _sigmoid(g) * a).astype(BF16))
                contrib = t if contrib is None else contrib + t
            acc_ref[...] += contrib
        if up_slot is not None:
            up_proj(up_slot)

    @pl.when(c == 0)
    def _():
        starts_seq = is_lat & ((i * tm) % seq == 0)
        ends_seq = is_lat & (((i + 1) * tm) % seq == 0)
        zero_halo = jnp.zeros(hp_ref.shape, hp_ref.dtype)
        hext_ref[0:halo, :] = jnp.where(starts_seq, zero_halo, hp_ref[...])
        hext_ref[halo:halo + tm, :] = h_ref[...]
        hext_ref[halo + tm:, :] = jnp.where(ends_seq, zero_halo, hn_ref[...])
        acc_ref[...] = jnp.zeros_like(acc_ref)
        step(0, None)

    for masked in (False, True):
        for slot in (0, 1):
            @pl.when((c >= 1) & (c < nf) & (c % 2 == slot) & (is_lat != masked))
            def _():
                step(slot, 1 - slot, masked)

        @pl.when((c == nf) & (is_lat != masked))
        def _():
            step(None, (nf - 1) % 2, masked)

    @pl.when(c == nf)
    def _():
        o_ref[...] = _layer_norm(alpha * x_ref[...] + gate_ref[0] * acc_ref[...].T, lg_ref[...], lb_ref[...])


def _ffn(h2, x1, layer, w_up, dw_w, dw_b, w_down, mod3, mod_base, ln_g, ln_b, *, tm, n_lat_tiles, seq, ctx_len,
         nbatch, alpha):
    n_out, d = x1.shape
    depth, _, d_ff = w_down.shape
    tf = _pick((d_ff,), (512, 256, 128))
    nf = d_ff // tf
    halo = V7X_BF16_SUBLANES
    hb = tm // halo
    nhalo = h2.shape[0] // halo
    ntiles = n_out // tm
    assert dw_w.shape[1] == 3

    def cond_row(i):
        return jnp.where(i < n_lat_tiles, (i * tm) // seq, nbatch)

    dw_b3 = dw_b.reshape(depth, 1, 2 * d_ff)

    def up(c):
        return jnp.minimum(c, nf - 1)

    def down(c):
        return jnp.maximum(c - 1, 0)

    mid = nf // 2

    def h_tile(i, c):
        return jnp.where(c < mid, i, jnp.minimum(i + 1, ntiles - 1))

    def x_tile(i, c):
        return jnp.where(c < mid, jnp.maximum(i - 1, 0), i)

    return pl.pallas_call(
        functools.partial(_ffn_kernel, tm=tm, halo=halo, sub=min(tf, 256), nf=nf, seq=seq, ctx_len=ctx_len,
                          n_lat_tiles=n_lat_tiles, alpha=alpha),
        grid=(ntiles, nf + 1),
        in_specs=[
            pl.BlockSpec((tm, d), lambda i, c: (h_tile(i, c), 0)),
            pl.BlockSpec((halo, d), lambda i, c: (jnp.maximum(i * hb - 1, 0), 0)),
            pl.BlockSpec((halo, d), lambda i, c: (jnp.minimum((i + 1) * hb, nhalo - 1), 0)),
            pl.BlockSpec((None, d, tf), lambda i, c: (layer, 0, up(c))),
            pl.BlockSpec((None, d, tf), lambda i, c: (layer, 0, nf + up(c))),
            pl.BlockSpec((None, 3, tf), lambda i, c: (layer, 0, down(c))),
            pl.BlockSpec((None, 3, tf), lambda i, c: (layer, 0, nf + down(c))),
            pl.BlockSpec((None, 1, tf), lambda i, c: (layer, 0, down(c))),
            pl.BlockSpec((None, 1, tf), lambda i, c: (layer, 0, nf + down(c))),
            pl.BlockSpec((None, d, tf), lambda i, c: (layer, 0, down(c))),
            pl.BlockSpec((tm, d), lambda i, c: (x_tile(i, c), 0)),
            pl.BlockSpec((1, 1, d), lambda i, c: (mod_base + cond_row(i) * 6 + 5, 0, 0)),
            pl.BlockSpec((1, d), lambda i, c: (0, 0)),
            pl.BlockSpec((1, d), lambda i, c: (0, 0)),
        ],
        out_specs=pl.BlockSpec((tm, d), lambda i, c: (i, 0)),
        out_shape=jax.ShapeDtypeStruct((n_out, d), F32),
        scratch_shapes=([pltpu.VMEM((tm + 2 * halo, d), BF16)] + [pltpu.VMEM((tm + 2 * halo, tf), F32)] * 4
                        + [pltpu.VMEM((d, tm), F32)]),
        compiler_params=_params("arbitrary", "arbitrary"),
        name="conv_ffn",
    )(h2, h2, h2, w_up, w_up, dw_w, dw_w, dw_b3, dw_b3, w_down, x1, mod3, ln_g.reshape(1, d), ln_b.reshape(1, d))


def _inproj_c_kernel(x_ref, sh_ref, sc_ref, w_ref, o_ref, *, q_scale):
    hb = (x_ref[...] * (1.0 + sc_ref[0]) + sh_ref[0]).astype(BF16)
    d = x_ref.shape[1]
    cols = 512
    for c0 in range(0, o_ref.shape[1], cols):
        z = _dot(hb, w_ref[:, c0:c0 + cols])
        if c0 < d:
            z = z * q_scale
        o_ref[:, c0:c0 + cols] = z.astype(BF16)


def _inproj_c(xc, mod3, mod_base, w_in, *, tm, n_lat_tiles, seq, nbatch):
    n, d = xc.shape
    n_in = w_in.shape[1]
    assert d % 512 == 0 and n_in == 3 * d

    def cond_row(i):
        return jnp.where(i < n_lat_tiles, (i * tm) // seq, nbatch)

    return pl.pallas_call(
        functools.partial(_inproj_c_kernel, q_scale=ATTN_Q_SCALE),
        grid=(n // tm,),
        in_specs=[
            pl.BlockSpec((tm, d), lambda i: (i, 0)),
            pl.BlockSpec((1, 1, d), lambda i: (mod_base + cond_row(i) * 6 + 0, 0, 0)),
            pl.BlockSpec((1, 1, d), lambda i: (mod_base + cond_row(i) * 6 + 1, 0, 0)),
            pl.BlockSpec((d, n_in), lambda i: (0, 0), pipeline_mode=pl.Buffered(1)),
        ],
        out_specs=pl.BlockSpec((tm, n_in), lambda i: (i, 0)),
        out_shape=jax.ShapeDtypeStruct((n, n_in), BF16),
        compiler_params=_params("arbitrary"),
        name="inproj_c",
    )(xc, mod3, mod3, w_in)


def _na_bias_table(rel_bias):
    nheads, nrow, ncol = rel_bias.shape
    qc = np.arange(GRID_W)[:, None]
    kc = np.arange(GRID_W)[None, :]
    cstart = np.clip(qc - NA_COLS // 2, 0, GRID_W - NA_COLS)
    valid = (kc >= cstart) & (kc < cstart + NA_COLS)
    period = 2 * GRID_W - 1
    lead = GRID_W - NA_COLS - 1
    padded = jnp.pad(rel_bias, ((0, 0), (0, 0), (lead, period - lead - ncol)), constant_values=NEG)
    tiled = jnp.broadcast_to(padded[:, :, None, :], (nheads, nrow, GRID_W, period))
    skew = tiled.reshape(nheads, nrow, GRID_W * period)[:, :, :GRID_W * (period - 1)]
    toep = skew.reshape(nheads, nrow, GRID_W, period - 1)[:, :, :, GRID_W - 2:]
    toep = jnp.where(valid[None, None], toep * LOG2E, NEG)
    neg = jnp.full((nheads, 1, GRID_W, GRID_W), NEG, F32)
    ext = jnp.concatenate([neg, toep, neg], axis=1)
    return jnp.concatenate([ext[:, :2 * NA_ROWS], ext[:, 1:]], axis=-1)


def _natten_kernel(q_ref, k_ref, v_ref, kc_ref, vc_ref, bias_ref, o_ref, *, rows):
    nq = NA_QROWS * GRID_W
    nk = NA_KROWS * GRID_W
    left = lax.broadcasted_iota(jnp.int32, (GRID_W, 2 * GRID_W), 1) < GRID_W
    for blk in range(rows // NA_QROWS):
        ks = min(max(NA_QROWS * blk - NA_ROWS // 2, 0), rows - NA_KROWS)
        q = q_ref[blk * nq:(blk + 1) * nq, :]
        s_loc = _dot_t(q, k_ref[ks * GRID_W:ks * GRID_W + nk, :])
        bias_rows = []
        for qi in range(NA_QROWS):
            qr = NA_QROWS * blk + qi
            r0 = min(max(qr - NA_ROWS // 2, 0), rows - NA_ROWS)
            slabs = []
            for j in range(NA_KROWS // 2):
                kr = ks + 2 * j
                ok0 = r0 <= kr < r0 + NA_ROWS
                ok1 = r0 <= kr + 1 < r0 + NA_ROWS
                if not (ok0 or ok1):
                    slabs.append(jnp.full((GRID_W, 2 * GRID_W), NEG, F32))
                    continue
                slab = bias_ref[0, kr - qr + NA_ROWS]
                if not ok1:
                    slab = jnp.where(left, slab, NEG)
                elif not ok0:
                    slab = jnp.where(left, NEG, slab)
                slabs.append(slab)
            bias_rows.append(jnp.concatenate(slabs, axis=1))
        s_loc = s_loc + jnp.concatenate(bias_rows, axis=0)
        s_ctx = _dot_t(q, kc_ref[...])
        m = jnp.maximum(jnp.max(s_loc, axis=-1, keepdims=True), jnp.max(s_ctx, axis=-1, keepdims=True))
        p_loc = jnp.exp2(s_loc - m)
        p_ctx = jnp.exp2(s_ctx - m)
        l = jnp.sum(p_loc, axis=-1, keepdims=True) + jnp.sum(p_ctx, axis=-1, keepdims=True)
        o = (_dot(p_loc.astype(BF16), v_ref[ks * GRID_W:ks * GRID_W + nk, :])
             + _dot(p_ctx.astype(BF16), vc_ref[...]))
        o_ref[blk * nq:(blk + 1) * nq, :] = (o / l).astype(BF16)


def _natten(qkv, bias_tab, *, nbatch, seq, ctx_len, d):
    rows = seq // GRID_W
    nheads = d // HEAD_DIM
    assert rows % NA_QROWS == 0 and rows >= NA_KROWS
    ctx_blk0 = nbatch * seq // ctx_len
    return pl.pallas_call(
        functools.partial(_natten_kernel, rows=rows),
        grid=(nheads, nbatch),
        in_specs=[
            pl.BlockSpec((seq, HEAD_DIM), lambda h, b: (b, h)),
            pl.BlockSpec((seq, HEAD_DIM), lambda h, b: (b, nheads + h)),
            pl.BlockSpec((seq, HEAD_DIM), lambda h, b: (b, 2 * nheads + h)),
            pl.BlockSpec((ctx_len, HEAD_DIM), lambda h, b: (ctx_blk0 + b, nheads + h)),
            pl.BlockSpec((ctx_len, HEAD_DIM), lambda h, b: (ctx_blk0 + b, 2 * nheads + h)),
            pl.BlockSpec((1,) + bias_tab.shape[1:], lambda h, b: (h, 0, 0, 0)),
        ],
        out_specs=pl.BlockSpec((seq, HEAD_DIM), lambda h, b: (b, h)),
        out_shape=jax.ShapeDtypeStruct((nbatch * seq, d), BF16),
        compiler_params=_params("arbitrary", "arbitrary"),
        name="natten",
    )(qkv, qkv, qkv, qkv, qkv, bias_tab)


def kernel(x, c, ctx, c_ctx, ada_w, ada_b, post_ln_g, post_ln_b, ab_w_in, ab_w_out, ab_q_gain, ab_k_gain,
           ab_dw_w, ab_dw_b, ab_norm_g, ab_norm_b, c_w_in, c_w_out, c_rel_bias, ffn_w_up, ffn_dw_w, ffn_dw_b,
           ffn_w_down):
    nbatch, seq, d = x.shape
    ctx_len = ctx.shape[1]
    depth = ada_w.shape[0]
    assert depth == 2 and nbatch + 1 <= COND_ROWS and nbatch * ctx_len <= seq
    assert seq % GRID_W == 0 and seq & (seq - 1) == 0 and ctx_len & (ctx_len - 1) == 0 and ctx_len <= seq
    nx, nc = nbatch * seq, nbatch * ctx_len
    alpha = (2 * depth) ** 0.25
    tm = _pick((seq, nc), (512, 256))
    n_lat_tiles = nx // tm
    geo = dict(tm=tm, n_lat_tiles=n_lat_tiles, seq=seq, nbatch=nbatch)

    cond = jnp.concatenate([c, c_ctx[None], jnp.zeros((COND_ROWS - nbatch - 1, d), F32)], axis=0)
    mod3 = _ada(cond, ada_w, ada_b).reshape(depth * COND_ROWS * 6, 1, d)
    x2d, c2d = x.reshape(nx, d), ctx.reshape(nc, d)
    ffn_w = (ffn_w_up.astype(BF16), ffn_dw_w, ffn_dw_b, jnp.swapaxes(ffn_w_down, 1, 2).astype(BF16))

    mod_base = 0
    qkv, glu = _inproj_ab(x2d, c2d, mod3, mod_base, ab_w_in[0].astype(BF16), ab_q_gain[0][None],
                          ab_k_gain[0][None], _rope_tables(seq, tm), **geo)
    attn = _gqa(qkv, nbatch=nbatch, seq=seq, ctx_len=ctx_len)
    conv = _conformer_conv(glu, ab_dw_w[0], ab_dw_b[0], ab_norm_g[0], ab_norm_b[0],
                           nbatch=nbatch, seq=seq, ctx_len=ctx_len)
    x1, h2 = _outproj([attn, conv], ab_w_out[0].astype(BF16), [x2d, c2d], mod3, mod_base, post_ln_g[0, 0],
                      post_ln_b[0, 0], n_out=nx + nc, alpha=alpha, **geo)
    xc = _ffn(h2, x1, 0, *ffn_w, mod3, mod_base, post_ln_g[0, 1], post_ln_b[0, 1], ctx_len=ctx_len, alpha=alpha,
              **geo)

    mod_base = COND_ROWS * 6
    qkv = _inproj_c(xc, mod3, mod_base, c_w_in[0].astype(BF16), **geo)
    attn = _natten(qkv, _na_bias_table(c_rel_bias[0]), nbatch=nbatch, seq=seq, ctx_len=ctx_len, d=d)
    x1, h2 = _outproj([attn], c_w_out[0].astype(BF16), [xc], mod3, mod_base, post_ln_g[1, 0], post_ln_b[1, 0],
                      n_out=nx, alpha=alpha, **geo)
    out = _ffn(h2, x1, 1, *ffn_w, mod3, mod_base, post_ln_g[1, 1], post_ln_b[1, 1], ctx_len=ctx_len, alpha=alpha,
               **geo)
    return out.reshape(nbatch, seq, d)
```

```python
import functools

import jax
import jax.numpy as jnp
import numpy as np
from jax import lax
from jax.experimental import pallas as pl
from jax.experimental.pallas import tpu as pltpu

F32 = jnp.float32
BF16 = jnp.bfloat16

HEAD_DIM = 128
GRID_W = 64
A_HEADS = 8
A_KV_HEADS = 2
A_GROUP = A_HEADS // A_KV_HEADS
ROPE_THETA = 10000.0
NA_ROWS = 8
NA_COLS = 16
LN_EPS = 1e-6
NEG = -1e30
LOG2E = 1.4426950408889634
ATTN_Q_SCALE = HEAD_DIM ** -0.5 * LOG2E

V7X_LANES = 128
V7X_F32_SUBLANES = 8
V7X_BF16_SUBLANES = 16
V7X_VMEM_LIMIT = 56 * 1024 * 1024
COND_ROWS = 8

NA_QROWS = 4
NA_KROWS = NA_QROWS + NA_ROWS


def _pick(n, candidates):
    for t in candidates:
        if all(v % t == 0 for v in n):
            return t
    raise ValueError(f"no tile in {candidates} divides {n}")


def _params(*sem, flags=None):
    return pltpu.CompilerParams(dimension_semantics=sem, vmem_limit_bytes=V7X_VMEM_LIMIT, flags=flags)


def _layer_norm(r, g, b):
    mu = jnp.mean(r, axis=-1, keepdims=True)
    d = r - mu
    var = jnp.mean(d * d, axis=-1, keepdims=True)
    return d * lax.rsqrt(var + LN_EPS) * g + b


def _sigmoid(v):
    return 1.0 / (1.0 + jnp.exp(-v))


def _dot(a, b):
    return jnp.dot(a, b, preferred_element_type=F32)


def _dot_t(a, b):
    return lax.dot_general(a, b, (((1,), (1,)), ((), ())), preferred_element_type=F32)


def _ada_kernel(cond_ref, w_ref, b_ref, o_ref):
    cnd = cond_ref[...]
    s = (cnd * _sigmoid(cnd)).astype(BF16)
    o_ref[0] = _dot(s, w_ref[0].astype(BF16)) + b_ref[0]


def _ada(cond, ada_w, ada_b):
    depth, d, n6 = ada_w.shape
    tn = _pick((n6,), (1024, 512, 256, 128))
    return pl.pallas_call(
        _ada_kernel,
        grid=(depth, n6 // tn),
        in_specs=[
            pl.BlockSpec((COND_ROWS, d), lambda l, j: (0, 0)),
            pl.BlockSpec((1, d, tn), lambda l, j: (l, 0, j)),
            pl.BlockSpec((1, 1, tn), lambda l, j: (l, 0, j)),
        ],
        out_specs=pl.BlockSpec((1, COND_ROWS, tn), lambda l, j: (l, 0, j)),
        out_shape=jax.ShapeDtypeStruct((depth, COND_ROWS, n6), F32),
        compiler_params=_params("arbitrary", "arbitrary"),
        name="ada_mod",
    )(cond, ada_w, ada_b.reshape(depth, 1, n6))


def _inproj_ab_kernel(x_ref, c_ref, sh_ref, sc_ref, w_ref, qg_ref, kg_ref, rope_ref, qkv_ref, glu_ref, hb_ref, *,
                      q_scale, n_lat_tiles):
    is_lat = pl.program_id(0) < n_lat_tiles
    for src_ref, cond in ((x_ref, is_lat), (c_ref, jnp.logical_not(is_lat))):
        @pl.when(cond)
        def _():
            hb_ref[...] = (src_ref[...] * (1.0 + sc_ref[0]) + sh_ref[0]).astype(BF16)
    hb = hb_ref[...]
    cos, sin_lo, sin_hi = rope_ref[0], rope_ref[1], rope_ref[2]
    a_w = A_HEADS * HEAD_DIM
    kv_w = A_KV_HEADS * HEAD_DIM
    b_w = glu_ref.shape[1]

    def norm_rope(z, gain):
        zn = z * lax.rsqrt(jnp.mean(z * z, axis=-1, keepdims=True) + LN_EPS) * gain
        quarter = HEAD_DIM // 4
        return (zn * cos + pltpu.roll(zn, HEAD_DIM - quarter, 1) * sin_lo
                + pltpu.roll(zn, quarter, 1) * sin_hi)

    q_gain = qg_ref[...] * q_scale
    heads_per_dot = 4
    for c0 in range(0, a_w, heads_per_dot * HEAD_DIM):
        z = _dot(hb, w_ref[:, c0:c0 + heads_per_dot * HEAD_DIM])
        for hh in range(heads_per_dot):
            cs = slice(c0 + hh * HEAD_DIM, c0 + (hh + 1) * HEAD_DIM)
            qkv_ref[:, cs] = norm_rope(z[:, hh * HEAD_DIM:(hh + 1) * HEAD_DIM], q_gain).astype(BF16)
    z = _dot(hb, w_ref[:, a_w:a_w + 2 * kv_w])
    for hh in range(A_KV_HEADS):
        cs = slice(a_w + hh * HEAD_DIM, a_w + (hh + 1) * HEAD_DIM)
        qkv_ref[:, cs] = norm_rope(z[:, hh * HEAD_DIM:(hh + 1) * HEAD_DIM], kg_ref[...]).astype(BF16)
    qkv_ref[:, a_w + kv_w:a_w + 2 * kv_w] = z[:, kv_w:].astype(BF16)
    u0 = a_w + 2 * kv_w
    glu_cols = 512 if b_w % 512 == 0 else b_w
    for c0 in range(0, b_w, glu_cols):
        u = _dot(hb, w_ref[:, u0 + c0:u0 + c0 + glu_cols])
        g = _dot(hb, w_ref[:, u0 + b_w + c0:u0 + b_w + c0 + glu_cols])
        glu_ref[:, c0:c0 + glu_cols] = u * _sigmoid(g)


def _inproj_ab(x2d, c2d, mod3, mod_base, w_in, q_gain, k_gain, rope, *, tm, n_lat_tiles, seq, nbatch):
    d = x2d.shape[1]
    n = x2d.shape[0] + c2d.shape[0]
    n_in = w_in.shape[1]
    a_w, kv_w = A_HEADS * HEAD_DIM, A_KV_HEADS * HEAD_DIM
    b_w = (n_in - a_w - 2 * kv_w) // 2
    tiles_per_seq = seq // tm

    def cond_row(i):
        return jnp.where(i < n_lat_tiles, (i * tm) // seq, nbatch)

    def rope_blk(i):
        return jnp.where(i < n_lat_tiles, i % tiles_per_seq, tiles_per_seq)

    return pl.pallas_call(
        functools.partial(_inproj_ab_kernel, q_scale=ATTN_Q_SCALE, n_lat_tiles=n_lat_tiles),
        grid=(n // tm,),
        in_specs=[
            pl.BlockSpec((tm, d), lambda i: (jnp.minimum(i, n_lat_tiles - 1), 0)),
            pl.BlockSpec((tm, d), lambda i: (jnp.maximum(i - n_lat_tiles, 0), 0)),
            pl.BlockSpec((1, 1, d), lambda i: (mod_base + cond_row(i) * 6 + 0, 0, 0)),
            pl.BlockSpec((1, 1, d), lambda i: (mod_base + cond_row(i) * 6 + 1, 0, 0)),
            pl.BlockSpec((d, n_in), lambda i: (0, 0), pipeline_mode=pl.Buffered(1)),
            pl.BlockSpec((1, HEAD_DIM), lambda i: (0, 0)),
            pl.BlockSpec((1, HEAD_DIM), lambda i: (0, 0)),
            pl.BlockSpec((3, tm, HEAD_DIM), lambda i: (0, rope_blk(i), 0)),
        ],
        out_specs=[
            pl.BlockSpec((tm, a_w + 2 * kv_w), lambda i: (i, 0)),
            pl.BlockSpec((tm, b_w), lambda i: (i, 0)),
        ],
        out_shape=[
            jax.ShapeDtypeStruct((n, a_w + 2 * kv_w), BF16),
            jax.ShapeDtypeStruct((n, b_w), F32),
        ],
        scratch_shapes=[pltpu.VMEM((tm, d), BF16)],
        compiler_params=_params("arbitrary"),
        name="inproj_ab",
    )(x2d, c2d, mod3, mod3, w_in, q_gain, k_gain, rope)


def _rope_tables(seq, tm):
    t = jnp.arange(seq)
    nfreq = HEAD_DIM // 4
    inv = ROPE_THETA ** (-jnp.arange(nfreq, dtype=F32) / nfreq)
    ang_r = (t // GRID_W).astype(F32)[:, None] * inv
    ang_c = (t % GRID_W).astype(F32)[:, None] * inv
    ang = jnp.concatenate([ang_r, ang_r, ang_c, ang_c], axis=-1)
    cos, sin = jnp.cos(ang), jnp.sin(ang)
    low = (np.arange(HEAD_DIM) // nfreq) % 2 == 0
    sin_lo = jnp.where(low, -sin, 0.0)
    sin_hi = jnp.where(low, 0.0, sin)
    tab = jnp.stack([cos, sin_lo, sin_hi])
    ident = jnp.stack([jnp.ones((tm, HEAD_DIM), F32), jnp.zeros((tm, HEAD_DIM), F32), jnp.zeros((tm, HEAD_DIM), F32)])
    return jnp.concatenate([tab, ident], axis=1)


def _softmax_pv(q, parts):
    ss = [_dot_t(q, k_ref[...]) for k_ref, _ in parts]
    m = functools.reduce(jnp.maximum, [jnp.max(s, axis=-1, keepdims=True) for s in ss])
    ps = [jnp.exp2(s - m) for s in ss]
    l = functools.reduce(jnp.add, [jnp.sum(p, axis=-1, keepdims=True) for p in ps])
    o = functools.reduce(jnp.add, [_dot(p.astype(BF16), v_ref[...]) for p, (_, v_ref) in zip(ps, parts)])
    return o / l


def _gqa_kernel(q_ref, kx_ref, vx_ref, kc_ref, vc_ref, o_ref, *, n_lat_tiles):
    qi = pl.program_id(1)

    def run(parts):
        for g in range(A_GROUP):
            cs = slice(g * HEAD_DIM, (g + 1) * HEAD_DIM)
            o_ref[:, cs] = _softmax_pv(q_ref[:, cs], parts).astype(BF16)

    @pl.when(qi < n_lat_tiles)
    def _():
        run([(kx_ref, vx_ref), (kc_ref, vc_ref)])

    @pl.when(qi >= n_lat_tiles)
    def _():
        run([(kc_ref, vc_ref)])


def _gqa(qkv, *, nbatch, seq, ctx_len):
    n = qkv.shape[0]
    tq = _pick((seq, ctx_len), (256, 128))
    n_lat_tiles = nbatch * seq // tq
    lat_per_seq, ctx_per_seq = seq // tq, ctx_len // tq
    a_w = A_HEADS * HEAD_DIM
    gw = A_GROUP * HEAD_DIM
    k_col0 = a_w // HEAD_DIM
    v_col0 = k_col0 + A_KV_HEADS

    def batch(qi):
        return jnp.where(qi < n_lat_tiles, qi // lat_per_seq, (qi - n_lat_tiles) // ctx_per_seq)

    def lat_batch(qi):
        return jnp.minimum(qi // lat_per_seq, nbatch - 1)

    ctx_blk0 = nbatch * seq // ctx_len
    return pl.pallas_call(
        functools.partial(_gqa_kernel, n_lat_tiles=n_lat_tiles),
        grid=(A_KV_HEADS, n // tq),
        in_specs=[
            pl.BlockSpec((tq, gw), lambda h, qi: (qi, h)),
            pl.BlockSpec((seq, HEAD_DIM), lambda h, qi: (lat_batch(qi), k_col0 + h)),
            pl.BlockSpec((seq, HEAD_DIM), lambda h, qi: (lat_batch(qi), v_col0 + h)),
            pl.BlockSpec((ctx_len, HEAD_DIM), lambda h, qi: (ctx_blk0 + batch(qi), k_col0 + h)),
            pl.BlockSpec((ctx_len, HEAD_DIM), lambda h, qi: (ctx_blk0 + batch(qi), v_col0 + h)),
        ],
        out_specs=pl.BlockSpec((tq, gw), lambda h, qi: (qi, h)),
        out_shape=jax.ShapeDtypeStruct((n, a_w), BF16),
        compiler_params=_params("arbitrary", "arbitrary"),
        name="gqa_attn",
    )(qkv, qkv, qkv, qkv, qkv)


def _conv_kernel(x_ref, xp_ref, xn_ref, dw_ref, db_ref, g_ref, b_ref, o_ref, buf_ref, y_ref, shift_ref, *,
                 tt, halo, n_lat_tiles, lat_per_seq, ctx_per_seq):
    i = pl.program_id(0)
    taps = dw_ref.shape[0]
    pad = taps // 2
    idx = jnp.where(i < n_lat_tiles, i % lat_per_seq, (i - n_lat_tiles) % ctx_per_seq)
    per_seq = jnp.where(i < n_lat_tiles, lat_per_seq, ctx_per_seq)
    buf_ref[0:halo, :] = jnp.where(idx > 0, xp_ref[...], 0.0)
    buf_ref[halo:halo + tt, :] = x_ref[...]
    buf_ref[halo + tt:, :] = jnp.where(idx < per_seq - 1, xn_ref[...], 0.0)
    nchunk = x_ref.shape[1] // V7X_LANES

    rows = tt + 2 * halo

    def chunk(c, carry):
        cs = pl.ds(pl.multiple_of(c * V7X_LANES, V7X_LANES), V7X_LANES)
        xb = buf_ref[:, cs]
        for s in range(1, V7X_F32_SUBLANES):
            shift_ref[s] = pltpu.roll(xb, rows - s, 0)
        acc = jnp.broadcast_to(db_ref[:, cs], (tt, V7X_LANES))
        for k in range(taps):
            off = halo - pad + k
            base, s = off - off % V7X_F32_SUBLANES, off % V7X_F32_SUBLANES
            win = buf_ref[pl.ds(base, tt), cs] if s == 0 else shift_ref[s, base:base + tt, :]
            acc = acc + win * dw_ref[pl.ds(k, 1), cs]
        y_ref[:, cs] = acc
        return carry

    lax.fori_loop(0, nchunk, chunk, 0)
    yn = _layer_norm(y_ref[...], g_ref[...], b_ref[...])
    o_ref[...] = (yn * _sigmoid(yn)).astype(BF16)


def _conformer_conv(glu, dw_w, dw_b, n_g, n_b, *, nbatch, seq, ctx_len):
    n, cw = glu.shape
    tt = _pick((seq, ctx_len), (256, 128))
    halo = 16
    assert dw_w.shape[0] // 2 <= halo
    n_lat_tiles = nbatch * seq // tt
    hb = tt // halo
    nhalo = n // halo
    return pl.pallas_call(
        functools.partial(_conv_kernel, tt=tt, halo=halo, n_lat_tiles=n_lat_tiles,
                          lat_per_seq=seq // tt, ctx_per_seq=ctx_len // tt),
        grid=(n // tt,),
        in_specs=[
            pl.BlockSpec((tt, cw), lambda i: (i, 0)),
            pl.BlockSpec((halo, cw), lambda i: (jnp.maximum(i * hb - 1, 0), 0)),
            pl.BlockSpec((halo, cw), lambda i: (jnp.minimum((i + 1) * hb, nhalo - 1), 0)),
            pl.BlockSpec(dw_w.shape, lambda i: (0, 0)),
            pl.BlockSpec((1, cw), lambda i: (0, 0)),
            pl.BlockSpec((1, cw), lambda i: (0, 0)),
            pl.BlockSpec((1, cw), lambda i: (0, 0)),
        ],
        out_specs=pl.BlockSpec((tt, cw), lambda i: (i, 0)),
        out_shape=jax.ShapeDtypeStruct((n, cw), BF16),
        scratch_shapes=[pltpu.VMEM((tt + 2 * halo, cw), F32), pltpu.VMEM((tt, cw), F32),
                        pltpu.VMEM((V7X_F32_SUBLANES, tt + 2 * halo, V7X_LANES), F32)],
        compiler_params=_params("arbitrary"),
        name="conformer_conv",
    )(glu, glu, glu, dw_w, dw_b.reshape(1, cw), n_g.reshape(1, cw), n_b.reshape(1, cw))


def _outproj_kernel(*refs, n_parts, n_res, n_lat_tiles, alpha):
    part_refs = refs[:n_parts]
    res_refs = refs[n_parts + 1:n_parts + 1 + n_res]
    w_ref = refs[n_parts]
    gate_ref, sh_ref, sc_ref, lg_ref, lb_ref, x1_ref, h2_ref = refs[n_parts + 1 + n_res:]
    y = None
    k0 = 0
    for p_ref in part_refs:
        kw = p_ref.shape[1]
        t = _dot(p_ref[...], w_ref[k0:k0 + kw, :])
        y = t if y is None else y + t
        k0 += kw

    def finish(x_ref):
        x1 = _layer_norm(alpha * x_ref[...] + gate_ref[0] * y, lg_ref[...], lb_ref[...])
        x1_ref[...] = x1
        h2_ref[...] = (x1 * (1.0 + sc_ref[0]) + sh_ref[0]).astype(BF16)

    if n_res == 1:
        finish(res_refs[0])
    else:
        is_lat = pl.program_id(0) < n_lat_tiles
        pl.when(is_lat)(lambda: finish(res_refs[0]))
        pl.when(jnp.logical_not(is_lat))(lambda: finish(res_refs[1]))


def _outproj(parts, w_out, residuals, mod3, mod_base, ln_g, ln_b, *, n_out, tm, n_lat_tiles, seq, nbatch, alpha):
    d = residuals[0].shape[1]
    if len(residuals) == 1:
        res_specs = [pl.BlockSpec((tm, d), lambda i: (i, 0))]
    else:
        res_specs = [pl.BlockSpec((tm, d), lambda i: (jnp.minimum(i, n_lat_tiles - 1), 0)),
                     pl.BlockSpec((tm, d), lambda i: (jnp.maximum(i - n_lat_tiles, 0), 0))]

    def cond_row(i):
        return jnp.where(i < n_lat_tiles, (i * tm) // seq, nbatch)

    def mod_spec(k):
        return pl.BlockSpec((1, 1, d), lambda i: (mod_base + cond_row(i) * 6 + k, 0, 0))

    return pl.pallas_call(
        functools.partial(_outproj_kernel, n_parts=len(parts), n_res=len(residuals), n_lat_tiles=n_lat_tiles,
                          alpha=alpha),
        grid=(n_out // tm,),
        in_specs=[pl.BlockSpec((tm, p.shape[1]), lambda i: (i, 0)) for p in parts] + [
            pl.BlockSpec(w_out.shape, lambda i: (0, 0), pipeline_mode=pl.Buffered(1))] + res_specs + [
            mod_spec(2), mod_spec(3), mod_spec(4),
            pl.BlockSpec((1, d), lambda i: (0, 0)),
            pl.BlockSpec((1, d), lambda i: (0, 0)),
        ],
        out_specs=[pl.BlockSpec((tm, d), lambda i: (i, 0)), pl.BlockSpec((tm, d), lambda i: (i, 0))],
        out_shape=[jax.ShapeDtypeStruct((n_out, d), F32), jax.ShapeDtypeStruct((n_out, d), BF16)],
        compiler_params=_params("arbitrary"),
        name="outproj_ln",
    )(*parts, w_out, *residuals, mod3, mod3, mod3, ln_g.reshape(1, d), ln_b.reshape(1, d))


def _ffn_kernel(h_ref, hp_ref, hn_ref, wa_ref, wg_ref, dwa_ref, dwg_ref, dba_ref, dbg_ref, wd_ref,
                x_ref, gate_ref, lg_ref, lb_ref, o_ref,
                hext_ref, za0_ref, za1_ref, zg0_ref, zg1_ref, acc_ref, *,
                tm, halo, sub, nf, seq, ctx_len, n_lat_tiles, alpha):
    za_refs = (za0_ref, za1_ref)
    zg_refs = (zg0_ref, zg1_ref)
    i = pl.program_id(0)
    c = pl.program_id(1)
    is_lat = i < n_lat_tiles
    row = lax.broadcasted_iota(jnp.int32, (tm, 1), 0) + i * tm
    pos = row & (ctx_len - 1)
    has_prev = pos != 0
    has_next = pos != ctx_len - 1

    chunks = [slice(s0, s0 + sub) for s0 in range(0, wa_ref.shape[1], sub)]

    def up_proj(slot):
        za_refs[slot][...] = _dot(hext_ref[...], wa_ref[...])
        zg_refs[slot][...] = _dot(hext_ref[...], wg_ref[...])

    def conv(z_ref, dw_ref, db_ref, cs, masked):
        z = z_ref[:, cs]
        rows_ext = tm + 2 * halo
        z_prev = pltpu.roll(z, 1, 0)[halo:halo + tm]
        z_next = pltpu.roll(z, rows_ext - 1, 0)[halo:halo + tm]
        if masked:
            z_prev = jnp.where(has_prev, z_prev, 0.0)
            z_next = jnp.where(has_next, z_next, 0.0)
        return z_prev * dw_ref[0:1, cs] + z[halo:halo + tm] * dw_ref[1:2, cs] + z_next * dw_ref[2:3, cs] + db_ref[:, cs]

    def step(up_slot, down_slot, masked=False):
        if up_slot is not None:
            up_proj(up_slot)
        if down_slot is not None:
            contrib = None
            for cs in chunks:
                a = conv(za_refs[down_slot], dwa_ref, dba_ref, cs, masked)
                g = conv(zg_refs[down_slot], dwg_ref, dbg_ref, cs, masked)
                t = _dot_t(wd_ref[:, cs], (g * _sigmoid(g) * a).astype(BF16))
                contrib = t if contrib is None else contrib + t
            acc_ref[...] += contrib

    @pl.when(c == 0)
    def _():
        starts_seq = is_lat & ((i * tm) % seq == 0)
        ends_seq = is_lat & (((i + 1) * tm) % seq == 0)
        zero_halo = jnp.zeros(hp_ref.shape, hp_ref.dtype)
        hext_ref[0:halo, :] = jnp.where(starts_seq, zero_halo, hp_ref[...])
        hext_ref[halo:halo + tm, :] = h_ref[...]
        hext_ref[halo + tm:, :] = jnp.where(ends_seq, zero_halo, hn_ref[...])
        acc_ref[...] = jnp.zeros_like(acc_ref)
        step(0, None)

    for masked in (False, True):
        for slot in (0, 1):
            @pl.when((c >= 1) & (c < nf) & (c % 2 == slot) & (is_lat != masked))
            def _():
                step(slot, 1 - slot, masked)

        @pl.when((c == nf) & (is_lat != masked))
        def _():
            step(None, (nf - 1) % 2, masked)

    @pl.when(c == nf)
    def _():
        o_ref[...] = _layer_norm(alpha * x_ref[...] + gate_ref[0] * acc_ref[...].T, lg_ref[...], lb_ref[...])


def _ffn(h2, x1, layer, w_up, dw_w, dw_b, w_down, mod3, mod_base, ln_g, ln_b, *, tm, n_lat_tiles, seq, ctx_len,
         nbatch, alpha):
    n_out, d = x1.shape
    depth, _, d_ff = w_down.shape
    tf = _pick((d_ff,), (512, 256, 128))
    nf = d_ff // tf
    halo = V7X_BF16_SUBLANES
    hb = tm // halo
    nhalo = h2.shape[0] // halo
    ntiles = n_out // tm
    assert dw_w.shape[1] == 3

    def cond_row(i):
        return jnp.where(i < n_lat_tiles, (i * tm) // seq, nbatch)

    dw_b3 = dw_b.reshape(depth, 1, 2 * d_ff)

    def up(c):
        return jnp.minimum(c, nf - 1)

    def down(c):
        return jnp.maximum(c - 1, 0)

    mid = nf // 2

    def h_tile(i, c):
        return jnp.where(c < mid, i, jnp.minimum(i + 1, ntiles - 1))

    def x_tile(i, c):
        return jnp.where(c < mid, jnp.maximum(i - 1, 0), i)

    return pl.pallas_call(
        functools.partial(_ffn_kernel, tm=tm, halo=halo, sub=min(tf, 256), nf=nf, seq=seq, ctx_len=ctx_len,
                          n_lat_tiles=n_lat_tiles, alpha=alpha),
        grid=(ntiles, nf + 1),
        in_specs=[
            pl.BlockSpec((tm, d), lambda i, c: (h_tile(i, c), 0)),
            pl.BlockSpec((halo, d), lambda i, c: (jnp.maximum(i * hb - 1, 0), 0)),
            pl.BlockSpec((halo, d), lambda i, c: (jnp.minimum((i + 1) * hb, nhalo - 1), 0)),
            pl.BlockSpec((None, d, tf), lambda i, c: (layer, 0, up(c))),
            pl.BlockSpec((None, d, tf), lambda i, c: (layer, 0, nf + up(c))),
            pl.BlockSpec((None, 3, tf), lambda i, c: (layer, 0, down(c))),
            pl.BlockSpec((None, 3, tf), lambda i, c: (layer, 0, nf + down(c))),
            pl.BlockSpec((None, 1, tf), lambda i, c: (layer, 0, down(c))),
            pl.BlockSpec((None, 1, tf), lambda i, c: (layer, 0, nf + down(c))),
            pl.BlockSpec((None, d, tf), lambda i, c: (layer, 0, down(c))),
            pl.BlockSpec((tm, d), lambda i, c: (x_tile(i, c), 0)),
            pl.BlockSpec((1, 1, d), lambda i, c: (mod_base + cond_row(i) * 6 + 5, 0, 0)),
            pl.BlockSpec((1, d), lambda i, c: (0, 0)),
            pl.BlockSpec((1, d), lambda i, c: (0, 0)),
        ],
        out_specs=pl.BlockSpec((tm, d), lambda i, c: (i, 0)),
        out_shape=jax.ShapeDtypeStruct((n_out, d), F32),
        scratch_shapes=([pltpu.VMEM((tm + 2 * halo, d), BF16)] + [pltpu.VMEM((tm + 2 * halo, tf), F32)] * 4
                        + [pltpu.VMEM((d, tm), F32)]),
        compiler_params=_params("arbitrary", "arbitrary"),
        name="conv_ffn",
    )(h2, h2, h2, w_up, w_up, dw_w, dw_w, dw_b3, dw_b3, w_down, x1, mod3, ln_g.reshape(1, d), ln_b.reshape(1, d))


def _inproj_c_kernel(x_ref, sh_ref, sc_ref, w_ref, o_ref, *, q_scale):
    hb = (x_ref[...] * (1.0 + sc_ref[0]) + sh_ref[0]).astype(BF16)
    d = x_ref.shape[1]
    cols = 512
    for c0 in range(0, o_ref.shape[1], cols):
        z = _dot(hb, w_ref[:, c0:c0 + cols])
        if c0 < d:
            z = z * q_scale
        o_ref[:, c0:c0 + cols] = z.astype(BF16)


def _inproj_c(xc, mod3, mod_base, w_in, *, tm, n_lat_tiles, seq, nbatch):
    n, d = xc.shape
    n_in = w_in.shape[1]
    assert d % 512 == 0 and n_in == 3 * d

    def cond_row(i):
        return jnp.where(i < n_lat_tiles, (i * tm) // seq, nbatch)

    return pl.pallas_call(
        functools.partial(_inproj_c_kernel, q_scale=ATTN_Q_SCALE),
        grid=(n // tm,),
        in_specs=[
            pl.BlockSpec((tm, d), lambda i: (i, 0)),
            pl.BlockSpec((1, 1, d), lambda i: (mod_base + cond_row(i) * 6 + 0, 0, 0)),
            pl.BlockSpec((1, 1, d), lambda i: (mod_base + cond_row(i) * 6 + 1, 0, 0)),
            pl.BlockSpec((d, n_in), lambda i: (0, 0), pipeline_mode=pl.Buffered(1)),
        ],
        out_specs=pl.BlockSpec((tm, n_in), lambda i: (i, 0)),
        out_shape=jax.ShapeDtypeStruct((n, n_in), BF16),
        compiler_params=_params("arbitrary"),
        name="inproj_c",
    )(xc, mod3, mod3, w_in)


def _na_bias_table(rel_bias):
    nheads, nrow, ncol = rel_bias.shape
    qc = np.arange(GRID_W)[:, None]
    kc = np.arange(GRID_W)[None, :]
    cstart = np.clip(qc - NA_COLS // 2, 0, GRID_W - NA_COLS)
    valid = (kc >= cstart) & (kc < cstart + NA_COLS)
    period = 2 * GRID_W - 1
    lead = GRID_W - NA_COLS - 1
    padded = jnp.pad(rel_bias, ((0, 0), (0, 0), (lead, period - lead - ncol)), constant_values=NEG)
    tiled = jnp.broadcast_to(padded[:, :, None, :], (nheads, nrow, GRID_W, period))
    skew = tiled.reshape(nheads, nrow, GRID_W * period)[:, :, :GRID_W * (period - 1)]
    toep = skew.reshape(nheads, nrow, GRID_W, period - 1)[:, :, :, GRID_W - 2:]
    toep = jnp.where(valid[None, None], toep * LOG2E, NEG)
    neg = jnp.full((nheads, 1, GRID_W, GRID_W), NEG, F32)
    ext = jnp.concatenate([neg, toep, neg], axis=1)
    return jnp.concatenate([ext[:, :2 * NA_ROWS], ext[:, 1:]], axis=-1)


def _natten_kernel(q_ref, k_ref, v_ref, kc_ref, vc_ref, bias_ref, o_ref, *, rows):
    nq = NA_QROWS * GRID_W
    nk = NA_KROWS * GRID_W
    left = lax.broadcasted_iota(jnp.int32, (GRID_W, 2 * GRID_W), 1) < GRID_W
    for blk in range(rows // NA_QROWS):
        ks = min(max(NA_QROWS * blk - NA_ROWS // 2, 0), rows - NA_KROWS)
        q = q_ref[blk * nq:(blk + 1) * nq, :]
        s_loc = _dot_t(q, k_ref[ks * GRID_W:ks * GRID_W + nk, :])
        bias_rows = []
        for qi in range(NA_QROWS):
            qr = NA_QROWS * blk + qi
            r0 = min(max(qr - NA_ROWS // 2, 0), rows - NA_ROWS)
            slabs = []
            for j in range(NA_KROWS // 2):
                kr = ks + 2 * j
                ok0 = r0 <= kr < r0 + NA_ROWS
                ok1 = r0 <= kr + 1 < r0 + NA_ROWS
                if not (ok0 or ok1):
                    slabs.append(jnp.full((GRID_W, 2 * GRID_W), NEG, F32))
                    continue
                slab = bias_ref[0, kr - qr + NA_ROWS]
                if not ok1:
                    slab = jnp.where(left, slab, NEG)
                elif not ok0:
                    slab = jnp.where(left, NEG, slab)
                slabs.append(slab)
            bias_rows.append(jnp.concatenate(slabs, axis=1))
        s_loc = s_loc + jnp.concatenate(bias_rows, axis=0)
        s_ctx = _dot_t(q, kc_ref[...])
        m = jnp.maximum(jnp.max(s_loc, axis=-1, keepdims=True), jnp.max(s_ctx, axis=-1, keepdims=True))
        p_loc = jnp.exp2(s_loc - m)
        p_ctx = jnp.exp2(s_ctx - m)
        l = jnp.sum(p_loc, axis=-1, keepdims=True) + jnp.sum(p_ctx, axis=-1, keepdims=True)
        o = (_dot(p_loc.astype(BF16), v_ref[ks * GRID_W:ks * GRID_W + nk, :])
             + _dot(p_ctx.astype(BF16), vc_ref[...]))
        o_ref[blk * nq:(blk + 1) * nq, :] = (o / l).astype(BF16)


def _natten(qkv, bias_tab, *, nbatch, seq, ctx_len, d):
    rows = seq // GRID_W
    nheads = d // HEAD_DIM
    assert rows % NA_QROWS == 0 and rows >= NA_KROWS
    ctx_blk0 = nbatch * seq // ctx_len
    return pl.pallas_call(
        functools.partial(_natten_kernel, rows=rows),
        grid=(nheads, nbatch),
        in_specs=[
            pl.BlockSpec((seq, HEAD_DIM), lambda h, b: (b, h)),
            pl.BlockSpec((seq, HEAD_DIM), lambda h, b: (b, nheads + h)),
            pl.BlockSpec((seq, HEAD_DIM), lambda h, b: (b, 2 * nheads + h)),
            pl.BlockSpec((ctx_len, HEAD_DIM), lambda h, b: (ctx_blk0 + b, nheads + h)),
            pl.BlockSpec((ctx_len, HEAD_DIM), lambda h, b: (ctx_blk0 + b, 2 * nheads + h)),
            pl.BlockSpec((1,) + bias_tab.shape[1:], lambda h, b: (h, 0, 0, 0)),
        ],
        out_specs=pl.BlockSpec((seq, HEAD_DIM), lambda h, b: (b, h)),
        out_shape=jax.ShapeDtypeStruct((nbatch * seq, d), BF16),
        compiler_params=_params("arbitrary", "arbitrary"),
        name="natten",
    )(qkv, qkv, qkv, qkv, qkv, bias_tab)


def kernel(x, c, ctx, c_ctx, ada_w, ada_b, post_ln_g, post_ln_b, ab_w_in, ab_w_out, ab_q_gain, ab_k_gain,
           ab_dw_w, ab_dw_b, ab_norm_g, ab_norm_b, c_w_in, c_w_out, c_rel_bias, ffn_w_up, ffn_dw_w, ffn_dw_b,
           ffn_w_down):
    nbatch, seq, d = x.shape
    ctx_len = ctx.shape[1]
    depth = ada_w.shape[0]
    assert depth == 2 and nbatch + 1 <= COND_ROWS and nbatch * ctx_len <= seq
    assert seq % GRID_W == 0 and seq & (seq - 1) == 0 and ctx_len & (ctx_len - 1) == 0 and ctx_len <= seq
    nx, nc = nbatch * seq, nbatch * ctx_len
    alpha = (2 * depth) ** 0.25
    tm = _pick((seq, nc), (512, 256))
    n_lat_tiles = nx // tm
    geo = dict(tm=tm, n_lat_tiles=n_lat_tiles, seq=seq, nbatch=nbatch)

    cond = jnp.concatenate([c, c_ctx[None], jnp.zeros((COND_ROWS - nbatch - 1, d), F32)], axis=0)
    mod3 = _ada(cond, ada_w, ada_b).reshape(depth * COND_ROWS * 6, 1, d)
    x2d, c2d = x.reshape(nx, d), ctx.reshape(nc, d)
    ffn_w = (ffn_w_up.astype(BF16), ffn_dw_w, ffn_dw_b, jnp.swapaxes(ffn_w_down, 1, 2).astype(BF16))

    mod_base = 0
    qkv, glu = _inproj_ab(x2d, c2d, mod3, mod_base, ab_w_in[0].astype(BF16), ab_q_gain[0][None],
                          ab_k_gain[0][None], _rope_tables(seq, tm), **geo)
    attn = _gqa(qkv, nbatch=nbatch, seq=seq, ctx_len=ctx_len)
    conv = _conformer_conv(glu, ab_dw_w[0], ab_dw_b[0], ab_norm_g[0], ab_norm_b[0],
                           nbatch=nbatch, seq=seq, ctx_len=ctx_len)
    x1, h2 = _outproj([attn, conv], ab_w_out[0].astype(BF16), [x2d, c2d], mod3, mod_base, post_ln_g[0, 0],
                      post_ln_b[0, 0], n_out=nx + nc, alpha=alpha, **geo)
    xc = _ffn(h2, x1, 0, *ffn_w, mod3, mod_base, post_ln_g[0, 1], post_ln_b[0, 1], ctx_len=ctx_len, alpha=alpha,
              **geo)

    mod_base = COND_ROWS * 6
    qkv = _inproj_c(xc, mod3, mod_base, c_w_in[0].astype(BF16), **geo)
    attn = _natten(qkv, _na_bias_table(c_rel_bias[0]), nbatch=nbatch, seq=seq, ctx_len=ctx_len, d=d)
    x1, h2 = _outproj([attn], c_w_out[0].astype(BF16), [xc], mod3, mod_base, post_ln_g[1, 0], post_ln_b[1, 0],
                      n_out=nx, alpha=alpha, **geo)
    out = _ffn(h2, x1, 1, *ffn_w, mod3, mod_base, post_ln_g[1, 1], post_ln_b[1, 1], ctx_len=ctx_len, alpha=alpha,
               **geo)
    return out.reshape(nbatch, seq, d)
```

```python
import functools

import jax
import jax.numpy as jnp
import numpy as np
from jax import lax
from jax.experimental import pallas as pl
from jax.experimental.pallas import tpu as pltpu

F32 = jnp.float32
BF16 = jnp.bfloat16

HEAD_DIM = 128
GRID_W = 64
A_HEADS = 8
A_KV_HEADS = 2
A_GROUP = A_HEADS // A_KV_HEADS
ROPE_THETA = 10000.0
NA_ROWS = 8
NA_COLS = 16
LN_EPS = 1e-6
NEG = -1e30
LOG2E = 1.4426950408889634
ATTN_Q_SCALE = HEAD_DIM ** -0.5 * LOG2E

V7X_LANES = 128
V7X_F32_SUBLANES = 8
V7X_BF16_SUBLANES = 16
V7X_VMEM_LIMIT = 56 * 1024 * 1024
COND_ROWS = 8

NA_QROWS = 4
NA_KROWS = NA_QROWS + NA_ROWS


def _pick(n, candidates):
    for t in candidates:
        if all(v % t == 0 for v in n):
            return t
    raise ValueError(f"no tile in {candidates} divides {n}")


def _params(*sem, flags=None):
    return pltpu.CompilerParams(dimension_semantics=sem, vmem_limit_bytes=V7X_VMEM_LIMIT, flags=flags)


def _layer_norm(r, g, b):
    mu = jnp.mean(r, axis=-1, keepdims=True)
    d = r - mu
    var = jnp.mean(d * d, axis=-1, keepdims=True)
    return d * lax.rsqrt(var + LN_EPS) * g + b


def _sigmoid(v):
    return 1.0 / (1.0 + jnp.exp(-v))


def _dot(a, b):
    return jnp.dot(a, b, preferred_element_type=F32)


def _dot_t(a, b):
    return lax.dot_general(a, b, (((1,), (1,)), ((), ())), preferred_element_type=F32)


def _ada_kernel(cond_ref, w_ref, b_ref, o_ref):
    cnd = cond_ref[...]
    s = (cnd * _sigmoid(cnd)).astype(BF16)
    o_ref[0] = _dot(s, w_ref[0].astype(BF16)) + b_ref[0]


def _ada(cond, ada_w, ada_b):
    depth, d, n6 = ada_w.shape
    tn = _pick((n6,), (1024, 512, 256, 128))
    return pl.pallas_call(
        _ada_kernel,
        grid=(depth, n6 // tn),
        in_specs=[
            pl.BlockSpec((COND_ROWS, d), lambda l, j: (0, 0)),
            pl.BlockSpec((1, d, tn), lambda l, j: (l, 0, j)),
            pl.BlockSpec((1, 1, tn), lambda l, j: (l, 0, j)),
        ],
        out_specs=pl.BlockSpec((1, COND_ROWS, tn), lambda l, j: (l, 0, j)),
        out_shape=jax.ShapeDtypeStruct((depth, COND_ROWS, n6), F32),
        compiler_params=_params("arbitrary", "arbitrary"),
        name="ada_mod",
    )(cond, ada_w, ada_b.reshape(depth, 1, n6))


def _inproj_ab_kernel(x_ref, c_ref, sh_ref, sc_ref, w_ref, qg_ref, kg_ref, rope_ref, qkv_ref, glu_ref, hb_ref, *,
                      q_scale, n_lat_tiles):
    is_lat = pl.program_id(0) < n_lat_tiles
    for src_ref, cond in ((x_ref, is_lat), (c_ref, jnp.logical_not(is_lat))):
        @pl.when(cond)
        def _():
            hb_ref[...] = (src_ref[...] * (1.0 + sc_ref[0]) + sh_ref[0]).astype(BF16)
    hb = hb_ref[...]
    cos, sin_lo, sin_hi = rope_ref[0], rope_ref[1], rope_ref[2]
    a_w = A_HEADS * HEAD_DIM
    kv_w = A_KV_HEADS * HEAD_DIM
    b_w = glu_ref.shape[1]

    def norm_rope(z, gain):
        zn = z * lax.rsqrt(jnp.mean(z * z, axis=-1, keepdims=True) + LN_EPS) * gain
        quarter = HEAD_DIM // 4
        return (zn * cos + pltpu.roll(zn, HEAD_DIM - quarter, 1) * sin_lo
                + pltpu.roll(zn, quarter, 1) * sin_hi)

    q_gain = qg_ref[...] * q_scale
    heads_per_dot = 4
    u0 = a_w + 2 * kv_w
    glu_cols = 512 if b_w % 512 == 0 else b_w

    def proj(c0, width):
        return lambda: (_dot(hb, w_ref[:, c0:c0 + width]),)

    def glu_proj(c0):
        return lambda: (_dot(hb, w_ref[:, u0 + c0:u0 + c0 + glu_cols]),
                        _dot(hb, w_ref[:, u0 + b_w + c0:u0 + b_w + c0 + glu_cols]))

    def q_out(c0):
        def write(z):
            for hh in range(heads_per_dot):
                cs = slice(c0 + hh * HEAD_DIM, c0 + (hh + 1) * HEAD_DIM)
                qkv_ref[:, cs] = norm_rope(z[:, hh * HEAD_DIM:(hh + 1) * HEAD_DIM], q_gain).astype(BF16)
        return write

    def kv_out(z):
        for hh in range(A_KV_HEADS):
            cs = slice(a_w + hh * HEAD_DIM, a_w + (hh + 1) * HEAD_DIM)
            qkv_ref[:, cs] = norm_rope(z[:, hh * HEAD_DIM:(hh + 1) * HEAD_DIM], kg_ref[...]).astype(BF16)
        qkv_ref[:, a_w + kv_w:a_w + 2 * kv_w] = z[:, kv_w:].astype(BF16)

    def glu_out(c0):
        def write(u, g):
            glu_ref[:, c0:c0 + glu_cols] = u * _sigmoid(g)
        return write

    stages = [(proj(c0, heads_per_dot * HEAD_DIM), q_out(c0)) for c0 in range(0, a_w, heads_per_dot * HEAD_DIM)]
    stages.append((proj(a_w, 2 * kv_w), kv_out))
    stages += [(glu_proj(c0), glu_out(c0)) for c0 in range(0, b_w, glu_cols)]
    pending = stages[0][0]()
    for k, (_, write) in enumerate(stages):
        upcoming = stages[k + 1][0]() if k + 1 < len(stages) else None
        write(*pending)
        pending = upcoming


def _inproj_ab(x2d, c2d, mod3, mod_base, w_in, q_gain, k_gain, rope, *, tm, n_lat_tiles, seq, nbatch):
    d = x2d.shape[1]
    n = x2d.shape[0] + c2d.shape[0]
    n_in = w_in.shape[1]
    a_w, kv_w = A_HEADS * HEAD_DIM, A_KV_HEADS * HEAD_DIM
    b_w = (n_in - a_w - 2 * kv_w) // 2
    tiles_per_seq = seq // tm

    def cond_row(i):
        return jnp.where(i < n_lat_tiles, (i * tm) // seq, nbatch)

    def rope_blk(i):
        return jnp.where(i < n_lat_tiles, i % tiles_per_seq, tiles_per_seq)

    return pl.pallas_call(
        functools.partial(_inproj_ab_kernel, q_scale=ATTN_Q_SCALE, n_lat_tiles=n_lat_tiles),
        grid=(n // tm,),
        in_specs=[
            pl.BlockSpec((tm, d), lambda i: (jnp.minimum(i, n_lat_tiles - 1), 0)),
            pl.BlockSpec((tm, d), lambda i: (jnp.maximum(i - n_lat_tiles, 0), 0)),
            pl.BlockSpec((1, 1, d), lambda i: (mod_base + cond_row(i) * 6 + 0, 0, 0)),
            pl.BlockSpec((1, 1, d), lambda i: (mod_base + cond_row(i) * 6 + 1, 0, 0)),
            pl.BlockSpec((d, n_in), lambda i: (0, 0), pipeline_mode=pl.Buffered(1)),
            pl.BlockSpec((1, HEAD_DIM), lambda i: (0, 0)),
            pl.BlockSpec((1, HEAD_DIM), lambda i: (0, 0)),
            pl.BlockSpec((3, tm, HEAD_DIM), lambda i: (0, rope_blk(i), 0)),
        ],
        out_specs=[
            pl.BlockSpec((tm, a_w + 2 * kv_w), lambda i: (i, 0)),
            pl.BlockSpec((tm, b_w), lambda i: (i, 0)),
        ],
        out_shape=[
            jax.ShapeDtypeStruct((n, a_w + 2 * kv_w), BF16),
            jax.ShapeDtypeStruct((n, b_w), F32),
        ],
        scratch_shapes=[pltpu.VMEM((tm, d), BF16)],
        compiler_params=_params("arbitrary"),
        name="inproj_ab",
    )(x2d, c2d, mod3, mod3, w_in, q_gain, k_gain, rope)


def _rope_tables(seq, tm):
    t = jnp.arange(seq)
    nfreq = HEAD_DIM // 4
    inv = ROPE_THETA ** (-jnp.arange(nfreq, dtype=F32) / nfreq)
    ang_r = (t // GRID_W).astype(F32)[:, None] * inv
    ang_c = (t % GRID_W).astype(F32)[:, None] * inv
    ang = jnp.concatenate([ang_r, ang_r, ang_c, ang_c], axis=-1)
    cos, sin = jnp.cos(ang), jnp.sin(ang)
    low = (np.arange(HEAD_DIM) // nfreq) % 2 == 0
    sin_lo = jnp.where(low, -sin, 0.0)
    sin_hi = jnp.where(low, 0.0, sin)
    tab = jnp.stack([cos, sin_lo, sin_hi])
    ident = jnp.stack([jnp.ones((tm, HEAD_DIM), F32), jnp.zeros((tm, HEAD_DIM), F32), jnp.zeros((tm, HEAD_DIM), F32)])
    return jnp.concatenate([tab, ident], axis=1)


def _scores(q, parts):
    return [_dot_t(q, k_ref[...]) for k_ref, _ in parts]


def _softmax_pv(ss, parts):
    m = functools.reduce(jnp.maximum, [jnp.max(s, axis=-1, keepdims=True) for s in ss])
    ps = [jnp.exp2(s - m) for s in ss]
    l = functools.reduce(jnp.add, [jnp.sum(p, axis=-1, keepdims=True) for p in ps])
    o = functools.reduce(jnp.add, [_dot(p.astype(BF16), v_ref[...]) for p, (_, v_ref) in zip(ps, parts)])
    return o / l


def _gqa_kernel(q_ref, kx_ref, vx_ref, kc_ref, vc_ref, o_ref, *, n_lat_tiles):
    qi = pl.program_id(1)

    def run(parts):
        heads = [slice(g * HEAD_DIM, (g + 1) * HEAD_DIM) for g in range(A_GROUP)]
        ahead = 1
        pending = [_scores(q_ref[:, cs], parts) for cs in heads[:ahead]]
        for g, cs in enumerate(heads):
            if g + ahead < A_GROUP:
                pending.append(_scores(q_ref[:, heads[g + ahead]], parts))
            o_ref[:, cs] = _softmax_pv(pending.pop(0), parts).astype(BF16)

    @pl.when(qi < n_lat_tiles)
    def _():
        run([(kx_ref, vx_ref), (kc_ref, vc_ref)])

    @pl.when(qi >= n_lat_tiles)
    def _():
        run([(kc_ref, vc_ref)])


def _gqa(qkv, *, nbatch, seq, ctx_len):
    n = qkv.shape[0]
    tq = _pick((seq, ctx_len), (256, 128))
    n_lat_tiles = nbatch * seq // tq
    lat_per_seq, ctx_per_seq = seq // tq, ctx_len // tq
    a_w = A_HEADS * HEAD_DIM
    gw = A_GROUP * HEAD_DIM
    k_col0 = a_w // HEAD_DIM
    v_col0 = k_col0 + A_KV_HEADS

    def batch(qi):
        return jnp.where(qi < n_lat_tiles, qi // lat_per_seq, (qi - n_lat_tiles) // ctx_per_seq)

    def lat_batch(qi):
        return jnp.minimum(qi // lat_per_seq, nbatch - 1)

    ctx_blk0 = nbatch * seq // ctx_len
    return pl.pallas_call(
        functools.partial(_gqa_kernel, n_lat_tiles=n_lat_tiles),
        grid=(A_KV_HEADS, n // tq),
        in_specs=[
            pl.BlockSpec((tq, gw), lambda h, qi: (qi, h)),
            pl.BlockSpec((seq, HEAD_DIM), lambda h, qi: (lat_batch(qi), k_col0 + h)),
            pl.BlockSpec((seq, HEAD_DIM), lambda h, qi: (lat_batch(qi), v_col0 + h)),
            pl.BlockSpec((ctx_len, HEAD_DIM), lambda h, qi: (ctx_blk0 + batch(qi), k_col0 + h)),
            pl.BlockSpec((ctx_len, HEAD_DIM), lambda h, qi: (ctx_blk0 + batch(qi), v_col0 + h)),
        ],
        out_specs=pl.BlockSpec((tq, gw), lambda h, qi: (qi, h)),
        out_shape=jax.ShapeDtypeStruct((n, a_w), BF16),
        compiler_params=_params("arbitrary", "arbitrary"),
        name="gqa_attn",
    )(qkv, qkv, qkv, qkv, qkv)


def _conv_kernel(x_ref, xp_ref, xn_ref, dw_ref, db_ref, g_ref, b_ref, o_ref, buf_ref, y_ref, shift_ref, *,
                 tt, halo, n_lat_tiles, lat_per_seq, ctx_per_seq):
    i = pl.program_id(0)
    taps = dw_ref.shape[0]
    pad = taps // 2
    idx = jnp.where(i < n_lat_tiles, i % lat_per_seq, (i - n_lat_tiles) % ctx_per_seq)
    per_seq = jnp.where(i < n_lat_tiles, lat_per_seq, ctx_per_seq)
    buf_ref[0:halo, :] = jnp.where(idx > 0, xp_ref[...], 0.0)
    buf_ref[halo:halo + tt, :] = x_ref[...]
    buf_ref[halo + tt:, :] = jnp.where(idx < per_seq - 1, xn_ref[...], 0.0)
    nchunk = x_ref.shape[1] // V7X_LANES

    rows = tt + 2 * halo

    def chunk(c, carry):
        cs = pl.ds(pl.multiple_of(c * V7X_LANES, V7X_LANES), V7X_LANES)
        xb = buf_ref[:, cs]
        for s in range(1, V7X_F32_SUBLANES):
            shift_ref[s] = pltpu.roll(xb, rows - s, 0)
        acc = jnp.broadcast_to(db_ref[:, cs], (tt, V7X_LANES))
        for k in range(taps):
            off = halo - pad + k
            base, s = off - off % V7X_F32_SUBLANES, off % V7X_F32_SUBLANES
            win = buf_ref[pl.ds(base, tt), cs] if s == 0 else shift_ref[s, base:base + tt, :]
            acc = acc + win * dw_ref[pl.ds(k, 1), cs]
        y_ref[:, cs] = acc
        return carry

    lax.fori_loop(0, nchunk, chunk, 0)
    yn = _layer_norm(y_ref[...], g_ref[...], b_ref[...])
    o_ref[...] = (yn * _sigmoid(yn)).astype(BF16)


def _conformer_conv(glu, dw_w, dw_b, n_g, n_b, *, nbatch, seq, ctx_len):
    n, cw = glu.shape
    tt = _pick((seq, ctx_len), (256, 128))
    halo = 16
    assert dw_w.shape[0] // 2 <= halo
    n_lat_tiles = nbatch * seq // tt
    hb = tt // halo
    nhalo = n // halo
    return pl.pallas_call(
        functools.partial(_conv_kernel, tt=tt, halo=halo, n_lat_tiles=n_lat_tiles,
                          lat_per_seq=seq // tt, ctx_per_seq=ctx_len // tt),
        grid=(n // tt,),
        in_specs=[
            pl.BlockSpec((tt, cw), lambda i: (i, 0)),
            pl.BlockSpec((halo, cw), lambda i: (jnp.maximum(i * hb - 1, 0), 0)),
            pl.BlockSpec((halo, cw), lambda i: (jnp.minimum((i + 1) * hb, nhalo - 1), 0)),
            pl.BlockSpec(dw_w.shape, lambda i: (0, 0)),
            pl.BlockSpec((1, cw), lambda i: (0, 0)),
            pl.BlockSpec((1, cw), lambda i: (0, 0)),
            pl.BlockSpec((1, cw), lambda i: (0, 0)),
        ],
        out_specs=pl.BlockSpec((tt, cw), lambda i: (i, 0)),
        out_shape=jax.ShapeDtypeStruct((n, cw), BF16),
        scratch_shapes=[pltpu.VMEM((tt + 2 * halo, cw), F32), pltpu.VMEM((tt, cw), F32),
                        pltpu.VMEM((V7X_F32_SUBLANES, tt + 2 * halo, V7X_LANES), F32)],
        compiler_params=_params("arbitrary"),
        name="conformer_conv",
    )(glu, glu, glu, dw_w, dw_b.reshape(1, cw), n_g.reshape(1, cw), n_b.reshape(1, cw))


def _outproj_kernel(*refs, n_parts, n_res, n_lat_tiles, alpha):
    part_refs = refs[:n_parts]
    res_refs = refs[n_parts + 1:n_parts + 1 + n_res]
    w_ref = refs[n_parts]
    gate_ref, sh_ref, sc_ref, lg_ref, lb_ref, x1_ref, h2_ref = refs[n_parts + 1 + n_res:]
    tm = x1_ref.shape[0]
    rb = tm // 2 if tm % (2 * V7X_BF16_SUBLANES) == 0 else tm

    def proj(r0):
        y = None
        k0 = 0
        for p_ref in part_refs:
            kw = p_ref.shape[1]
            t = _dot(p_ref[r0:r0 + rb, :], w_ref[k0:k0 + kw, :])
            y = t if y is None else y + t
            k0 += kw
        return y

    def finish(x_ref, r0, y):
        rs = slice(r0, r0 + rb)
        x1 = _layer_norm(alpha * x_ref[rs, :] + gate_ref[0] * y, lg_ref[...], lb_ref[...])
        x1_ref[rs, :] = x1
        h2_ref[rs, :] = (x1 * (1.0 + sc_ref[0]) + sh_ref[0]).astype(BF16)

    def body(x_ref):
        pending = proj(0)
        for r0 in range(0, tm, rb):
            upcoming = proj(r0 + rb) if r0 + rb < tm else None
            finish(x_ref, r0, pending)
            pending = upcoming

    if n_res == 1:
        body(res_refs[0])
    else:
        is_lat = pl.program_id(0) < n_lat_tiles
        pl.when(is_lat)(functools.partial(body, res_refs[0]))
        pl.when(jnp.logical_not(is_lat))(functools.partial(body, res_refs[1]))


def _outproj(parts, w_out, residuals, mod3, mod_base, ln_g, ln_b, *, n_out, tm, n_lat_tiles, seq, nbatch, alpha):
    d = residuals[0].shape[1]
    if len(residuals) == 1:
        res_specs = [pl.BlockSpec((tm, d), lambda i: (i, 0))]
    else:
        res_specs = [pl.BlockSpec((tm, d), lambda i: (jnp.minimum(i, n_lat_tiles - 1), 0)),
                     pl.BlockSpec((tm, d), lambda i: (jnp.maximum(i - n_lat_tiles, 0), 0))]

    def cond_row(i):
        return jnp.where(i < n_lat_tiles, (i * tm) // seq, nbatch)

    def mod_spec(k):
        return pl.BlockSpec((1, 1, d), lambda i: (mod_base + cond_row(i) * 6 + k, 0, 0))

    return pl.pallas_call(
        functools.partial(_outproj_kernel, n_parts=len(parts), n_res=len(residuals), n_lat_tiles=n_lat_tiles,
                          alpha=alpha),
        grid=(n_out // tm,),
        in_specs=[pl.BlockSpec((tm, p.shape[1]), lambda i: (i, 0)) for p in parts] + [
            pl.BlockSpec(w_out.shape, lambda i: (0, 0), pipeline_mode=pl.Buffered(1))] + res_specs + [
            mod_spec(2), mod_spec(3), mod_spec(4),
            pl.BlockSpec((1, d), lambda i: (0, 0)),
            pl.BlockSpec((1, d), lambda i: (0, 0)),
        ],
        out_specs=[pl.BlockSpec((tm, d), lambda i: (i, 0)), pl.BlockSpec((tm, d), lambda i: (i, 0))],
        out_shape=[jax.ShapeDtypeStruct((n_out, d), F32), jax.ShapeDtypeStruct((n_out, d), BF16)],
        compiler_params=_params("arbitrary"),
        name="outproj_ln",
    )(*parts, w_out, *residuals, mod3, mod3, mod3, ln_g.reshape(1, d), ln_b.reshape(1, d))


def _ffn_kernel(h_ref, hp_ref, hn_ref, wa_ref, wg_ref, dwa_ref, dwg_ref, dba_ref, dbg_ref, wd_ref,
                x_ref, gate_ref, lg_ref, lb_ref, o_ref,
                hext_ref, za0_ref, za1_ref, zg0_ref, zg1_ref, acc_ref, *,
                tm, halo, sub, nf, seq, ctx_len, n_lat_tiles, alpha):
    za_refs = (za0_ref, za1_ref)
    zg_refs = (zg0_ref, zg1_ref)
    i = pl.program_id(0)
    c = pl.program_id(1)
    is_lat = i < n_lat_tiles
    row = lax.broadcasted_iota(jnp.int32, (tm, 1), 0) + i * tm
    pos = row & (ctx_len - 1)
    has_prev = pos != 0
    has_next = pos != ctx_len - 1

    chunks = [slice(s0, s0 + sub) for s0 in range(0, wa_ref.shape[1], sub)]

    def up_proj(slot):
        za_refs[slot][...] = _dot(hext_ref[...], wa_ref[...])
        zg_refs[slot][...] = _dot(hext_ref[...], wg_ref[...])

    def conv(z_ref, dw_ref, db_ref, cs, masked):
        z = z_ref[:, cs]
        rows_ext = tm + 2 * halo
        z_prev = pltpu.roll(z, 1, 0)[halo:halo + tm]
        z_next = pltpu.roll(z, rows_ext - 1, 0)[halo:halo + tm]
        if masked:
            z_prev = jnp.where(has_prev, z_prev, 0.0)
            z_next = jnp.where(has_next, z_next, 0.0)
        return z_prev * dw_ref[0:1, cs] + z[halo:halo + tm] * dw_ref[1:2, cs] + z_next * dw_ref[2:3, cs] + db_ref[:, cs]

    def step(up_slot, down_slot, masked=False):
        if up_slot is not None:
            up_proj(up_slot)
        if down_slot is not None:
            contrib = None
            for cs in chunks:
                a = conv(za_refs[down_slot], dwa_ref, dba_ref, cs, masked)
                g = conv(zg_refs[down_slot], dwg_ref, dbg_ref, cs, masked)
                t = _dot_t(wd_ref[:, cs], (g * _sigmoid(g) * a).astype(BF16))
                contrib = t if contrib is None else contrib + t
            acc_ref[...] += contrib

    @pl.when(c == 0)
    def _():
        starts_seq = is_lat & ((i * tm) % seq == 0)
        ends_seq = is_lat & (((i + 1) * tm) % seq == 0)
        zero_halo = jnp.zeros(hp_ref.shape, hp_ref.dtype)
        hext_ref[0:halo, :] = jnp.where(starts_seq, zero_halo, hp_ref[...])
        hext_ref[halo:halo + tm, :] = h_ref[...]
        hext_ref[halo + tm:, :] = jnp.where(ends_seq, zero_halo, hn_ref[...])
        acc_ref[...] = jnp.zeros_like(acc_ref)
        step(0, None)

    for masked in (False, True):
        for slot in (0, 1):
            @pl.when((c >= 1) & (c < nf) & (c % 2 == slot) & (is_lat != masked))
            def _():
                step(slot, 1 - slot, masked)

        @pl.when((c == nf) & (is_lat != masked))
        def _():
            step(None, (nf - 1) % 2, masked)

    @pl.when(c == nf)
    def _():
        o_ref[...] = _layer_norm(alpha * x_ref[...] + gate_ref[0] * acc_ref[...].T, lg_ref[...], lb_ref[...])


def _ffn(h2, x1, layer, w_up, dw_w, dw_b, w_down, mod3, mod_base, ln_g, ln_b, *, tm, n_lat_tiles, seq, ctx_len,
         nbatch, alpha):
    n_out, d = x1.shape
    depth, _, d_ff = w_down.shape
    tf = _pick((d_ff,), (512, 256, 128))
    nf = d_ff // tf
    halo = V7X_BF16_SUBLANES
    hb = tm // halo
    nhalo = h2.shape[0] // halo
    ntiles = n_out // tm
    assert dw_w.shape[1] == 3

    def cond_row(i):
        return jnp.where(i < n_lat_tiles, (i * tm) // seq, nbatch)

    dw_b3 = dw_b.reshape(depth, 1, 2 * d_ff)

    def up(c):
        return jnp.minimum(c, nf - 1)

    def down(c):
        return jnp.maximum(c - 1, 0)

    mid = nf // 2

    def h_tile(i, c):
        return jnp.where(c < mid, i, jnp.minimum(i + 1, ntiles - 1))

    def x_tile(i, c):
        return jnp.where(c < mid, jnp.maximum(i - 1, 0), i)

    return pl.pallas_call(
        functools.partial(_ffn_kernel, tm=tm, halo=halo, sub=min(tf, 256), nf=nf, seq=seq, ctx_len=ctx_len,
                          n_lat_tiles=n_lat_tiles, alpha=alpha),
        grid=(ntiles, nf + 1),
        in_specs=[
            pl.BlockSpec((tm, d), lambda i, c: (h_tile(i, c), 0)),
            pl.BlockSpec((halo, d), lambda i, c: (jnp.maximum(i * hb - 1, 0), 0)),
            pl.BlockSpec((halo, d), lambda i, c: (jnp.minimum((i + 1) * hb, nhalo - 1), 0)),
            pl.BlockSpec((None, d, tf), lambda i, c: (layer, 0, up(c))),
            pl.BlockSpec((None, d, tf), lambda i, c: (layer, 0, nf + up(c))),
            pl.BlockSpec((None, 3, tf), lambda i, c: (layer, 0, down(c))),
            pl.BlockSpec((None, 3, tf), lambda i, c: (layer, 0, nf + down(c))),
            pl.BlockSpec((None, 1, tf), lambda i, c: (layer, 0, down(c))),
            pl.BlockSpec((None, 1, tf), lambda i, c: (layer, 0, nf + down(c))),
            pl.BlockSpec((None, d, tf), lambda i, c: (layer, 0, down(c))),
            pl.BlockSpec((tm, d), lambda i, c: (x_tile(i, c), 0)),
            pl.BlockSpec((1, 1, d), lambda i, c: (mod_base + cond_row(i) * 6 + 5, 0, 0)),
            pl.BlockSpec((1, d), lambda i, c: (0, 0)),
            pl.BlockSpec((1, d), lambda i, c: (0, 0)),
        ],
        out_specs=pl.BlockSpec((tm, d), lambda i, c: (i, 0)),
        out_shape=jax.ShapeDtypeStruct((n_out, d), F32),
        scratch_shapes=([pltpu.VMEM((tm + 2 * halo, d), BF16)] + [pltpu.VMEM((tm + 2 * halo, tf), F32)] * 4
                        + [pltpu.VMEM((d, tm), F32)]),
        compiler_params=_params("arbitrary", "arbitrary"),
        name="conv_ffn",
    )(h2, h2, h2, w_up, w_up, dw_w, dw_w, dw_b3, dw_b3, w_down, x1, mod3, ln_g.reshape(1, d), ln_b.reshape(1, d))


def _inproj_c_kernel(x_ref, sh_ref, sc_ref, w_ref, o_ref, *, q_scale):
    hb = (x_ref[...] * (1.0 + sc_ref[0]) + sh_ref[0]).astype(BF16)
    d = x_ref.shape[1]
    cols = 512
    for c0 in range(0, o_ref.shape[1], cols):
        z = _dot(hb, w_ref[:, c0:c0 + cols])
        if c0 < d:
            z = z * q_scale
        o_ref[:, c0:c0 + cols] = z.astype(BF16)


def _inproj_c(xc, mod3, mod_base, w_in, *, tm, n_lat_tiles, seq, nbatch):
    n, d = xc.shape
    n_in = w_in.shape[1]
    assert d % 512 == 0 and n_in == 3 * d

    def cond_row(i):
        return jnp.where(i < n_lat_tiles, (i * tm) // seq, nbatch)

    return pl.pallas_call(
        functools.partial(_inproj_c_kernel, q_scale=ATTN_Q_SCALE),
        grid=(n // tm,),
        in_specs=[
            pl.BlockSpec((tm, d), lambda i: (i, 0)),
            pl.BlockSpec((1, 1, d), lambda i: (mod_base + cond_row(i) * 6 + 0, 0, 0)),
            pl.BlockSpec((1, 1, d), lambda i: (mod_base + cond_row(i) * 6 + 1, 0, 0)),
            pl.BlockSpec((d, n_in), lambda i: (0, 0), pipeline_mode=pl.Buffered(1)),
        ],
        out_specs=pl.BlockSpec((tm, n_in), lambda i: (i, 0)),
        out_shape=jax.ShapeDtypeStruct((n, n_in), BF16),
        compiler_params=_params("arbitrary"),
        name="inproj_c",
    )(xc, mod3, mod3, w_in)


def _na_bias_table(rel_bias):
    nheads, nrow, ncol = rel_bias.shape
    qc = np.arange(GRID_W)[:, None]
    kc = np.arange(GRID_W)[None, :]
    cstart = np.clip(qc - NA_COLS // 2, 0, GRID_W - NA_COLS)
    valid = (kc >= cstart) & (kc < cstart + NA_COLS)
    period = 2 * GRID_W - 1
    lead = GRID_W - NA_COLS - 1
    padded = jnp.pad(rel_bias, ((0, 0), (0, 0), (lead, period - lead - ncol)), constant_values=NEG)
    tiled = jnp.broadcast_to(padded[:, :, None, :], (nheads, nrow, GRID_W, period))
    skew = tiled.reshape(nheads, nrow, GRID_W * period)[:, :, :GRID_W * (period - 1)]
    toep = skew.reshape(nheads, nrow, GRID_W, period - 1)[:, :, :, GRID_W - 2:]
    toep = jnp.where(valid[None, None], toep * LOG2E, NEG)
    neg = jnp.full((nheads, 1, GRID_W, GRID_W), NEG, F32)
    ext = jnp.concatenate([neg, toep, neg], axis=1)
    return jnp.concatenate([ext[:, :2 * NA_ROWS], ext[:, 1:]], axis=-1)


def _natten_kernel(q_ref, k_ref, v_ref, kc_ref, vc_ref, bias_ref, o_ref, *, rows):
    nq = NA_QROWS * GRID_W
    nk = NA_KROWS * GRID_W
    left = lax.broadcasted_iota(jnp.int32, (GRID_W, 2 * GRID_W), 1) < GRID_W
    for blk in range(rows // NA_QROWS):
        ks = min(max(NA_QROWS * blk - NA_ROWS // 2, 0), rows - NA_KROWS)
        q = q_ref[blk * nq:(blk + 1) * nq, :]
        s_loc = _dot_t(q, k_ref[ks * GRID_W:ks * GRID_W + nk, :])
        bias_rows = []
        for qi in range(NA_QROWS):
            qr = NA_QROWS * blk + qi
            r0 = min(max(qr - NA_ROWS // 2, 0), rows - NA_ROWS)
            slabs = []
            for j in range(NA_KROWS // 2):
                kr = ks + 2 * j
                ok0 = r0 <= kr < r0 + NA_ROWS
                ok1 = r0 <= kr + 1 < r0 + NA_ROWS
                if not (ok0 or ok1):
                    slabs.append(jnp.full((GRID_W, 2 * GRID_W), NEG, F32))
                    continue
                slab = bias_ref[0, kr - qr + NA_ROWS]
                if not ok1:
                    slab = jnp.where(left, slab, NEG)
                elif not ok0:
                    slab = jnp.where(left, NEG, slab)
                slabs.append(slab)
            bias_rows.append(jnp.concatenate(slabs, axis=1))
        s_loc = s_loc + jnp.concatenate(bias_rows, axis=0)
        s_ctx = _dot_t(q, kc_ref[...])
        m = jnp.maximum(jnp.max(s_loc, axis=-1, keepdims=True), jnp.max(s_ctx, axis=-1, keepdims=True))
        p_loc = jnp.exp2(s_loc - m)
        p_ctx = jnp.exp2(s_ctx - m)
        l = jnp.sum(p_loc, axis=-1, keepdims=True) + jnp.sum(p_ctx, axis=-1, keepdims=True)
        o = (_dot(p_loc.astype(BF16), v_ref[ks * GRID_W:ks * GRID_W + nk, :])
             + _dot(p_ctx.astype(BF16), vc_ref[...]))
        o_ref[blk * nq:(blk + 1) * nq, :] = (o / l).astype(BF16)


def _natten(qkv, bias_tab, *, nbatch, seq, ctx_len, d):
    rows = seq // GRID_W
    nheads = d // HEAD_DIM
    assert rows % NA_QROWS == 0 and rows >= NA_KROWS
    ctx_blk0 = nbatch * seq // ctx_len
    return pl.pallas_call(
        functools.partial(_natten_kernel, rows=rows),
        grid=(nheads, nbatch),
        in_specs=[
            pl.BlockSpec((seq, HEAD_DIM), lambda h, b: (b, h)),
            pl.BlockSpec((seq, HEAD_DIM), lambda h, b: (b, nheads + h)),
            pl.BlockSpec((seq, HEAD_DIM), lambda h, b: (b, 2 * nheads + h)),
            pl.BlockSpec((ctx_len, HEAD_DIM), lambda h, b: (ctx_blk0 + b, nheads + h)),
            pl.BlockSpec((ctx_len, HEAD_DIM), lambda h, b: (ctx_blk0 + b, 2 * nheads + h)),
            pl.BlockSpec((1,) + bias_tab.shape[1:], lambda h, b: (h, 0, 0, 0)),
        ],
        out_specs=pl.BlockSpec((seq, HEAD_DIM), lambda h, b: (b, h)),
        out_shape=jax.ShapeDtypeStruct((nbatch * seq, d), BF16),
        compiler_params=_params("arbitrary", "arbitrary"),
        name="natten",
    )(qkv, qkv, qkv, qkv, qkv, bias_tab)


def kernel(x, c, ctx, c_ctx, ada_w, ada_b, post_ln_g, post_ln_b, ab_w_in, ab_w_out, ab_q_gain, ab_k_gain,
           ab_dw_w, ab_dw_b, ab_norm_g, ab_norm_b, c_w_in, c_w_out, c_rel_bias, ffn_w_up, ffn_dw_w, ffn_dw_b,
           ffn_w_down):
    nbatch, seq, d = x.shape
    ctx_len = ctx.shape[1]
    depth = ada_w.shape[0]
    assert depth == 2 and nbatch + 1 <= COND_ROWS and nbatch * ctx_len <= seq
    assert seq % GRID_W == 0 and seq & (seq - 1) == 0 and ctx_len & (ctx_len - 1) == 0 and ctx_len <= seq
    nx, nc = nbatch * seq, nbatch * ctx_len
    alpha = (2 * depth) ** 0.25
    tm = _pick((seq, nc), (512, 256))
    n_lat_tiles = nx // tm
    geo = dict(tm=tm, n_lat_tiles=n_lat_tiles, seq=seq, nbatch=nbatch)

    cond = jnp.concatenate([c, c_ctx[None], jnp.zeros((COND_ROWS - nbatch - 1, d), F32)], axis=0)
    mod3 = _ada(cond, ada_w, ada_b).reshape(depth * COND_ROWS * 6, 1, d)
    x2d, c2d = x.reshape(nx, d), ctx.reshape(nc, d)
    ffn_w = (ffn_w_up.astype(BF16), ffn_dw_w, ffn_dw_b, jnp.swapaxes(ffn_w_down, 1, 2).astype(BF16))

    mod_base = 0
    qkv, glu = _inproj_ab(x2d, c2d, mod3, mod_base, ab_w_in[0].astype(BF16), ab_q_gain[0][None],
                          ab_k_gain[0][None], _rope_tables(seq, tm), **geo)
    attn = _gqa(qkv, nbatch=nbatch, seq=seq, ctx_len=ctx_len)
    conv = _conformer_conv(glu, ab_dw_w[0], ab_dw_b[0], ab_norm_g[0], ab_norm_b[0],
                           nbatch=nbatch, seq=seq, ctx_len=ctx_len)
    x1, h2 = _outproj([attn, conv], ab_w_out[0].astype(BF16), [x2d, c2d], mod3, mod_base, post_ln_g[0, 0],
                      post_ln_b[0, 0], n_out=nx + nc, alpha=alpha, **geo)
    xc = _ffn(h2, x1, 0, *ffn_w, mod3, mod_base, post_ln_g[0, 1], post_ln_b[0, 1], ctx_len=ctx_len, alpha=alpha,
              **geo)

    mod_base = COND_ROWS * 6
    qkv = _inproj_c(xc, mod3, mod_base, c_w_in[0].astype(BF16), **geo)
    attn = _natten(qkv, _na_bias_table(c_rel_bias[0]), nbatch=nbatch, seq=seq, ctx_len=ctx_len, d=d)
    x1, h2 = _outproj([attn], c_w_out[0].astype(BF16), [xc], mod3, mod_base, post_ln_g[1, 0], post_ln_b[1, 0],
                      n_out=nx, alpha=alpha, **geo)
    out = _ffn(h2, x1, 1, *ffn_w, mod3, mod_base, post_ln_g[1, 1], post_ln_b[1, 1], ctx_len=ctx_len, alpha=alpha,
               **geo)
    return out.reshape(nbatch, seq, d)
```

```python
import functools

import jax
import jax.numpy as jnp
import numpy as np
from jax import lax
from jax.experimental import pallas as pl
from jax.experimental.pallas import tpu as pltpu

F32 = jnp.float32
BF16 = jnp.bfloat16

HEAD_DIM = 128
GRID_W = 64
A_HEADS = 8
A_KV_HEADS = 2
A_GROUP = A_HEADS // A_KV_HEADS
ROPE_THETA = 10000.0
NA_ROWS = 8
NA_COLS = 16
LN_EPS = 1e-6
NEG = -1e30
LOG2E = 1.4426950408889634
ATTN_Q_SCALE = HEAD_DIM ** -0.5 * LOG2E

V7X_LANES = 128
V7X_F32_SUBLANES = 8
V7X_BF16_SUBLANES = 16
V7X_MXU_DIM = 256
V7X_VMEM_LIMIT = 56 * 1024 * 1024
PROJ_COLS = 2 * V7X_MXU_DIM
COND_ROWS = 8

NA_QROWS = 4
NA_HEADS_PER_STEP = 4
NA_KROWS = NA_QROWS + NA_ROWS


def _pick(n, candidates):
    for t in candidates:
        if all(v % t == 0 for v in n):
            return t
    raise ValueError(f"no tile in {candidates} divides {n}")


def _params(*sem):
    return pltpu.CompilerParams(dimension_semantics=sem, vmem_limit_bytes=V7X_VMEM_LIMIT)


def _layer_norm(r, g, b):
    mu = jnp.mean(r, axis=-1, keepdims=True)
    d = r - mu
    var = jnp.mean(d * d, axis=-1, keepdims=True)
    return d * lax.rsqrt(var + LN_EPS) * g + b


def _sigmoid(v):
    return 1.0 / (1.0 + jnp.exp(-v))


def _dot(a, b):
    return jnp.dot(a, b, preferred_element_type=F32)


def _dot_t(a, b):
    return lax.dot_general(a, b, (((1,), (1,)), ((), ())), preferred_element_type=F32)


def _ada_kernel(cond_ref, w_ref, b_ref, o_ref):
    cnd = cond_ref[...]
    s = (cnd * _sigmoid(cnd)).astype(BF16)
    o_ref[0] = _dot(s, w_ref[0].astype(BF16)) + b_ref[0]


def _ada(cond, ada_w, ada_b):
    depth, d, n6 = ada_w.shape
    tn = _pick((n6,), (1024, 512, 256, 128))
    return pl.pallas_call(
        _ada_kernel,
        grid=(depth, n6 // tn),
        in_specs=[
            pl.BlockSpec((COND_ROWS, d), lambda l, j: (0, 0)),
            pl.BlockSpec((1, d, tn), lambda l, j: (l, 0, j)),
            pl.BlockSpec((1, 1, tn), lambda l, j: (l, 0, j)),
        ],
        out_specs=pl.BlockSpec((1, COND_ROWS, tn), lambda l, j: (l, 0, j)),
        out_shape=jax.ShapeDtypeStruct((depth, COND_ROWS, n6), F32),
        compiler_params=_params("arbitrary", "arbitrary"),
        name="ada_mod",
    )(cond, ada_w, ada_b.reshape(depth, 1, n6))


def _inproj_ab_kernel(x_ref, c_ref, sh_ref, sc_ref, w_ref, qg_ref, kg_ref, rope_ref, qkv_ref, glu_ref, hb_ref, *,
                      q_scale, n_lat_tiles):
    is_lat = pl.program_id(0) < n_lat_tiles
    for src_ref, cond in ((x_ref, is_lat), (c_ref, jnp.logical_not(is_lat))):
        @pl.when(cond)
        def _():
            hb_ref[...] = (src_ref[...] * (1.0 + sc_ref[0]) + sh_ref[0]).astype(BF16)
    hb = hb_ref[...]
    cos, sin_lo, sin_hi = rope_ref[0], rope_ref[1], rope_ref[2]
    a_w = A_HEADS * HEAD_DIM
    kv_w = A_KV_HEADS * HEAD_DIM
    b_w = glu_ref.shape[1]

    def norm_rope(z, gain):
        zn = z * lax.rsqrt(jnp.mean(z * z, axis=-1, keepdims=True) + LN_EPS) * gain
        quarter = HEAD_DIM // 4
        return (zn * cos + pltpu.roll(zn, HEAD_DIM - quarter, 1) * sin_lo
                + pltpu.roll(zn, quarter, 1) * sin_hi)

    q_gain = qg_ref[...] * q_scale
    heads_per_dot = PROJ_COLS // HEAD_DIM
    for c0 in range(0, a_w, heads_per_dot * HEAD_DIM):
        z = _dot(hb, w_ref[:, c0:c0 + heads_per_dot * HEAD_DIM])
        for hh in range(heads_per_dot):
            cs = slice(c0 + hh * HEAD_DIM, c0 + (hh + 1) * HEAD_DIM)
            qkv_ref[:, cs] = norm_rope(z[:, hh * HEAD_DIM:(hh + 1) * HEAD_DIM], q_gain).astype(BF16)
    z = _dot(hb, w_ref[:, a_w:a_w + 2 * kv_w])
    for hh in range(A_KV_HEADS):
        cs = slice(a_w + hh * HEAD_DIM, a_w + (hh + 1) * HEAD_DIM)
        qkv_ref[:, cs] = norm_rope(z[:, hh * HEAD_DIM:(hh + 1) * HEAD_DIM], kg_ref[...]).astype(BF16)
    qkv_ref[:, a_w + kv_w:a_w + 2 * kv_w] = z[:, kv_w:].astype(BF16)
    u0 = a_w + 2 * kv_w
    glu_cols = PROJ_COLS if b_w % PROJ_COLS == 0 else b_w
    for c0 in range(0, b_w, glu_cols):
        u = _dot(hb, w_ref[:, u0 + c0:u0 + c0 + glu_cols])
        g = _dot(hb, w_ref[:, u0 + b_w + c0:u0 + b_w + c0 + glu_cols])
        glu_ref[:, c0:c0 + glu_cols] = u * _sigmoid(g)


def _inproj_ab(x2d, c2d, mod3, mod_base, w_in, q_gain, k_gain, rope, *, tm, n_lat_tiles, seq, nbatch):
    d = x2d.shape[1]
    n = x2d.shape[0] + c2d.shape[0]
    n_in = w_in.shape[1]
    a_w, kv_w = A_HEADS * HEAD_DIM, A_KV_HEADS * HEAD_DIM
    b_w = (n_in - a_w - 2 * kv_w) // 2
    tiles_per_seq = seq // tm

    def cond_row(i):
        return jnp.where(i < n_lat_tiles, (i * tm) // seq, nbatch)

    def rope_blk(i):
        return jnp.where(i < n_lat_tiles, i % tiles_per_seq, tiles_per_seq)

    return pl.pallas_call(
        functools.partial(_inproj_ab_kernel, q_scale=ATTN_Q_SCALE, n_lat_tiles=n_lat_tiles),
        grid=(n // tm,),
        in_specs=[
            pl.BlockSpec((tm, d), lambda i: (jnp.minimum(i, n_lat_tiles - 1), 0)),
            pl.BlockSpec((tm, d), lambda i: (jnp.maximum(i - n_lat_tiles, 0), 0)),
            pl.BlockSpec((1, 1, d), lambda i: (mod_base + cond_row(i) * 6 + 0, 0, 0)),
            pl.BlockSpec((1, 1, d), lambda i: (mod_base + cond_row(i) * 6 + 1, 0, 0)),
            pl.BlockSpec((d, n_in), lambda i: (0, 0), pipeline_mode=pl.Buffered(1)),
            pl.BlockSpec((1, HEAD_DIM), lambda i: (0, 0)),
            pl.BlockSpec((1, HEAD_DIM), lambda i: (0, 0)),
            pl.BlockSpec((3, tm, HEAD_DIM), lambda i: (0, rope_blk(i), 0)),
        ],
        out_specs=[
            pl.BlockSpec((tm, a_w + 2 * kv_w), lambda i: (i, 0)),
            pl.BlockSpec((tm, b_w), lambda i: (i, 0)),
        ],
        out_shape=[
            jax.ShapeDtypeStruct((n, a_w + 2 * kv_w), BF16),
            jax.ShapeDtypeStruct((n, b_w), F32),
        ],
        scratch_shapes=[pltpu.VMEM((tm, d), BF16)],
        compiler_params=_params("arbitrary"),
        name="inproj_ab",
    )(x2d, c2d, mod3, mod3, w_in, q_gain, k_gain, rope)


def _rope_tables(seq, tm):
    t = jnp.arange(seq)
    nfreq = HEAD_DIM // 4
    inv = ROPE_THETA ** (-jnp.arange(nfreq, dtype=F32) / nfreq)
    ang_r = (t // GRID_W).astype(F32)[:, None] * inv
    ang_c = (t % GRID_W).astype(F32)[:, None] * inv
    ang = jnp.concatenate([ang_r, ang_r, ang_c, ang_c], axis=-1)
    cos, sin = jnp.cos(ang), jnp.sin(ang)
    low = (np.arange(HEAD_DIM) // nfreq) % 2 == 0
    sin_lo = jnp.where(low, -sin, 0.0)
    sin_hi = jnp.where(low, 0.0, sin)
    tab = jnp.stack([cos, sin_lo, sin_hi])
    ident = jnp.stack([jnp.ones((tm, HEAD_DIM), F32), jnp.zeros((tm, HEAD_DIM), F32), jnp.zeros((tm, HEAD_DIM), F32)])
    return jnp.concatenate([tab, ident], axis=1)


def _scores(q, parts):
    return [_dot_t(q, k_ref[...]) for k_ref, _ in parts]


def _softmax_pv(ss, parts):
    m = functools.reduce(jnp.maximum, [jnp.max(s, axis=-1, keepdims=True) for s in ss])
    ps = [jnp.exp2(s - m) for s in ss]
    l = functools.reduce(jnp.add, [jnp.sum(p, axis=-1, keepdims=True) for p in ps])
    o = functools.reduce(jnp.add, [_dot(p.astype(BF16), v_ref[...]) for p, (_, v_ref) in zip(ps, parts)])
    return o / l


def _gqa_kernel(q_ref, kx_ref, vx_ref, kc_ref, vc_ref, o_ref, *, n_lat_tiles):
    qi = pl.program_id(1)

    def run(parts):
        heads = [slice(g * HEAD_DIM, (g + 1) * HEAD_DIM) for g in range(A_GROUP)]
        ss = _scores(q_ref[:, heads[0]], parts)
        for g, cs in enumerate(heads):
            ss_next = _scores(q_ref[:, heads[g + 1]], parts) if g + 1 < A_GROUP else None
            o_ref[:, cs] = _softmax_pv(ss, parts).astype(BF16)
            ss = ss_next

    @pl.when(qi < n_lat_tiles)
    def _():
        run([(kx_ref, vx_ref), (kc_ref, vc_ref)])

    @pl.when(qi >= n_lat_tiles)
    def _():
        run([(kc_ref, vc_ref)])


def _gqa(qkv, *, nbatch, seq, ctx_len):
    n = qkv.shape[0]
    tq = _pick((seq, ctx_len), (256, 128))
    n_lat_tiles = nbatch * seq // tq
    lat_per_seq, ctx_per_seq = seq // tq, ctx_len // tq
    a_w = A_HEADS * HEAD_DIM
    gw = A_GROUP * HEAD_DIM
    k_col0 = a_w // HEAD_DIM
    v_col0 = k_col0 + A_KV_HEADS

    def batch(qi):
        return jnp.where(qi < n_lat_tiles, qi // lat_per_seq, (qi - n_lat_tiles) // ctx_per_seq)

    def lat_batch(qi):
        return jnp.minimum(qi // lat_per_seq, nbatch - 1)

    ctx_blk0 = nbatch * seq // ctx_len
    return pl.pallas_call(
        functools.partial(_gqa_kernel, n_lat_tiles=n_lat_tiles),
        grid=(A_KV_HEADS, n // tq),
        in_specs=[
            pl.BlockSpec((tq, gw), lambda h, qi: (qi, h)),
            pl.BlockSpec((seq, HEAD_DIM), lambda h, qi: (lat_batch(qi), k_col0 + h)),
            pl.BlockSpec((seq, HEAD_DIM), lambda h, qi: (lat_batch(qi), v_col0 + h)),
            pl.BlockSpec((ctx_len, HEAD_DIM), lambda h, qi: (ctx_blk0 + batch(qi), k_col0 + h)),
            pl.BlockSpec((ctx_len, HEAD_DIM), lambda h, qi: (ctx_blk0 + batch(qi), v_col0 + h)),
        ],
        out_specs=pl.BlockSpec((tq, gw), lambda h, qi: (qi, h)),
        out_shape=jax.ShapeDtypeStruct((n, a_w), BF16),
        compiler_params=_params("arbitrary", "arbitrary"),
        name="gqa_attn",
    )(qkv, qkv, qkv, qkv, qkv)


def _conv_kernel(x_ref, xp_ref, xn_ref, dw_ref, db_ref, g_ref, b_ref, o_ref, buf_ref, y_ref, shift_ref, *,
                 tt, halo, n_lat_tiles, lat_per_seq, ctx_per_seq):
    i = pl.program_id(0)
    taps = dw_ref.shape[0]
    pad = taps // 2
    idx = jnp.where(i < n_lat_tiles, i % lat_per_seq, (i - n_lat_tiles) % ctx_per_seq)
    per_seq = jnp.where(i < n_lat_tiles, lat_per_seq, ctx_per_seq)
    buf_ref[0:halo, :] = jnp.where(idx > 0, xp_ref[...], 0.0)
    buf_ref[halo:halo + tt, :] = x_ref[...]
    buf_ref[halo + tt:, :] = jnp.where(idx < per_seq - 1, xn_ref[...], 0.0)
    nchunk = x_ref.shape[1] // V7X_LANES

    rows = tt + 2 * halo

    def chunk(c, carry):
        cs = pl.ds(pl.multiple_of(c * V7X_LANES, V7X_LANES), V7X_LANES)
        xb = buf_ref[:, cs]
        for s in range(1, V7X_F32_SUBLANES):
            shift_ref[s] = pltpu.roll(xb, rows - s, 0)
        acc = jnp.broadcast_to(db_ref[:, cs], (tt, V7X_LANES))
        for k in range(taps):
            off = halo - pad + k
            base, s = off - off % V7X_F32_SUBLANES, off % V7X_F32_SUBLANES
            win = buf_ref[pl.ds(base, tt), cs] if s == 0 else shift_ref[s, base:base + tt, :]
            acc = acc + win * dw_ref[pl.ds(k, 1), cs]
        y_ref[:, cs] = acc
        return carry

    lax.fori_loop(0, nchunk, chunk, 0)
    yn = _layer_norm(y_ref[...], g_ref[...], b_ref[...])
    o_ref[...] = (yn * _sigmoid(yn)).astype(BF16)


def _conformer_conv(glu, dw_w, dw_b, n_g, n_b, *, nbatch, seq, ctx_len):
    n, cw = glu.shape
    tt = _pick((seq, ctx_len), (256, 128))
    halo = pl.cdiv(dw_w.shape[0] // 2, V7X_F32_SUBLANES) * V7X_F32_SUBLANES
    n_lat_tiles = nbatch * seq // tt
    hb = tt // halo
    nhalo = n // halo
    return pl.pallas_call(
        functools.partial(_conv_kernel, tt=tt, halo=halo, n_lat_tiles=n_lat_tiles,
                          lat_per_seq=seq // tt, ctx_per_seq=ctx_len // tt),
        grid=(n // tt,),
        in_specs=[
            pl.BlockSpec((tt, cw), lambda i: (i, 0)),
            pl.BlockSpec((halo, cw), lambda i: (jnp.maximum(i * hb - 1, 0), 0)),
            pl.BlockSpec((halo, cw), lambda i: (jnp.minimum((i + 1) * hb, nhalo - 1), 0)),
            pl.BlockSpec(dw_w.shape, lambda i: (0, 0)),
            pl.BlockSpec((1, cw), lambda i: (0, 0)),
            pl.BlockSpec((1, cw), lambda i: (0, 0)),
            pl.BlockSpec((1, cw), lambda i: (0, 0)),
        ],
        out_specs=pl.BlockSpec((tt, cw), lambda i: (i, 0)),
        out_shape=jax.ShapeDtypeStruct((n, cw), BF16),
        scratch_shapes=[pltpu.VMEM((tt + 2 * halo, cw), F32), pltpu.VMEM((tt, cw), F32),
                        pltpu.VMEM((V7X_F32_SUBLANES, tt + 2 * halo, V7X_LANES), F32)],
        compiler_params=_params("arbitrary"),
        name="conformer_conv",
    )(glu, glu, glu, dw_w, dw_b.reshape(1, cw), n_g.reshape(1, cw), n_b.reshape(1, cw))


def _outproj_kernel(*refs, n_parts, n_res, n_lat_tiles, alpha):
    part_refs = refs[:n_parts]
    res_refs = refs[n_parts + 1:n_parts + 1 + n_res]
    w_ref = refs[n_parts]
    gate_ref, sh_ref, sc_ref, lg_ref, lb_ref, x1_ref, h2_ref = refs[n_parts + 1 + n_res:]
    tm = x1_ref.shape[0]
    rb = tm // 2 if tm % (2 * V7X_BF16_SUBLANES) == 0 else tm

    def proj(r0):
        y = None
        k0 = 0
        for p_ref in part_refs:
            kw = p_ref.shape[1]
            t = _dot(p_ref[r0:r0 + rb, :], w_ref[k0:k0 + kw, :])
            y = t if y is None else y + t
            k0 += kw
        return y

    def finish(x_ref, r0, y):
        rs = slice(r0, r0 + rb)
        x1 = _layer_norm(alpha * x_ref[rs, :] + gate_ref[0] * y, lg_ref[...], lb_ref[...])
        x1_ref[rs, :] = x1
        h2_ref[rs, :] = (x1 * (1.0 + sc_ref[0]) + sh_ref[0]).astype(BF16)

    def body(x_ref):
        pending = proj(0)
        for r0 in range(0, tm, rb):
            upcoming = proj(r0 + rb) if r0 + rb < tm else None
            finish(x_ref, r0, pending)
            pending = upcoming

    if n_res == 1:
        body(res_refs[0])
    else:
        is_lat = pl.program_id(0) < n_lat_tiles
        pl.when(is_lat)(functools.partial(body, res_refs[0]))
        pl.when(jnp.logical_not(is_lat))(functools.partial(body, res_refs[1]))


def _outproj(parts, w_out, residuals, mod3, mod_base, ln_g, ln_b, *, n_out, tm, n_lat_tiles, seq, nbatch, alpha):
    d = residuals[0].shape[1]
    if len(residuals) == 1:
        res_specs = [pl.BlockSpec((tm, d), lambda i: (i, 0))]
    else:
        res_specs = [pl.BlockSpec((tm, d), lambda i: (jnp.minimum(i, n_lat_tiles - 1), 0)),
                     pl.BlockSpec((tm, d), lambda i: (jnp.maximum(i - n_lat_tiles, 0), 0))]

    def cond_row(i):
        return jnp.where(i < n_lat_tiles, (i * tm) // seq, nbatch)

    def mod_spec(k):
        return pl.BlockSpec((1, 1, d), lambda i: (mod_base + cond_row(i) * 6 + k, 0, 0))

    return pl.pallas_call(
        functools.partial(_outproj_kernel, n_parts=len(parts), n_res=len(residuals), n_lat_tiles=n_lat_tiles,
                          alpha=alpha),
        grid=(n_out // tm,),
        in_specs=[pl.BlockSpec((tm, p.shape[1]), lambda i: (i, 0)) for p in parts] + [
            pl.BlockSpec(w_out.shape, lambda i: (0, 0), pipeline_mode=pl.Buffered(1))] + res_specs + [
            mod_spec(2), mod_spec(3), mod_spec(4),
            pl.BlockSpec((1, d), lambda i: (0, 0)),
            pl.BlockSpec((1, d), lambda i: (0, 0)),
        ],
        out_specs=[pl.BlockSpec((tm, d), lambda i: (i, 0)), pl.BlockSpec((tm, d), lambda i: (i, 0))],
        out_shape=[jax.ShapeDtypeStruct((n_out, d), F32), jax.ShapeDtypeStruct((n_out, d), BF16)],
        compiler_params=_params("arbitrary"),
        name="outproj_ln",
    )(*parts, w_out, *residuals, mod3, mod3, mod3, ln_g.reshape(1, d), ln_b.reshape(1, d))


def _ffn_kernel(h_ref, hp_ref, hn_ref, wa_ref, wg_ref, dwa_ref, dwg_ref, dba_ref, dbg_ref, wd_ref,
                x_ref, gate_ref, lg_ref, lb_ref, o_ref,
                hext_ref, za0_ref, za1_ref, zg0_ref, zg1_ref, acc_ref, *,
                tm, halo, sub, nf, seq, ctx_len, n_lat_tiles, alpha):
    za_refs = (za0_ref, za1_ref)
    zg_refs = (zg0_ref, zg1_ref)
    i = pl.program_id(0)
    c = pl.program_id(1)
    is_lat = i < n_lat_tiles
    row = lax.broadcasted_iota(jnp.int32, (tm, 1), 0) + i * tm
    pos = row & (ctx_len - 1)
    has_prev = pos != 0
    has_next = pos != ctx_len - 1

    chunks = [slice(s0, s0 + sub) for s0 in range(0, wa_ref.shape[1], sub)]

    def up_proj(slot):
        za_refs[slot][...] = _dot(hext_ref[...], wa_ref[...])
        zg_refs[slot][...] = _dot(hext_ref[...], wg_ref[...])

    def conv(z_ref, dw_ref, db_ref, cs, masked):
        z = z_ref[:, cs]
        rows_ext = tm + 2 * halo
        z_prev = pltpu.roll(z, 1, 0)[halo:halo + tm]
        z_next = pltpu.roll(z, rows_ext - 1, 0)[halo:halo + tm]
        if masked:
            z_prev = jnp.where(has_prev, z_prev, 0.0)
            z_next = jnp.where(has_next, z_next, 0.0)
        return z_prev * dw_ref[0:1, cs] + z[halo:halo + tm] * dw_ref[1:2, cs] + z_next * dw_ref[2:3, cs] + db_ref[:, cs]

    def step(up_slot, down_slot, masked=False):
        if up_slot is not None:
            up_proj(up_slot)
        if down_slot is not None:
            contrib = None
            for cs in chunks:
                a = conv(za_refs[down_slot], dwa_ref, dba_ref, cs, masked)
                g = conv(zg_refs[down_slot], dwg_ref, dbg_ref, cs, masked)
                t = _dot_t(wd_ref[:, cs], (g * _sigmoid(g) * a).astype(BF16))
                contrib = t if contrib is None else contrib + t
            acc_ref[...] += contrib

    @pl.when(c == 0)
    def _():
        starts_seq = is_lat & ((i * tm) % seq == 0)
        ends_seq = is_lat & (((i + 1) * tm) % seq == 0)
        zero_halo = jnp.zeros(hp_ref.shape, hp_ref.dtype)
        hext_ref[0:halo, :] = jnp.where(starts_seq, zero_halo, hp_ref[...])
        hext_ref[halo:halo + tm, :] = h_ref[...]
        hext_ref[halo + tm:, :] = jnp.where(ends_seq, zero_halo, hn_ref[...])
        acc_ref[...] = jnp.zeros_like(acc_ref)
        step(0, None)

    for masked in (False, True):
        for slot in (0, 1):
            @pl.when((c >= 1) & (c < nf) & (c % 2 == slot) & (is_lat != masked))
            def _():
                step(slot, 1 - slot, masked)

        @pl.when((c == nf) & (is_lat != masked))
        def _():
            step(None, (nf - 1) % 2, masked)

    @pl.when(c == nf)
    def _():
        o_ref[...] = _layer_norm(alpha * x_ref[...] + gate_ref[0] * acc_ref[...].T, lg_ref[...], lb_ref[...])


def _ffn(h2, x1, layer, w_up, dw_w, dw_b, w_down, mod3, mod_base, ln_g, ln_b, *, tm, n_lat_tiles, seq, ctx_len,
         nbatch, alpha):
    n_out, d = x1.shape
    depth, _, d_ff = w_down.shape
    tf = _pick((d_ff,), (512, 256, 128))
    nf = d_ff // tf
    halo = V7X_BF16_SUBLANES
    hb = tm // halo
    nhalo = h2.shape[0] // halo
    ntiles = n_out // tm
    assert dw_w.shape[1] == 3

    def cond_row(i):
        return jnp.where(i < n_lat_tiles, (i * tm) // seq, nbatch)

    dw_b3 = dw_b.reshape(depth, 1, 2 * d_ff)

    def up(c):
        return jnp.minimum(c, nf - 1)

    def down(c):
        return jnp.maximum(c - 1, 0)

    return pl.pallas_call(
        functools.partial(_ffn_kernel, tm=tm, halo=halo, sub=min(tf, V7X_MXU_DIM), nf=nf, seq=seq, ctx_len=ctx_len,
                          n_lat_tiles=n_lat_tiles, alpha=alpha),
        grid=(ntiles, nf + 1),
        in_specs=[
            pl.BlockSpec((tm, d), lambda i, c: (i, 0)),
            pl.BlockSpec((halo, d), lambda i, c: (jnp.maximum(i * hb - 1, 0), 0)),
            pl.BlockSpec((halo, d), lambda i, c: (jnp.minimum((i + 1) * hb, nhalo - 1), 0)),
            pl.BlockSpec((None, d, tf), lambda i, c: (layer, 0, up(c))),
            pl.BlockSpec((None, d, tf), lambda i, c: (layer, 0, nf + up(c))),
            pl.BlockSpec((None, 3, tf), lambda i, c: (layer, 0, down(c))),
            pl.BlockSpec((None, 3, tf), lambda i, c: (layer, 0, nf + down(c))),
            pl.BlockSpec((None, 1, tf), lambda i, c: (layer, 0, down(c))),
            pl.BlockSpec((None, 1, tf), lambda i, c: (layer, 0, nf + down(c))),
            pl.BlockSpec((None, d, tf), lambda i, c: (layer, 0, down(c))),
            pl.BlockSpec((tm, d), lambda i, c: (i, 0)),
            pl.BlockSpec((1, 1, d), lambda i, c: (mod_base + cond_row(i) * 6 + 5, 0, 0)),
            pl.BlockSpec((1, d), lambda i, c: (0, 0)),
            pl.BlockSpec((1, d), lambda i, c: (0, 0)),
        ],
        out_specs=pl.BlockSpec((tm, d), lambda i, c: (i, 0)),
        out_shape=jax.ShapeDtypeStruct((n_out, d), F32),
        scratch_shapes=([pltpu.VMEM((tm + 2 * halo, d), BF16)] + [pltpu.VMEM((tm + 2 * halo, tf), F32)] * 4
                        + [pltpu.VMEM((d, tm), F32)]),
        compiler_params=_params("arbitrary", "arbitrary"),
        name="conv_ffn",
    )(h2, h2, h2, w_up, w_up, dw_w, dw_w, dw_b3, dw_b3, w_down, x1, mod3, ln_g.reshape(1, d), ln_b.reshape(1, d))


def _inproj_c_kernel(x_ref, sh_ref, sc_ref, w_ref, o_ref, *, q_scale):
    hb = (x_ref[...] * (1.0 + sc_ref[0]) + sh_ref[0]).astype(BF16)
    d = x_ref.shape[1]
    cols = PROJ_COLS
    for c0 in range(0, o_ref.shape[1], cols):
        z = _dot(hb, w_ref[:, c0:c0 + cols])
        if c0 < d:
            z = z * q_scale
        o_ref[:, c0:c0 + cols] = z.astype(BF16)


def _inproj_c(xc, mod3, mod_base, w_in, *, tm, n_lat_tiles, seq, nbatch):
    n, d = xc.shape
    n_in = w_in.shape[1]
    assert d % PROJ_COLS == 0 and n_in == 3 * d

    def cond_row(i):
        return jnp.where(i < n_lat_tiles, (i * tm) // seq, nbatch)

    return pl.pallas_call(
        functools.partial(_inproj_c_kernel, q_scale=ATTN_Q_SCALE),
        grid=(n // tm,),
        in_specs=[
            pl.BlockSpec((tm, d), lambda i: (i, 0)),
            pl.BlockSpec((1, 1, d), lambda i: (mod_base + cond_row(i) * 6 + 0, 0, 0)),
            pl.BlockSpec((1, 1, d), lambda i: (mod_base + cond_row(i) * 6 + 1, 0, 0)),
            pl.BlockSpec((d, n_in), lambda i: (0, 0), pipeline_mode=pl.Buffered(1)),
        ],
        out_specs=pl.BlockSpec((tm, n_in), lambda i: (i, 0)),
        out_shape=jax.ShapeDtypeStruct((n, n_in), BF16),
        compiler_params=_params("arbitrary"),
        name="inproj_c",
    )(xc, mod3, mod3, w_in)


def _na_bias_table(rel_bias):
    nheads, nrow, ncol = rel_bias.shape
    qc = np.arange(GRID_W)[:, None]
    kc = np.arange(GRID_W)[None, :]
    cstart = np.clip(qc - NA_COLS // 2, 0, GRID_W - NA_COLS)
    valid = (kc >= cstart) & (kc < cstart + NA_COLS)
    period = 2 * GRID_W - 1
    lead = GRID_W - NA_COLS - 1
    padded = jnp.pad(rel_bias, ((0, 0), (0, 0), (lead, period - lead - ncol)), constant_values=NEG)
    tiled = jnp.broadcast_to(padded[:, :, None, :], (nheads, nrow, GRID_W, period))
    skew = tiled.reshape(nheads, nrow, GRID_W * period)[:, :, :GRID_W * (period - 1)]
    toep = skew.reshape(nheads, nrow, GRID_W, period - 1)[:, :, :, GRID_W - 2:]
    toep = jnp.where(valid[None, None], toep * LOG2E, NEG)
    neg = jnp.full((nheads, 1, GRID_W, GRID_W), NEG, F32)
    ext = jnp.concatenate([neg, toep, neg], axis=1)
    return jnp.concatenate([ext[:, :2 * NA_ROWS], ext[:, 1:]], axis=-1)


def _natten_kernel(q_ref, k_ref, v_ref, kc_ref, vc_ref, bias_ref, o_ref, *, rows):
    nq = NA_QROWS * GRID_W
    nk = NA_KROWS * GRID_W
    left = lax.broadcasted_iota(jnp.int32, (GRID_W, 2 * GRID_W), 1) < GRID_W
    for blk, hh in [(b_, h_) for b_ in range(rows // NA_QROWS) for h_ in range(bias_ref.shape[0])]:
        hs = slice(hh * HEAD_DIM, (hh + 1) * HEAD_DIM)
        ks = min(max(NA_QROWS * blk - NA_ROWS // 2, 0), rows - NA_KROWS)
        q = q_ref[blk * nq:(blk + 1) * nq, hs]
        s_loc = _dot_t(q, k_ref[ks * GRID_W:ks * GRID_W + nk, hs])
        bias_rows = []
        for qi in range(NA_QROWS):
            qr = NA_QROWS * blk + qi
            r0 = min(max(qr - NA_ROWS // 2, 0), rows - NA_ROWS)
            slabs = []
            for j in range(NA_KROWS // 2):
                kr = ks + 2 * j
                ok0 = r0 <= kr < r0 + NA_ROWS
                ok1 = r0 <= kr + 1 < r0 + NA_ROWS
                if not (ok0 or ok1):
                    slabs.append(jnp.full((GRID_W, 2 * GRID_W), NEG, F32))
                    continue
                slab = bias_ref[hh, kr - qr + NA_ROWS]
                if not ok1:
                    slab = jnp.where(left, slab, NEG)
                elif not ok0:
                    slab = jnp.where(left, NEG, slab)
                slabs.append(slab)
            bias_rows.append(jnp.concatenate(slabs, axis=1))
        s_loc = s_loc + jnp.concatenate(bias_rows, axis=0)
        s_ctx = _dot_t(q, kc_ref[:, hs])
        m = jnp.maximum(jnp.max(s_loc, axis=-1, keepdims=True), jnp.max(s_ctx, axis=-1, keepdims=True))
        p_loc = jnp.exp2(s_loc - m)
        p_ctx = jnp.exp2(s_ctx - m)
        l = jnp.sum(p_loc, axis=-1, keepdims=True) + jnp.sum(p_ctx, axis=-1, keepdims=True)
        o = (_dot(p_loc.astype(BF16), v_ref[ks * GRID_W:ks * GRID_W + nk, hs])
             + _dot(p_ctx.astype(BF16), vc_ref[:, hs]))
        o_ref[blk * nq:(blk + 1) * nq, hs] = (o / l).astype(BF16)


def _natten(qkv, bias_tab, *, nbatch, seq, ctx_len, d):
    rows = seq // GRID_W
    nheads = d // HEAD_DIM
    assert rows % NA_QROWS == 0 and rows >= NA_KROWS
    ctx_blk0 = nbatch * seq // ctx_len
    hg = NA_HEADS_PER_STEP
    assert nheads % hg == 0
    ngroups = nheads // hg
    gw = hg * HEAD_DIM
    return pl.pallas_call(
        functools.partial(_natten_kernel, rows=rows),
        grid=(ngroups, nbatch),
        in_specs=[
            pl.BlockSpec((seq, gw), lambda h, b: (b, h)),
            pl.BlockSpec((seq, gw), lambda h, b: (b, ngroups + h)),
            pl.BlockSpec((seq, gw), lambda h, b: (b, 2 * ngroups + h)),
            pl.BlockSpec((ctx_len, gw), lambda h, b: (ctx_blk0 + b, ngroups + h)),
            pl.BlockSpec((ctx_len, gw), lambda h, b: (ctx_blk0 + b, 2 * ngroups + h)),
            pl.BlockSpec((hg,) + bias_tab.shape[1:], lambda h, b: (h, 0, 0, 0)),
        ],
        out_specs=pl.BlockSpec((seq, gw), lambda h, b: (b, h)),
        out_shape=jax.ShapeDtypeStruct((nbatch * seq, d), BF16),
        compiler_params=_params("arbitrary", "arbitrary"),
        name="natten",
    )(qkv, qkv, qkv, qkv, qkv, bias_tab)


def kernel(x, c, ctx, c_ctx, ada_w, ada_b, post_ln_g, post_ln_b, ab_w_in, ab_w_out, ab_q_gain, ab_k_gain,
           ab_dw_w, ab_dw_b, ab_norm_g, ab_norm_b, c_w_in, c_w_out, c_rel_bias, ffn_w_up, ffn_dw_w, ffn_dw_b,
           ffn_w_down):
    nbatch, seq, d = x.shape
    ctx_len = ctx.shape[1]
    depth = ada_w.shape[0]
    assert depth == 2 and nbatch + 1 <= COND_ROWS and nbatch * ctx_len <= seq
    assert seq % GRID_W == 0 and seq & (seq - 1) == 0 and ctx_len & (ctx_len - 1) == 0 and ctx_len <= seq
    nx, nc = nbatch * seq, nbatch * ctx_len
    alpha = (2 * depth) ** 0.25
    tm = _pick((seq, nc), (512, 256))
    n_lat_tiles = nx // tm
    geo = dict(tm=tm, n_lat_tiles=n_lat_tiles, seq=seq, nbatch=nbatch)

    cond = jnp.concatenate([c, c_ctx[None], jnp.zeros((COND_ROWS - nbatch - 1, d), F32)], axis=0)
    mod3 = _ada(cond, ada_w, ada_b).reshape(depth * COND_ROWS * 6, 1, d)
    x2d, c2d = x.reshape(nx, d), ctx.reshape(nc, d)
    ffn_w = (ffn_w_up.astype(BF16), ffn_dw_w, ffn_dw_b, jnp.swapaxes(ffn_w_down, 1, 2).astype(BF16))

    mod_base = 0
    qkv, glu = _inproj_ab(x2d, c2d, mod3, mod_base, ab_w_in[0].astype(BF16), ab_q_gain[0][None],
                          ab_k_gain[0][None], _rope_tables(seq, tm), **geo)
    attn = _gqa(qkv, nbatch=nbatch, seq=seq, ctx_len=ctx_len)
    conv = _conformer_conv(glu, ab_dw_w[0], ab_dw_b[0], ab_norm_g[0], ab_norm_b[0],
                           nbatch=nbatch, seq=seq, ctx_len=ctx_len)
    x1, h2 = _outproj([attn, conv], ab_w_out[0].astype(BF16), [x2d, c2d], mod3, mod_base, post_ln_g[0, 0],
                      post_ln_b[0, 0], n_out=nx + nc, alpha=alpha, **geo)
    xc = _ffn(h2, x1, 0, *ffn_w, mod3, mod_base, post_ln_g[0, 1], post_ln_b[0, 1], ctx_len=ctx_len, alpha=alpha,
              **geo)

    mod_base = COND_ROWS * 6
    qkv = _inproj_c(xc, mod3, mod_base, c_w_in[0].astype(BF16), **geo)
    attn = _natten(qkv, _na_bias_table(c_rel_bias[0]), nbatch=nbatch, seq=seq, ctx_len=ctx_len, d=d)
    x1, h2 = _outproj([attn], c_w_out[0].astype(BF16), [xc], mod3, mod_base, post_ln_g[1, 0], post_ln_b[1, 0],
                      n_out=nx, alpha=alpha, **geo)
    out = _ffn(h2, x1, 1, *ffn_w, mod3, mod_base, post_ln_g[1, 1], post_ln_b[1, 1], ctx_len=ctx_len, alpha=alpha,
               **geo)
    return out.reshape(nbatch, seq, d)
```

```python
import functools

import jax
import jax.numpy as jnp
import numpy as np
from jax import lax
from jax.experimental import pallas as pl
from jax.experimental.pallas import tpu as pltpu

F32 = jnp.float32
BF16 = jnp.bfloat16

HEAD_DIM = 128
GRID_W = 64
A_HEADS = 8
A_KV_HEADS = 2
A_GROUP = A_HEADS // A_KV_HEADS
ROPE_THETA = 10000.0
NA_ROWS = 8
NA_COLS = 16
LN_EPS = 1e-6
NEG = -1e30
LOG2E = 1.4426950408889634
ATTN_Q_SCALE = HEAD_DIM ** -0.5 * LOG2E

V7X_LANES = 128
V7X_F32_SUBLANES = 8
V7X_BF16_SUBLANES = 16
V7X_MXU_DIM = 256
V7X_VMEM_LIMIT = 56 * 1024 * 1024
PROJ_COLS = 2 * V7X_MXU_DIM
COND_ROWS = 8

NA_QROWS = 4
NA_HEADS_PER_STEP = 4
NA_KROWS = NA_QROWS + NA_ROWS


def _pick(n, candidates):
    for t in candidates:
        if all(v % t == 0 for v in n):
            return t
    raise ValueError(f"no tile in {candidates} divides {n}")


def _params(*sem):
    return pltpu.CompilerParams(dimension_semantics=sem, vmem_limit_bytes=V7X_VMEM_LIMIT)


def _layer_norm(r, g, b):
    mu = jnp.mean(r, axis=-1, keepdims=True)
    d = r - mu
    var = jnp.mean(d * d, axis=-1, keepdims=True)
    return d * lax.rsqrt(var + LN_EPS) * g + b


def _sigmoid(v):
    return 1.0 / (1.0 + jnp.exp(-v))


def _dot(a, b):
    return jnp.dot(a, b, preferred_element_type=F32)


def _dot_t(a, b):
    return lax.dot_general(a, b, (((1,), (1,)), ((), ())), preferred_element_type=F32)


def _ada_kernel(cond_ref, w_ref, b_ref, o_ref):
    cnd = cond_ref[...]
    s = (cnd * _sigmoid(cnd)).astype(BF16)
    o_ref[0] = _dot(s, w_ref[0].astype(BF16)) + b_ref[0]


def _ada(cond, ada_w, ada_b):
    depth, d, n6 = ada_w.shape
    tn = _pick((n6,), (1024, 512, 256, 128))
    return pl.pallas_call(
        _ada_kernel,
        grid=(depth, n6 // tn),
        in_specs=[
            pl.BlockSpec((COND_ROWS, d), lambda l, j: (0, 0)),
            pl.BlockSpec((1, d, tn), lambda l, j: (l, 0, j)),
            pl.BlockSpec((1, 1, tn), lambda l, j: (l, 0, j)),
        ],
        out_specs=pl.BlockSpec((1, COND_ROWS, tn), lambda l, j: (l, 0, j)),
        out_shape=jax.ShapeDtypeStruct((depth, COND_ROWS, n6), F32),
        compiler_params=_params("arbitrary", "arbitrary"),
        name="ada_mod",
    )(cond, ada_w, ada_b.reshape(depth, 1, n6))


def _inproj_ab_kernel(x_ref, c_ref, sh_ref, sc_ref, w_ref, qg_ref, kg_ref, rope_ref, qkv_ref, glu_ref, hb_ref, *,
                      q_scale, n_lat_tiles):
    is_lat = pl.program_id(0) < n_lat_tiles
    for src_ref, cond in ((x_ref, is_lat), (c_ref, jnp.logical_not(is_lat))):
        @pl.when(cond)
        def _():
            hb_ref[...] = (src_ref[...] * (1.0 + sc_ref[0]) + sh_ref[0]).astype(BF16)
    hb = hb_ref[...]
    cos, sin_lo, sin_hi = rope_ref[0], rope_ref[1], rope_ref[2]
    a_w = A_HEADS * HEAD_DIM
    kv_w = A_KV_HEADS * HEAD_DIM
    b_w = glu_ref.shape[1]

    def norm_rope(z, gain):
        zn = z * lax.rsqrt(jnp.mean(z * z, axis=-1, keepdims=True) + LN_EPS) * gain
        quarter = HEAD_DIM // 4
        return (zn * cos + pltpu.roll(zn, HEAD_DIM - quarter, 1) * sin_lo
                + pltpu.roll(zn, quarter, 1) * sin_hi)

    q_gain = qg_ref[...] * q_scale
    heads_per_dot = PROJ_COLS // HEAD_DIM
    for c0 in range(0, a_w, heads_per_dot * HEAD_DIM):
        z = _dot(hb, w_ref[:, c0:c0 + heads_per_dot * HEAD_DIM])
        for hh in range(heads_per_dot):
            cs = slice(c0 + hh * HEAD_DIM, c0 + (hh + 1) * HEAD_DIM)
            qkv_ref[:, cs] = norm_rope(z[:, hh * HEAD_DIM:(hh + 1) * HEAD_DIM], q_gain).astype(BF16)
    z = _dot(hb, w_ref[:, a_w:a_w + 2 * kv_w])
    for hh in range(A_KV_HEADS):
        cs = slice(a_w + hh * HEAD_DIM, a_w + (hh + 1) * HEAD_DIM)
        qkv_ref[:, cs] = norm_rope(z[:, hh * HEAD_DIM:(hh + 1) * HEAD_DIM], kg_ref[...]).astype(BF16)
    qkv_ref[:, a_w + kv_w:a_w + 2 * kv_w] = z[:, kv_w:].astype(BF16)
    u0 = a_w + 2 * kv_w
    glu_cols = PROJ_COLS if b_w % PROJ_COLS == 0 else b_w
    for c0 in range(0, b_w, glu_cols):
        u = _dot(hb, w_ref[:, u0 + c0:u0 + c0 + glu_cols])
        g = _dot(hb, w_ref[:, u0 + b_w + c0:u0 + b_w + c0 + glu_cols])
        glu_ref[:, c0:c0 + glu_cols] = u * _sigmoid(g)


def _inproj_ab(x2d, c2d, mod3, mod_base, w_in, q_gain, k_gain, rope, *, tm, n_lat_tiles, seq, nbatch):
    d = x2d.shape[1]
    n = x2d.shape[0] + c2d.shape[0]
    n_in = w_in.shape[1]
    a_w, kv_w = A_HEADS * HEAD_DIM, A_KV_HEADS * HEAD_DIM
    b_w = (n_in - a_w - 2 * kv_w) // 2
    tiles_per_seq = seq // tm

    def cond_row(i):
        return jnp.where(i < n_lat_tiles, (i * tm) // seq, nbatch)

    def rope_blk(i):
        return jnp.where(i < n_lat_tiles, i % tiles_per_seq, tiles_per_seq)

    return pl.pallas_call(
        functools.partial(_inproj_ab_kernel, q_scale=ATTN_Q_SCALE, n_lat_tiles=n_lat_tiles),
        grid=(n // tm,),
        in_specs=[
            pl.BlockSpec((tm, d), lambda i: (jnp.minimum(i, n_lat_tiles - 1), 0)),
            pl.BlockSpec((tm, d), lambda i: (jnp.maximum(i - n_lat_tiles, 0), 0)),
            pl.BlockSpec((1, 1, d), lambda i: (mod_base + cond_row(i) * 6 + 0, 0, 0)),
            pl.BlockSpec((1, 1, d), lambda i: (mod_base + cond_row(i) * 6 + 1, 0, 0)),
            pl.BlockSpec((d, n_in), lambda i: (0, 0), pipeline_mode=pl.Buffered(1)),
            pl.BlockSpec((1, HEAD_DIM), lambda i: (0, 0)),
            pl.BlockSpec((1, HEAD_DIM), lambda i: (0, 0)),
            pl.BlockSpec((3, tm, HEAD_DIM), lambda i: (0, rope_blk(i), 0)),
        ],
        out_specs=[
            pl.BlockSpec((tm, a_w + 2 * kv_w), lambda i: (i, 0)),
            pl.BlockSpec((tm, b_w), lambda i: (i, 0)),
        ],
        out_shape=[
            jax.ShapeDtypeStruct((n, a_w + 2 * kv_w), BF16),
            jax.ShapeDtypeStruct((n, b_w), F32),
        ],
        scratch_shapes=[pltpu.VMEM((tm, d), BF16)],
        compiler_params=_params("arbitrary"),
        name="inproj_ab",
    )(x2d, c2d, mod3, mod3, w_in, q_gain, k_gain, rope)


def _rope_tables(seq, tm):
    t = jnp.arange(seq)
    nfreq = HEAD_DIM // 4
    inv = ROPE_THETA ** (-jnp.arange(nfreq, dtype=F32) / nfreq)
    ang_r = (t // GRID_W).astype(F32)[:, None] * inv
    ang_c = (t % GRID_W).astype(F32)[:, None] * inv
    ang = jnp.concatenate([ang_r, ang_r, ang_c, ang_c], axis=-1)
    cos, sin = jnp.cos(ang), jnp.sin(ang)
    low = (np.arange(HEAD_DIM) // nfreq) % 2 == 0
    sin_lo = jnp.where(low, -sin, 0.0)
    sin_hi = jnp.where(low, 0.0, sin)
    tab = jnp.stack([cos, sin_lo, sin_hi])
    ident = jnp.stack([jnp.ones((tm, HEAD_DIM), F32), jnp.zeros((tm, HEAD_DIM), F32), jnp.zeros((tm, HEAD_DIM), F32)])
    return jnp.concatenate([tab, ident], axis=1)


def _with_ones(v):
    return jnp.concatenate([v, jnp.ones(v.shape, v.dtype)], axis=1)


def _scores(q, parts):
    return [_dot_t(q, k_ref[...]) for k_ref, _ in parts]


def _softmax_pv(ss, parts):
    m = functools.reduce(jnp.maximum, [jnp.max(s, axis=-1, keepdims=True) for s in ss])
    ps = [jnp.exp2(s - m) for s in ss]
    o = functools.reduce(jnp.add, [_dot(p.astype(BF16), _with_ones(v_ref[...])) for p, (_, v_ref) in zip(ps, parts)])
    return o[:, :HEAD_DIM] / o[:, HEAD_DIM:HEAD_DIM + 1]


def _gqa_kernel(q_ref, kx_ref, vx_ref, kc_ref, vc_ref, o_ref, *, n_lat_tiles):
    qi = pl.program_id(1)

    def run(parts):
        heads = [slice(g * HEAD_DIM, (g + 1) * HEAD_DIM) for g in range(A_GROUP)]
        ss = _scores(q_ref[:, heads[0]], parts)
        for g, cs in enumerate(heads):
            ss_next = _scores(q_ref[:, heads[g + 1]], parts) if g + 1 < A_GROUP else None
            o_ref[:, cs] = _softmax_pv(ss, parts).astype(BF16)
            ss = ss_next

    @pl.when(qi < n_lat_tiles)
    def _():
        run([(kx_ref, vx_ref), (kc_ref, vc_ref)])

    @pl.when(qi >= n_lat_tiles)
    def _():
        run([(kc_ref, vc_ref)])


def _gqa(qkv, *, nbatch, seq, ctx_len):
    n = qkv.shape[0]
    tq = _pick((seq, ctx_len), (256, 128))
    n_lat_tiles = nbatch * seq // tq
    lat_per_seq, ctx_per_seq = seq // tq, ctx_len // tq
    a_w = A_HEADS * HEAD_DIM
    gw = A_GROUP * HEAD_DIM
    k_col0 = a_w // HEAD_DIM
    v_col0 = k_col0 + A_KV_HEADS

    def batch(qi):
        return jnp.where(qi < n_lat_tiles, qi // lat_per_seq, (qi - n_lat_tiles) // ctx_per_seq)

    def lat_batch(qi):
        return jnp.minimum(qi // lat_per_seq, nbatch - 1)

    ctx_blk0 = nbatch * seq // ctx_len
    return pl.pallas_call(
        functools.partial(_gqa_kernel, n_lat_tiles=n_lat_tiles),
        grid=(A_KV_HEADS, n // tq),
        in_specs=[
            pl.BlockSpec((tq, gw), lambda h, qi: (qi, h)),
            pl.BlockSpec((seq, HEAD_DIM), lambda h, qi: (lat_batch(qi), k_col0 + h)),
            pl.BlockSpec((seq, HEAD_DIM), lambda h, qi: (lat_batch(qi), v_col0 + h)),
            pl.BlockSpec((ctx_len, HEAD_DIM), lambda h, qi: (ctx_blk0 + batch(qi), k_col0 + h)),
            pl.BlockSpec((ctx_len, HEAD_DIM), lambda h, qi: (ctx_blk0 + batch(qi), v_col0 + h)),
        ],
        out_specs=pl.BlockSpec((tq, gw), lambda h, qi: (qi, h)),
        out_shape=jax.ShapeDtypeStruct((n, a_w), BF16),
        compiler_params=_params("arbitrary", "arbitrary"),
        name="gqa_attn",
    )(qkv, qkv, qkv, qkv, qkv)


def _conv_kernel(x_ref, xp_ref, xn_ref, dw_ref, db_ref, g_ref, b_ref, o_ref, buf_ref, y_ref, shift_ref, *,
                 tt, halo, n_lat_tiles, lat_per_seq, ctx_per_seq):
    i = pl.program_id(0)
    taps = dw_ref.shape[0]
    pad = taps // 2
    idx = jnp.where(i < n_lat_tiles, i % lat_per_seq, (i - n_lat_tiles) % ctx_per_seq)
    per_seq = jnp.where(i < n_lat_tiles, lat_per_seq, ctx_per_seq)
    buf_ref[0:halo, :] = jnp.where(idx > 0, xp_ref[...], 0.0)
    buf_ref[halo:halo + tt, :] = x_ref[...]
    buf_ref[halo + tt:, :] = jnp.where(idx < per_seq - 1, xn_ref[...], 0.0)
    nchunk = x_ref.shape[1] // V7X_LANES

    rows = tt + 2 * halo

    def chunk(c, carry):
        cs = pl.ds(pl.multiple_of(c * V7X_LANES, V7X_LANES), V7X_LANES)
        xb = buf_ref[:, cs]
        for s in range(1, V7X_F32_SUBLANES):
            shift_ref[s] = pltpu.roll(xb, rows - s, 0)
        acc = jnp.broadcast_to(db_ref[:, cs], (tt, V7X_LANES))
        for k in range(taps):
            off = halo - pad + k
            base, s = off - off % V7X_F32_SUBLANES, off % V7X_F32_SUBLANES
            win = buf_ref[pl.ds(base, tt), cs] if s == 0 else shift_ref[s, base:base + tt, :]
            acc = acc + win * dw_ref[pl.ds(k, 1), cs]
        y_ref[:, cs] = acc
        return carry

    lax.fori_loop(0, nchunk, chunk, 0)
    yn = _layer_norm(y_ref[...], g_ref[...], b_ref[...])
    o_ref[...] = (yn * _sigmoid(yn)).astype(BF16)


def _conformer_conv(glu, dw_w, dw_b, n_g, n_b, *, nbatch, seq, ctx_len):
    n, cw = glu.shape
    tt = _pick((seq, ctx_len), (256, 128))
    halo = pl.cdiv(dw_w.shape[0] // 2, V7X_F32_SUBLANES) * V7X_F32_SUBLANES
    n_lat_tiles = nbatch * seq // tt
    hb = tt // halo
    nhalo = n // halo
    return pl.pallas_call(
        functools.partial(_conv_kernel, tt=tt, halo=halo, n_lat_tiles=n_lat_tiles,
                          lat_per_seq=seq // tt, ctx_per_seq=ctx_len // tt),
        grid=(n // tt,),
        in_specs=[
            pl.BlockSpec((tt, cw), lambda i: (i, 0)),
            pl.BlockSpec((halo, cw), lambda i: (jnp.maximum(i * hb - 1, 0), 0)),
            pl.BlockSpec((halo, cw), lambda i: (jnp.minimum((i + 1) * hb, nhalo - 1), 0)),
            pl.BlockSpec(dw_w.shape, lambda i: (0, 0)),
            pl.BlockSpec((1, cw), lambda i: (0, 0)),
            pl.BlockSpec((1, cw), lambda i: (0, 0)),
            pl.BlockSpec((1, cw), lambda i: (0, 0)),
        ],
        out_specs=pl.BlockSpec((tt, cw), lambda i: (i, 0)),
        out_shape=jax.ShapeDtypeStruct((n, cw), BF16),
        scratch_shapes=[pltpu.VMEM((tt + 2 * halo, cw), F32), pltpu.VMEM((tt, cw), F32),
                        pltpu.VMEM((V7X_F32_SUBLANES, tt + 2 * halo, V7X_LANES), F32)],
        compiler_params=_params("arbitrary"),
        name="conformer_conv",
    )(glu, glu, glu, dw_w, dw_b.reshape(1, cw), n_g.reshape(1, cw), n_b.reshape(1, cw))


def _outproj_kernel(*refs, n_parts, n_res, n_lat_tiles, alpha):
    part_refs = refs[:n_parts]
    res_refs = refs[n_parts + 1:n_parts + 1 + n_res]
    w_ref = refs[n_parts]
    gate_ref, sh_ref, sc_ref, lg_ref, lb_ref, x1_ref, h2_ref = refs[n_parts + 1 + n_res:]
    tm = x1_ref.shape[0]
    rb = tm // 2 if tm % (2 * V7X_BF16_SUBLANES) == 0 else tm

    def proj(r0):
        y = None
        k0 = 0
        for p_ref in part_refs:
            kw = p_ref.shape[1]
            t = _dot(p_ref[r0:r0 + rb, :], w_ref[k0:k0 + kw, :])
            y = t if y is None else y + t
            k0 += kw
        return y

    def finish(x_ref, r0, y):
        rs = slice(r0, r0 + rb)
        x1 = _layer_norm(alpha * x_ref[rs, :] + gate_ref[0] * y, lg_ref[...], lb_ref[...])
        x1_ref[rs, :] = x1
        h2_ref[rs, :] = (x1 * (1.0 + sc_ref[0]) + sh_ref[0]).astype(BF16)

    def body(x_ref):
        pending = proj(0)
        for r0 in range(0, tm, rb):
            upcoming = proj(r0 + rb) if r0 + rb < tm else None
            finish(x_ref, r0, pending)
            pending = upcoming

    if n_res == 1:
        body(res_refs[0])
    else:
        is_lat = pl.program_id(0) < n_lat_tiles
        pl.when(is_lat)(functools.partial(body, res_refs[0]))
        pl.when(jnp.logical_not(is_lat))(functools.partial(body, res_refs[1]))


def _outproj(parts, w_out, residuals, mod3, mod_base, ln_g, ln_b, *, n_out, tm, n_lat_tiles, seq, nbatch, alpha):
    d = residuals[0].shape[1]
    if len(residuals) == 1:
        res_specs = [pl.BlockSpec((tm, d), lambda i: (i, 0))]
    else:
        res_specs = [pl.BlockSpec((tm, d), lambda i: (jnp.minimum(i, n_lat_tiles - 1), 0)),
                     pl.BlockSpec((tm, d), lambda i: (jnp.maximum(i - n_lat_tiles, 0), 0))]

    def cond_row(i):
        return jnp.where(i < n_lat_tiles, (i * tm) // seq, nbatch)

    def mod_spec(k):
        return pl.BlockSpec((1, 1, d), lambda i: (mod_base + cond_row(i) * 6 + k, 0, 0))

    return pl.pallas_call(
        functools.partial(_outproj_kernel, n_parts=len(parts), n_res=len(residuals), n_lat_tiles=n_lat_tiles,
                          alpha=alpha),
        grid=(n_out // tm,),
        in_specs=[pl.BlockSpec((tm, p.shape[1]), lambda i: (i, 0)) for p in parts] + [
            pl.BlockSpec(w_out.shape, lambda i: (0, 0), pipeline_mode=pl.Buffered(1))] + res_specs + [
            mod_spec(2), mod_spec(3), mod_spec(4),
            pl.BlockSpec((1, d), lambda i: (0, 0)),
            pl.BlockSpec((1, d), lambda i: (0, 0)),
        ],
        out_specs=[pl.BlockSpec((tm, d), lambda i: (i, 0)), pl.BlockSpec((tm, d), lambda i: (i, 0))],
        out_shape=[jax.ShapeDtypeStruct((n_out, d), F32), jax.ShapeDtypeStruct((n_out, d), BF16)],
        compiler_params=_params("arbitrary"),
        name="outproj_ln",
    )(*parts, w_out, *residuals, mod3, mod3, mod3, ln_g.reshape(1, d), ln_b.reshape(1, d))


def _ffn_kernel(h_ref, hp_ref, hn_ref, wa_ref, wg_ref, dwa_ref, dwg_ref, dba_ref, dbg_ref, wd_ref,
                x_ref, gate_ref, lg_ref, lb_ref, o_ref,
                hext_ref, za0_ref, za1_ref, zg0_ref, zg1_ref, acc_ref, *,
                tm, halo, sub, nf, seq, ctx_len, n_lat_tiles, alpha):
    za_refs = (za0_ref, za1_ref)
    zg_refs = (zg0_ref, zg1_ref)
    i = pl.program_id(0)
    c = pl.program_id(1)
    is_lat = i < n_lat_tiles
    row = lax.broadcasted_iota(jnp.int32, (tm, 1), 0) + i * tm
    pos = row & (ctx_len - 1)
    has_prev = pos != 0
    has_next = pos != ctx_len - 1

    chunks = [slice(s0, s0 + sub) for s0 in range(0, wa_ref.shape[1], sub)]

    def up_proj(slot):
        za_refs[slot][...] = _dot(hext_ref[...], wa_ref[...])
        zg_refs[slot][...] = _dot(hext_ref[...], wg_ref[...])

    def conv(z_ref, dw_ref, db_ref, cs, masked):
        z = z_ref[:, cs]
        rows_ext = tm + 2 * halo
        z_prev = pltpu.roll(z, 1, 0)[halo:halo + tm]
        z_next = pltpu.roll(z, rows_ext - 1, 0)[halo:halo + tm]
        if masked:
            z_prev = jnp.where(has_prev, z_prev, 0.0)
            z_next = jnp.where(has_next, z_next, 0.0)
        return z_prev * dw_ref[0:1, cs] + z[halo:halo + tm] * dw_ref[1:2, cs] + z_next * dw_ref[2:3, cs] + db_ref[:, cs]

    def step(up_slot, down_slot, masked=False):
        if up_slot is not None:
            up_proj(up_slot)
        if down_slot is not None:
            contrib = None
            for cs in chunks:
                a = conv(za_refs[down_slot], dwa_ref, dba_ref, cs, masked)
                g = conv(zg_refs[down_slot], dwg_ref, dbg_ref, cs, masked)
                t = _dot_t(wd_ref[:, cs], (g * _sigmoid(g) * a).astype(BF16))
                contrib = t if contrib is None else contrib + t
            acc_ref[...] += contrib

    @pl.when(c == 0)
    def _():
        starts_seq = is_lat & ((i * tm) % seq == 0)
        ends_seq = is_lat & (((i + 1) * tm) % seq == 0)
        zero_halo = jnp.zeros(hp_ref.shape, hp_ref.dtype)
        hext_ref[0:halo, :] = jnp.where(starts_seq, zero_halo, hp_ref[...])
        hext_ref[halo:halo + tm, :] = h_ref[...]
        hext_ref[halo + tm:, :] = jnp.where(ends_seq, zero_halo, hn_ref[...])
        acc_ref[...] = jnp.zeros_like(acc_ref)
        step(0, None)

    for masked in (False, True):
        for slot in (0, 1):
            @pl.when((c >= 1) & (c < nf) & (c % 2 == slot) & (is_lat != masked))
            def _():
                step(slot, 1 - slot, masked)

        @pl.when((c == nf) & (is_lat != masked))
        def _():
            step(None, (nf - 1) % 2, masked)

    @pl.when(c == nf)
    def _():
        o_ref[...] = _layer_norm(alpha * x_ref[...] + gate_ref[0] * acc_ref[...].T, lg_ref[...], lb_ref[...])


def _ffn(h2, x1, layer, w_up, dw_w, dw_b, w_down, mod3, mod_base, ln_g, ln_b, *, tm, n_lat_tiles, seq, ctx_len,
         nbatch, alpha):
    n_out, d = x1.shape
    depth, _, d_ff = w_down.shape
    tf = _pick((d_ff,), (512, 256, 128))
    nf = d_ff // tf
    halo = V7X_BF16_SUBLANES
    hb = tm // halo
    nhalo = h2.shape[0] // halo
    ntiles = n_out // tm
    assert dw_w.shape[1] == 3

    def cond_row(i):
        return jnp.where(i < n_lat_tiles, (i * tm) // seq, nbatch)

    dw_b3 = dw_b.reshape(depth, 1, 2 * d_ff)

    def up(c):
        return jnp.minimum(c, nf - 1)

    def down(c):
        return jnp.maximum(c - 1, 0)

    return pl.pallas_call(
        functools.partial(_ffn_kernel, tm=tm, halo=halo, sub=min(tf, V7X_MXU_DIM), nf=nf, seq=seq, ctx_len=ctx_len,
                          n_lat_tiles=n_lat_tiles, alpha=alpha),
        grid=(ntiles, nf + 1),
        in_specs=[
            pl.BlockSpec((tm, d), lambda i, c: (i, 0)),
            pl.BlockSpec((halo, d), lambda i, c: (jnp.maximum(i * hb - 1, 0), 0)),
            pl.BlockSpec((halo, d), lambda i, c: (jnp.minimum((i + 1) * hb, nhalo - 1), 0)),
            pl.BlockSpec((None, d, tf), lambda i, c: (layer, 0, up(c))),
            pl.BlockSpec((None, d, tf), lambda i, c: (layer, 0, nf + up(c))),
            pl.BlockSpec((None, 3, tf), lambda i, c: (layer, 0, down(c))),
            pl.BlockSpec((None, 3, tf), lambda i, c: (layer, 0, nf + down(c))),
            pl.BlockSpec((None, 1, tf), lambda i, c: (layer, 0, down(c))),
            pl.BlockSpec((None, 1, tf), lambda i, c: (layer, 0, nf + down(c))),
            pl.BlockSpec((None, d, tf), lambda i, c: (layer, 0, down(c))),
            pl.BlockSpec((tm, d), lambda i, c: (i, 0)),
            pl.BlockSpec((1, 1, d), lambda i, c: (mod_base + cond_row(i) * 6 + 5, 0, 0)),
            pl.BlockSpec((1, d), lambda i, c: (0, 0)),
            pl.BlockSpec((1, d), lambda i, c: (0, 0)),
        ],
        out_specs=pl.BlockSpec((tm, d), lambda i, c: (i, 0)),
        out_shape=jax.ShapeDtypeStruct((n_out, d), F32),
        scratch_shapes=([pltpu.VMEM((tm + 2 * halo, d), BF16)] + [pltpu.VMEM((tm + 2 * halo, tf), F32)] * 4
                        + [pltpu.VMEM((d, tm), F32)]),
        compiler_params=_params("arbitrary", "arbitrary"),
        name="conv_ffn",
    )(h2, h2, h2, w_up, w_up, dw_w, dw_w, dw_b3, dw_b3, w_down, x1, mod3, ln_g.reshape(1, d), ln_b.reshape(1, d))


def _inproj_c_kernel(x_ref, sh_ref, sc_ref, w_ref, o_ref, *, q_scale):
    hb = (x_ref[...] * (1.0 + sc_ref[0]) + sh_ref[0]).astype(BF16)
    d = x_ref.shape[1]
    cols = PROJ_COLS
    for c0 in range(0, o_ref.shape[1], cols):
        z = _dot(hb, w_ref[:, c0:c0 + cols])
        if c0 < d:
            z = z * q_scale
        o_ref[:, c0:c0 + cols] = z.astype(BF16)


def _inproj_c(xc, mod3, mod_base, w_in, *, tm, n_lat_tiles, seq, nbatch):
    n, d = xc.shape
    n_in = w_in.shape[1]
    assert d % PROJ_COLS == 0 and n_in == 3 * d

    def cond_row(i):
        return jnp.where(i < n_lat_tiles, (i * tm) // seq, nbatch)

    return pl.pallas_call(
        functools.partial(_inproj_c_kernel, q_scale=ATTN_Q_SCALE),
        grid=(n // tm,),
        in_specs=[
            pl.BlockSpec((tm, d), lambda i: (i, 0)),
            pl.BlockSpec((1, 1, d), lambda i: (mod_base + cond_row(i) * 6 + 0, 0, 0)),
            pl.BlockSpec((1, 1, d), lambda i: (mod_base + cond_row(i) * 6 + 1, 0, 0)),
            pl.BlockSpec((d, n_in), lambda i: (0, 0), pipeline_mode=pl.Buffered(1)),
        ],
        out_specs=pl.BlockSpec((tm, n_in), lambda i: (i, 0)),
        out_shape=jax.ShapeDtypeStruct((n, n_in), BF16),
        compiler_params=_params("arbitrary"),
        name="inproj_c",
    )(xc, mod3, mod3, w_in)


def _na_bias_table(rel_bias):
    nheads, nrow, ncol = rel_bias.shape
    qc = np.arange(GRID_W)[:, None]
    kc = np.arange(GRID_W)[None, :]
    cstart = np.clip(qc - NA_COLS // 2, 0, GRID_W - NA_COLS)
    valid = (kc >= cstart) & (kc < cstart + NA_COLS)
    period = 2 * GRID_W - 1
    lead = GRID_W - NA_COLS - 1
    padded = jnp.pad(rel_bias, ((0, 0), (0, 0), (lead, period - lead - ncol)), constant_values=NEG)
    tiled = jnp.broadcast_to(padded[:, :, None, :], (nheads, nrow, GRID_W, period))
    skew = tiled.reshape(nheads, nrow, GRID_W * period)[:, :, :GRID_W * (period - 1)]
    toep = skew.reshape(nheads, nrow, GRID_W, period - 1)[:, :, :, GRID_W - 2:]
    toep = jnp.where(valid[None, None], toep * LOG2E, NEG)
    neg = jnp.full((nheads, 1, GRID_W, GRID_W), NEG, F32)
    ext = jnp.concatenate([neg, toep, neg], axis=1)
    return jnp.concatenate([ext[:, :2 * NA_ROWS], ext[:, 1:]], axis=-1)


def _natten_kernel(q_ref, k_ref, v_ref, kc_ref, vc_ref, bias_ref, o_ref, *, rows):
    nq = NA_QROWS * GRID_W
    nk = NA_KROWS * GRID_W
    left = lax.broadcasted_iota(jnp.int32, (GRID_W, 2 * GRID_W), 1) < GRID_W
    for blk, hh in [(b_, h_) for b_ in range(rows // NA_QROWS) for h_ in range(bias_ref.shape[0])]:
        hs = slice(hh * HEAD_DIM, (hh + 1) * HEAD_DIM)
        ks = min(max(NA_QROWS * blk - NA_ROWS // 2, 0), rows - NA_KROWS)
        q = q_ref[blk * nq:(blk + 1) * nq, hs]
        s_loc = _dot_t(q, k_ref[ks * GRID_W:ks * GRID_W + nk, hs])
        bias_rows = []
        for qi in range(NA_QROWS):
            qr = NA_QROWS * blk + qi
            r0 = min(max(qr - NA_ROWS // 2, 0), rows - NA_ROWS)
            slabs = []
            for j in range(NA_KROWS // 2):
                kr = ks + 2 * j
                ok0 = r0 <= kr < r0 + NA_ROWS
                ok1 = r0 <= kr + 1 < r0 + NA_ROWS
                if not (ok0 or ok1):
                    slabs.append(jnp.full((GRID_W, 2 * GRID_W), NEG, F32))
                    continue
                slab = bias_ref[hh, kr - qr + NA_ROWS]
                if not ok1:
                    slab = jnp.where(left, slab, NEG)
                elif not ok0:
                    slab = jnp.where(left, NEG, slab)
                slabs.append(slab)
            bias_rows.append(jnp.concatenate(slabs, axis=1))
        s_loc = s_loc + jnp.concatenate(bias_rows, axis=0)
        s_ctx = _dot_t(q, kc_ref[:, hs])
        m = jnp.maximum(jnp.max(s_loc, axis=-1, keepdims=True), jnp.max(s_ctx, axis=-1, keepdims=True))
        p_loc = jnp.exp2(s_loc - m)
        p_ctx = jnp.exp2(s_ctx - m)
        o = (_dot(p_loc.astype(BF16), _with_ones(v_ref[ks * GRID_W:ks * GRID_W + nk, hs]))
             + _dot(p_ctx.astype(BF16), _with_ones(vc_ref[:, hs])))
        o_ref[blk * nq:(blk + 1) * nq, hs] = (o[:, :HEAD_DIM] / o[:, HEAD_DIM:HEAD_DIM + 1]).astype(BF16)


def _natten(qkv, bias_tab, *, nbatch, seq, ctx_len, d):
    rows = seq // GRID_W
    nheads = d // HEAD_DIM
    assert rows % NA_QROWS == 0 and rows >= NA_KROWS
    ctx_blk0 = nbatch * seq // ctx_len
    hg = NA_HEADS_PER_STEP
    assert nheads % hg == 0
    ngroups = nheads // hg
    gw = hg * HEAD_DIM
    return pl.pallas_call(
        functools.partial(_natten_kernel, rows=rows),
        grid=(ngroups, nbatch),
        in_specs=[
            pl.BlockSpec((seq, gw), lambda h, b: (b, h)),
            pl.BlockSpec((seq, gw), lambda h, b: (b, ngroups + h)),
            pl.BlockSpec((seq, gw), lambda h, b: (b, 2 * ngroups + h)),
            pl.BlockSpec((ctx_len, gw), lambda h, b: (ctx_blk0 + b, ngroups + h)),
            pl.BlockSpec((ctx_len, gw), lambda h, b: (ctx_blk0 + b, 2 * ngroups + h)),
            pl.BlockSpec((hg,) + bias_tab.shape[1:], lambda h, b: (h, 0, 0, 0)),
        ],
        out_specs=pl.BlockSpec((seq, gw), lambda h, b: (b, h)),
        out_shape=jax.ShapeDtypeStruct((nbatch * seq, d), BF16),
        compiler_params=_params("arbitrary", "arbitrary"),
        name="natten",
    )(qkv, qkv, qkv, qkv, qkv, bias_tab)


def kernel(x, c, ctx, c_ctx, ada_w, ada_b, post_ln_g, post_ln_b, ab_w_in, ab_w_out, ab_q_gain, ab_k_gain,
           ab_dw_w, ab_dw_b, ab_norm_g, ab_norm_b, c_w_in, c_w_out, c_rel_bias, ffn_w_up, ffn_dw_w, ffn_dw_b,
           ffn_w_down):
    nbatch, seq, d = x.shape
    ctx_len = ctx.shape[1]
    depth = ada_w.shape[0]
    assert depth == 2 and nbatch + 1 <= COND_ROWS and nbatch * ctx_len <= seq
    assert seq % GRID_W == 0 and seq & (seq - 1) == 0 and ctx_len & (ctx_len - 1) == 0 and ctx_len <= seq
    nx, nc = nbatch * seq, nbatch * ctx_len
    alpha = (2 * depth) ** 0.25
    tm = _pick((seq, nc), (512, 256))
    n_lat_tiles = nx // tm
    geo = dict(tm=tm, n_lat_tiles=n_lat_tiles, seq=seq, nbatch=nbatch)

    cond = jnp.concatenate([c, c_ctx[None], jnp.zeros((COND_ROWS - nbatch - 1, d), F32)], axis=0)
    mod3 = _ada(cond, ada_w, ada_b).reshape(depth * COND_ROWS * 6, 1, d)
    x2d, c2d = x.reshape(nx, d), ctx.reshape(nc, d)
    ffn_w = (ffn_w_up.astype(BF16), ffn_dw_w, ffn_dw_b, jnp.swapaxes(ffn_w_down, 1, 2).astype(BF16))

    mod_base = 0
    qkv, glu = _inproj_ab(x2d, c2d, mod3, mod_base, ab_w_in[0].astype(BF16), ab_q_gain[0][None],
                          ab_k_gain[0][None], _rope_tables(seq, tm), **geo)
    attn = _gqa(qkv, nbatch=nbatch, seq=seq, ctx_len=ctx_len)
    conv = _conformer_conv(glu, ab_dw_w[0], ab_dw_b[0], ab_norm_g[0], ab_norm_b[0],
                           nbatch=nbatch, seq=seq, ctx_len=ctx_len)
    x1, h2 = _outproj([attn, conv], ab_w_out[0].astype(BF16), [x2d, c2d], mod3, mod_base, post_ln_g[0, 0],
                      post_ln_b[0, 0], n_out=nx + nc, alpha=alpha, **geo)
    xc = _ffn(h2, x1, 0, *ffn_w, mod3, mod_base, post_ln_g[0, 1], post_ln_b[0, 1], ctx_len=ctx_len, alpha=alpha,
              **geo)

    mod_base = COND_ROWS * 6
    qkv = _inproj_c(xc, mod3, mod_base, c_w_in[0].astype(BF16), **geo)
    attn = _natten(qkv, _na_bias_table(c_rel_bias[0]), nbatch=nbatch, seq=seq, ctx_len=ctx_len, d=d)
    x1, h2 = _outproj([attn], c_w_out[0].astype(BF16), [xc], mod3, mod_base, post_ln_g[1, 0], post_ln_b[1, 0],
                      n_out=nx, alpha=alpha, **geo)
    out = _ffn(h2, x1, 1, *ffn_w, mod3, mod_base, post_ln_g[1, 1], post_ln_b[1, 1], ctx_len=ctx_len, alpha=alpha,
               **geo)
    return out.reshape(nbatch, seq, d)
```

```python
import functools

import jax
import jax.numpy as jnp
import numpy as np
from jax import lax
from jax.experimental import pallas as pl
from jax.experimental.pallas import tpu as pltpu

F32 = jnp.float32
BF16 = jnp.bfloat16

HEAD_DIM = 128
GRID_W = 64
A_HEADS = 8
A_KV_HEADS = 2
A_GROUP = A_HEADS // A_KV_HEADS
ROPE_THETA = 10000.0
NA_ROWS = 8
NA_COLS = 16
LN_EPS = 1e-6
NEG = -1e30
LOG2E = 1.4426950408889634
ATTN_Q_SCALE = HEAD_DIM ** -0.5 * LOG2E

V7X_LANES = 128
V7X_F32_SUBLANES = 8
V7X_BF16_SUBLANES = 16
V7X_MXU_DIM = 256
V7X_VMEM_LIMIT = 56 * 1024 * 1024
PROJ_COLS = 2 * V7X_MXU_DIM
COND_ROWS = 8

NA_QROWS = 4
NA_HEADS_PER_STEP = 4
NA_KROWS = NA_QROWS + NA_ROWS


def _pick(n, candidates):
    for t in candidates:
        if all(v % t == 0 for v in n):
            return t
    raise ValueError(f"no tile in {candidates} divides {n}")


def _params(*sem):
    return pltpu.CompilerParams(dimension_semantics=sem, vmem_limit_bytes=V7X_VMEM_LIMIT)


def _layer_norm(r, g, b):
    mu = jnp.mean(r, axis=-1, keepdims=True)
    d = r - mu
    var = jnp.mean(d * d, axis=-1, keepdims=True)
    return d * lax.rsqrt(var + LN_EPS) * g + b


def _sigmoid(v):
    return 1.0 / (1.0 + jnp.exp(-v))


def _dot(a, b):
    return jnp.dot(a, b, preferred_element_type=F32)


def _dot_t(a, b):
    return lax.dot_general(a, b, (((1,), (1,)), ((), ())), preferred_element_type=F32)


def _ada_kernel(cond_ref, w_ref, b_ref, o_ref):
    cnd = cond_ref[...]
    s = (cnd * _sigmoid(cnd)).astype(BF16)
    o_ref[0] = _dot(s, w_ref[0].astype(BF16)) + b_ref[0]


def _ada(cond, ada_w, ada_b):
    depth, d, n6 = ada_w.shape
    tn = _pick((n6,), (1024, 512, 256, 128))
    return pl.pallas_call(
        _ada_kernel,
        grid=(depth, n6 // tn),
        in_specs=[
            pl.BlockSpec((COND_ROWS, d), lambda l, j: (0, 0)),
            pl.BlockSpec((1, d, tn), lambda l, j: (l, 0, j)),
            pl.BlockSpec((1, 1, tn), lambda l, j: (l, 0, j)),
        ],
        out_specs=pl.BlockSpec((1, COND_ROWS, tn), lambda l, j: (l, 0, j)),
        out_shape=jax.ShapeDtypeStruct((depth, COND_ROWS, n6), F32),
        compiler_params=_params("arbitrary", "arbitrary"),
        name="ada_mod",
    )(cond, ada_w, ada_b.reshape(depth, 1, n6))


def _inproj_ab_kernel(x_ref, c_ref, sh_ref, sc_ref, w_ref, qg_ref, kg_ref, rope_ref, qkv_ref, glu_ref, hb_ref, *,
                      q_scale, n_lat_tiles):
    is_lat = pl.program_id(0) < n_lat_tiles
    for src_ref, cond in ((x_ref, is_lat), (c_ref, jnp.logical_not(is_lat))):
        @pl.when(cond)
        def _():
            hb_ref[...] = (src_ref[...] * (1.0 + sc_ref[0]) + sh_ref[0]).astype(BF16)
    hb = hb_ref[...]
    cos, sin_lo, sin_hi = rope_ref[0], rope_ref[1], rope_ref[2]
    a_w = A_HEADS * HEAD_DIM
    kv_w = A_KV_HEADS * HEAD_DIM
    b_w = glu_ref.shape[1]

    def norm_rope(z, gain):
        zn = z * lax.rsqrt(jnp.mean(z * z, axis=-1, keepdims=True) + LN_EPS) * gain
        quarter = HEAD_DIM // 4
        return (zn * cos + pltpu.roll(zn, HEAD_DIM - quarter, 1) * sin_lo
                + pltpu.roll(zn, quarter, 1) * sin_hi)

    q_gain = qg_ref[...] * q_scale
    heads_per_dot = PROJ_COLS // HEAD_DIM
    for c0 in range(0, a_w, heads_per_dot * HEAD_DIM):
        z = _dot(hb, w_ref[:, c0:c0 + heads_per_dot * HEAD_DIM])
        for hh in range(heads_per_dot):
            cs = slice(c0 + hh * HEAD_DIM, c0 + (hh + 1) * HEAD_DIM)
            qkv_ref[:, cs] = norm_rope(z[:, hh * HEAD_DIM:(hh + 1) * HEAD_DIM], q_gain).astype(BF16)
    z = _dot(hb, w_ref[:, a_w:a_w + 2 * kv_w])
    for hh in range(A_KV_HEADS):
        cs = slice(a_w + hh * HEAD_DIM, a_w + (hh + 1) * HEAD_DIM)
        qkv_ref[:, cs] = norm_rope(z[:, hh * HEAD_DIM:(hh + 1) * HEAD_DIM], kg_ref[...]).astype(BF16)
    qkv_ref[:, a_w + kv_w:a_w + 2 * kv_w] = z[:, kv_w:].astype(BF16)
    u0 = a_w + 2 * kv_w
    glu_cols = PROJ_COLS if b_w % PROJ_COLS == 0 else b_w
    for c0 in range(0, b_w, glu_cols):
        u = _dot(hb, w_ref[:, u0 + c0:u0 + c0 + glu_cols])
        g = _dot(hb, w_ref[:, u0 + b_w + c0:u0 + b_w + c0 + glu_cols])
        glu_ref[:, c0:c0 + glu_cols] = u * _sigmoid(g)


def _inproj_ab(x2d, c2d, mod3, mod_base, w_in, q_gain, k_gain, rope, *, tm, n_lat_tiles, seq, nbatch):
    d = x2d.shape[1]
    n = x2d.shape[0] + c2d.shape[0]
    n_in = w_in.shape[1]
    a_w, kv_w = A_HEADS * HEAD_DIM, A_KV_HEADS * HEAD_DIM
    b_w = (n_in - a_w - 2 * kv_w) // 2
    tiles_per_seq = seq // tm

    def cond_row(i):
        return jnp.where(i < n_lat_tiles, (i * tm) // seq, nbatch)

    def rope_blk(i):
        return jnp.where(i < n_lat_tiles, i % tiles_per_seq, tiles_per_seq)

    return pl.pallas_call(
        functools.partial(_inproj_ab_kernel, q_scale=ATTN_Q_SCALE, n_lat_tiles=n_lat_tiles),
        grid=(n // tm,),
        in_specs=[
            pl.BlockSpec((tm, d), lambda i: (jnp.minimum(i, n_lat_tiles - 1), 0)),
            pl.BlockSpec((tm, d), lambda i: (jnp.maximum(i - n_lat_tiles, 0), 0)),
            pl.BlockSpec((1, 1, d), lambda i: (mod_base + cond_row(i) * 6 + 0, 0, 0)),
            pl.BlockSpec((1, 1, d), lambda i: (mod_base + cond_row(i) * 6 + 1, 0, 0)),
            pl.BlockSpec((d, n_in), lambda i: (0, 0), pipeline_mode=pl.Buffered(1)),
            pl.BlockSpec((1, HEAD_DIM), lambda i: (0, 0)),
            pl.BlockSpec((1, HEAD_DIM), lambda i: (0, 0)),
            pl.BlockSpec((3, tm, HEAD_DIM), lambda i: (0, rope_blk(i), 0)),
        ],
        out_specs=[
            pl.BlockSpec((tm, a_w + 2 * kv_w), lambda i: (i, 0)),
            pl.BlockSpec((tm, b_w), lambda i: (i, 0)),
        ],
        out_shape=[
            jax.ShapeDtypeStruct((n, a_w + 2 * kv_w), BF16),
            jax.ShapeDtypeStruct((n, b_w), F32),
        ],
        scratch_shapes=[pltpu.VMEM((tm, d), BF16)],
        compiler_params=_params("arbitrary"),
        name="inproj_ab",
    )(x2d, c2d, mod3, mod3, w_in, q_gain, k_gain, rope)


def _rope_tables(seq, tm):
    t = jnp.arange(seq)
    nfreq = HEAD_DIM // 4
    inv = ROPE_THETA ** (-jnp.arange(nfreq, dtype=F32) / nfreq)
    ang_r = (t // GRID_W).astype(F32)[:, None] * inv
    ang_c = (t % GRID_W).astype(F32)[:, None] * inv
    ang = jnp.concatenate([ang_r, ang_r, ang_c, ang_c], axis=-1)
    cos, sin = jnp.cos(ang), jnp.sin(ang)
    low = (np.arange(HEAD_DIM) // nfreq) % 2 == 0
    sin_lo = jnp.where(low, -sin, 0.0)
    sin_hi = jnp.where(low, 0.0, sin)
    tab = jnp.stack([cos, sin_lo, sin_hi])
    ident = jnp.stack([jnp.ones((tm, HEAD_DIM), F32), jnp.zeros((tm, HEAD_DIM), F32), jnp.zeros((tm, HEAD_DIM), F32)])
    return jnp.concatenate([tab, ident], axis=1)


def _with_ones(v):
    return jnp.concatenate([v, jnp.ones(v.shape, v.dtype)], axis=1)


def _scores(q, parts):
    return [_dot_t(q, k_ref[...]) for k_ref, _ in parts]


def _softmax_pv(ss, parts):
    m = functools.reduce(jnp.maximum, [jnp.max(s, axis=-1, keepdims=True) for s in ss])
    ps = [jnp.exp2(s - m) for s in ss]
    o = functools.reduce(jnp.add, [_dot(p.astype(BF16), _with_ones(v_ref[...])) for p, (_, v_ref) in zip(ps, parts)])
    return o[:, :HEAD_DIM] / o[:, HEAD_DIM:HEAD_DIM + 1]


def _gqa_kernel(q_ref, kx_ref, vx_ref, kc_ref, vc_ref, o_ref, *, n_lat_tiles):
    qi = pl.program_id(1)

    def run(parts):
        heads = [slice(g * HEAD_DIM, (g + 1) * HEAD_DIM) for g in range(A_GROUP)]
        ss = _scores(q_ref[:, heads[0]], parts)
        for g, cs in enumerate(heads):
            ss_next = _scores(q_ref[:, heads[g + 1]], parts) if g + 1 < A_GROUP else None
            o_ref[:, cs] = _softmax_pv(ss, parts).astype(BF16)
            ss = ss_next

    @pl.when(qi < n_lat_tiles)
    def _():
        run([(kx_ref, vx_ref), (kc_ref, vc_ref)])

    @pl.when(qi >= n_lat_tiles)
    def _():
        run([(kc_ref, vc_ref)])


def _gqa(qkv, *, nbatch, seq, ctx_len):
    n = qkv.shape[0]
    tq = _pick((seq, ctx_len), (256, 128))
    n_lat_tiles = nbatch * seq // tq
    lat_per_seq, ctx_per_seq = seq // tq, ctx_len // tq
    a_w = A_HEADS * HEAD_DIM
    gw = A_GROUP * HEAD_DIM
    k_col0 = a_w // HEAD_DIM
    v_col0 = k_col0 + A_KV_HEADS

    def batch(qi):
        return jnp.where(qi < n_lat_tiles, qi // lat_per_seq, (qi - n_lat_tiles) // ctx_per_seq)

    def lat_batch(qi):
        return jnp.minimum(qi // lat_per_seq, nbatch - 1)

    ctx_blk0 = nbatch * seq // ctx_len
    return pl.pallas_call(
        functools.partial(_gqa_kernel, n_lat_tiles=n_lat_tiles),
        grid=(A_KV_HEADS, n // tq),
        in_specs=[
            pl.BlockSpec((tq, gw), lambda h, qi: (qi, h)),
            pl.BlockSpec((seq, HEAD_DIM), lambda h, qi: (lat_batch(qi), k_col0 + h)),
            pl.BlockSpec((seq, HEAD_DIM), lambda h, qi: (lat_batch(qi), v_col0 + h)),
            pl.BlockSpec((ctx_len, HEAD_DIM), lambda h, qi: (ctx_blk0 + batch(qi), k_col0 + h)),
            pl.BlockSpec((ctx_len, HEAD_DIM), lambda h, qi: (ctx_blk0 + batch(qi), v_col0 + h)),
        ],
        out_specs=pl.BlockSpec((tq, gw), lambda h, qi: (qi, h)),
        out_shape=jax.ShapeDtypeStruct((n, a_w), BF16),
        compiler_params=_params("arbitrary", "arbitrary"),
        name="gqa_attn",
    )(qkv, qkv, qkv, qkv, qkv)


def _conv_kernel(x_ref, xp_ref, xn_ref, dw_ref, db_ref, g_ref, b_ref, o_ref, buf_ref, y_ref, shift_ref, *,
                 tt, halo, n_lat_tiles, lat_per_seq, ctx_per_seq):
    i = pl.program_id(0)
    taps = dw_ref.shape[0]
    pad = taps // 2
    idx = jnp.where(i < n_lat_tiles, i % lat_per_seq, (i - n_lat_tiles) % ctx_per_seq)
    per_seq = jnp.where(i < n_lat_tiles, lat_per_seq, ctx_per_seq)
    buf_ref[0:halo, :] = jnp.where(idx > 0, xp_ref[...], 0.0)
    buf_ref[halo:halo + tt, :] = x_ref[...]
    buf_ref[halo + tt:, :] = jnp.where(idx < per_seq - 1, xn_ref[...], 0.0)
    nchunk = x_ref.shape[1] // V7X_LANES

    rows = tt + 2 * halo

    def chunk(c, carry):
        cs = pl.ds(pl.multiple_of(c * V7X_LANES, V7X_LANES), V7X_LANES)
        xb = buf_ref[:, cs]
        for s in range(1, V7X_F32_SUBLANES):
            shift_ref[s] = pltpu.roll(xb, rows - s, 0)
        acc = jnp.broadcast_to(db_ref[:, cs], (tt, V7X_LANES))
        for k in range(taps):
            off = halo - pad + k
            base, s = off - off % V7X_F32_SUBLANES, off % V7X_F32_SUBLANES
            win = buf_ref[pl.ds(base, tt), cs] if s == 0 else shift_ref[s, base:base + tt, :]
            acc = acc + win * dw_ref[pl.ds(k, 1), cs]
        y_ref[:, cs] = acc
        return carry

    lax.fori_loop(0, nchunk, chunk, 0)
    yn = _layer_norm(y_ref[...], g_ref[...], b_ref[...])
    o_ref[...] = (yn * _sigmoid(yn)).astype(BF16)


def _conformer_conv(glu, dw_w, dw_b, n_g, n_b, *, nbatch, seq, ctx_len):
    n, cw = glu.shape
    tt = _pick((seq, ctx_len), (256, 128))
    halo = pl.cdiv(dw_w.shape[0] // 2, V7X_F32_SUBLANES) * V7X_F32_SUBLANES
    n_lat_tiles = nbatch * seq // tt
    hb = tt // halo
    nhalo = n // halo
    return pl.pallas_call(
        functools.partial(_conv_kernel, tt=tt, halo=halo, n_lat_tiles=n_lat_tiles,
                          lat_per_seq=seq // tt, ctx_per_seq=ctx_len // tt),
        grid=(n // tt,),
        in_specs=[
            pl.BlockSpec((tt, cw), lambda i: (i, 0)),
            pl.BlockSpec((halo, cw), lambda i: (jnp.maximum(i * hb - 1, 0), 0)),
            pl.BlockSpec((halo, cw), lambda i: (jnp.minimum((i + 1) * hb, nhalo - 1), 0)),
            pl.BlockSpec(dw_w.shape, lambda i: (0, 0)),
            pl.BlockSpec((1, cw), lambda i: (0, 0)),
            pl.BlockSpec((1, cw), lambda i: (0, 0)),
            pl.BlockSpec((1, cw), lambda i: (0, 0)),
        ],
        out_specs=pl.BlockSpec((tt, cw), lambda i: (i, 0)),
        out_shape=jax.ShapeDtypeStruct((n, cw), BF16),
        scratch_shapes=[pltpu.VMEM((tt + 2 * halo, cw), F32), pltpu.VMEM((tt, cw), F32),
                        pltpu.VMEM((V7X_F32_SUBLANES, tt + 2 * halo, V7X_LANES), F32)],
        compiler_params=_params("arbitrary"),
        name="conformer_conv",
    )(glu, glu, glu, dw_w, dw_b.reshape(1, cw), n_g.reshape(1, cw), n_b.reshape(1, cw))


def _outproj_kernel(*refs, n_parts, n_res, n_lat_tiles, alpha):
    part_refs = refs[:n_parts]
    res_refs = refs[n_parts + 1:n_parts + 1 + n_res]
    w_ref = refs[n_parts]
    gate_ref, sh_ref, sc_ref, lg_ref, lb_ref, x1_ref, h2_ref = refs[n_parts + 1 + n_res:]
    tm = x1_ref.shape[0]
    rb = tm // 2 if tm % (2 * V7X_BF16_SUBLANES) == 0 else tm

    def proj(r0):
        y = None
        k0 = 0
        for p_ref in part_refs:
            kw = p_ref.shape[1]
            t = _dot(p_ref[r0:r0 + rb, :], w_ref[k0:k0 + kw, :])
            y = t if y is None else y + t
            k0 += kw
        return y

    def finish(x_ref, r0, y):
        rs = slice(r0, r0 + rb)
        x1 = _layer_norm(alpha * x_ref[rs, :] + gate_ref[0] * y, lg_ref[...], lb_ref[...])
        x1_ref[rs, :] = x1
        h2_ref[rs, :] = (x1 * (1.0 + sc_ref[0]) + sh_ref[0]).astype(BF16)

    def body(x_ref):
        pending = proj(0)
        for r0 in range(0, tm, rb):
            upcoming = proj(r0 + rb) if r0 + rb < tm else None
            finish(x_ref, r0, pending)
            pending = upcoming

    if n_res == 1:
        body(res_refs[0])
    else:
        is_lat = pl.program_id(0) < n_lat_tiles
        pl.when(is_lat)(functools.partial(body, res_refs[0]))
        pl.when(jnp.logical_not(is_lat))(functools.partial(body, res_refs[1]))


def _outproj(parts, w_out, residuals, mod3, mod_base, ln_g, ln_b, *, n_out, tm, n_lat_tiles, seq, nbatch, alpha):
    d = residuals[0].shape[1]
    if len(residuals) == 1:
        res_specs = [pl.BlockSpec((tm, d), lambda i: (i, 0))]
    else:
        res_specs = [pl.BlockSpec((tm, d), lambda i: (jnp.minimum(i, n_lat_tiles - 1), 0)),
                     pl.BlockSpec((tm, d), lambda i: (jnp.maximum(i - n_lat_tiles, 0), 0))]

    def cond_row(i):
        return jnp.where(i < n_lat_tiles, (i * tm) // seq, nbatch)

    def mod_spec(k):
        return pl.BlockSpec((1, 1, d), lambda i: (mod_base + cond_row(i) * 6 + k, 0, 0))

    return pl.pallas_call(
        functools.partial(_outproj_kernel, n_parts=len(parts), n_res=len(residuals), n_lat_tiles=n_lat_tiles,
                          alpha=alpha),
        grid=(n_out // tm,),
        in_specs=[pl.BlockSpec((tm, p.shape[1]), lambda i: (i, 0)) for p in parts] + [
            pl.BlockSpec(w_out.shape, lambda i: (0, 0), pipeline_mode=pl.Buffered(1))] + res_specs + [
            mod_spec(2), mod_spec(3), mod_spec(4),
            pl.BlockSpec((1, d), lambda i: (0, 0)),
            pl.BlockSpec((1, d), lambda i: (0, 0)),
        ],
        out_specs=[pl.BlockSpec((tm, d), lambda i: (i, 0)), pl.BlockSpec((tm, d), lambda i: (i, 0))],
        out_shape=[jax.ShapeDtypeStruct((n_out, d), F32), jax.ShapeDtypeStruct((n_out, d), BF16)],
        compiler_params=_params("arbitrary"),
        name="outproj_ln",
    )(*parts, w_out, *residuals, mod3, mod3, mod3, ln_g.reshape(1, d), ln_b.reshape(1, d))


def _ffn_kernel(h_ref, hp_ref, hn_ref, wa_ref, wg_ref, dwa_ref, dwg_ref, dba_ref, dbg_ref, wd_ref,
                x_ref, gate_ref, lg_ref, lb_ref, o_ref,
                hext_ref, za0_ref, za1_ref, zg0_ref, zg1_ref, acc_ref, *,
                tm, halo, sub, nf, seq, ctx_len, n_lat_tiles, alpha):
    za_refs = (za0_ref, za1_ref)
    zg_refs = (zg0_ref, zg1_ref)
    i = pl.program_id(0)
    c = pl.program_id(1)
    is_lat = i < n_lat_tiles
    row = lax.broadcasted_iota(jnp.int32, (tm, 1), 0) + i * tm
    pos = row & (ctx_len - 1)
    has_prev = pos != 0
    has_next = pos != ctx_len - 1

    chunks = [slice(s0, s0 + sub) for s0 in range(0, wa_ref.shape[1], sub)]

    def up_proj(slot):
        za_refs[slot][...] = _dot(hext_ref[...], wa_ref[...])
        zg_refs[slot][...] = _dot(hext_ref[...], wg_ref[...])

    def conv(z_ref, dw_ref, db_ref, cs, masked):
        z = z_ref[:, cs]
        rows_ext = tm + 2 * halo
        z_prev = pltpu.roll(z, 1, 0)[halo:halo + tm]
        z_next = pltpu.roll(z, rows_ext - 1, 0)[halo:halo + tm]
        if masked:
            z_prev = jnp.where(has_prev, z_prev, 0.0)
            z_next = jnp.where(has_next, z_next, 0.0)
        return z_prev * dw_ref[0:1, cs] + z[halo:halo + tm] * dw_ref[1:2, cs] + z_next * dw_ref[2:3, cs] + db_ref[:, cs]

    def step(up_slot, down_slot, masked=False, first=False):
        if up_slot is not None:
            up_proj(up_slot)
        if down_slot is not None:
            contrib = None
            for cs in chunks:
                a = conv(za_refs[down_slot], dwa_ref, dba_ref, cs, masked)
                g = conv(zg_refs[down_slot], dwg_ref, dbg_ref, cs, masked)
                t = _dot_t(wd_ref[:, cs], (g * _sigmoid(g) * a).astype(BF16))
                contrib = t if contrib is None else contrib + t
            if first:
                acc_ref[...] = contrib
            else:
                acc_ref[...] += contrib

    @pl.when(c == 0)
    def _():
        starts_seq = is_lat & ((i * tm) % seq == 0)
        ends_seq = is_lat & (((i + 1) * tm) % seq == 0)
        zero_halo = jnp.zeros(hp_ref.shape, hp_ref.dtype)
        hext_ref[0:halo, :] = jnp.where(starts_seq, zero_halo, hp_ref[...])
        hext_ref[halo:halo + tm, :] = h_ref[...]
        hext_ref[halo + tm:, :] = jnp.where(ends_seq, zero_halo, hn_ref[...])
        step(0, None)

    assert nf >= 2
    for masked in (False, True):
        @pl.when((c == 1) & (is_lat != masked))
        def _():
            step(1, 0, masked, first=True)

        for slot in (0, 1):
            @pl.when((c >= 2) & (c < nf) & (c % 2 == slot) & (is_lat != masked))
            def _():
                step(slot, 1 - slot, masked)

        @pl.when((c == nf) & (is_lat != masked))
        def _():
            step(None, (nf - 1) % 2, masked)

    @pl.when(c == nf)
    def _():
        o_ref[...] = _layer_norm(alpha * x_ref[...] + gate_ref[0] * acc_ref[...].T, lg_ref[...], lb_ref[...])


def _ffn(h2, x1, layer, w_up, dw_w, dw_b, w_down, mod3, mod_base, ln_g, ln_b, *, tm, n_lat_tiles, seq, ctx_len,
         nbatch, alpha):
    n_out, d = x1.shape
    depth, _, d_ff = w_down.shape
    tf = _pick((d_ff,), (512, 256, 128))
    nf = d_ff // tf
    halo = V7X_BF16_SUBLANES
    hb = tm // halo
    nhalo = h2.shape[0] // halo
    ntiles = n_out // tm
    assert dw_w.shape[1] == 3

    def cond_row(i):
        return jnp.where(i < n_lat_tiles, (i * tm) // seq, nbatch)

    dw_b3 = dw_b.reshape(depth, 1, 2 * d_ff)

    def up(c):
        return jnp.minimum(c, nf - 1)

    def down(c):
        return jnp.maximum(c - 1, 0)

    return pl.pallas_call(
        functools.partial(_ffn_kernel, tm=tm, halo=halo, sub=min(tf, V7X_MXU_DIM), nf=nf, seq=seq, ctx_len=ctx_len,
                          n_lat_tiles=n_lat_tiles, alpha=alpha),
        grid=(ntiles, nf + 1),
        in_specs=[
            pl.BlockSpec((tm, d), lambda i, c: (i, 0)),
            pl.BlockSpec((halo, d), lambda i, c: (jnp.maximum(i * hb - 1, 0), 0)),
            pl.BlockSpec((halo, d), lambda i, c: (jnp.minimum((i + 1) * hb, nhalo - 1), 0)),
            pl.BlockSpec((None, d, tf), lambda i, c: (layer, 0, up(c))),
            pl.BlockSpec((None, d, tf), lambda i, c: (layer, 0, nf + up(c))),
            pl.BlockSpec((None, 3, tf), lambda i, c: (layer, 0, down(c))),
            pl.BlockSpec((None, 3, tf), lambda i, c: (layer, 0, nf + down(c))),
            pl.BlockSpec((None, 1, tf), lambda i, c: (layer, 0, down(c))),
            pl.BlockSpec((None, 1, tf), lambda i, c: (layer, 0, nf + down(c))),
            pl.BlockSpec((None, d, tf), lambda i, c: (layer, 0, down(c))),
            pl.BlockSpec((tm, d), lambda i, c: (i, 0)),
            pl.BlockSpec((1, 1, d), lambda i, c: (mod_base + cond_row(i) * 6 + 5, 0, 0)),
            pl.BlockSpec((1, d), lambda i, c: (0, 0)),
            pl.BlockSpec((1, d), lambda i, c: (0, 0)),
        ],
        out_specs=pl.BlockSpec((tm, d), lambda i, c: (i, 0)),
        out_shape=jax.ShapeDtypeStruct((n_out, d), F32),
        scratch_shapes=([pltpu.VMEM((tm + 2 * halo, d), BF16)] + [pltpu.VMEM((tm + 2 * halo, tf), F32)] * 4
                        + [pltpu.VMEM((d, tm), F32)]),
        compiler_params=_params("arbitrary", "arbitrary"),
        name="conv_ffn",
    )(h2, h2, h2, w_up, w_up, dw_w, dw_w, dw_b3, dw_b3, w_down, x1, mod3, ln_g.reshape(1, d), ln_b.reshape(1, d))


def _inproj_c_kernel(x_ref, sh_ref, sc_ref, w_ref, o_ref, *, q_scale):
    hb = (x_ref[...] * (1.0 + sc_ref[0]) + sh_ref[0]).astype(BF16)
    d = x_ref.shape[1]
    cols = PROJ_COLS
    for c0 in range(0, o_ref.shape[1], cols):
        z = _dot(hb, w_ref[:, c0:c0 + cols])
        if c0 < d:
            z = z * q_scale
        o_ref[:, c0:c0 + cols] = z.astype(BF16)


def _inproj_c(xc, mod3, mod_base, w_in, *, tm, n_lat_tiles, seq, nbatch):
    n, d = xc.shape
    n_in = w_in.shape[1]
    assert d % PROJ_COLS == 0 and n_in == 3 * d

    def cond_row(i):
        return jnp.where(i < n_lat_tiles, (i * tm) // seq, nbatch)

    return pl.pallas_call(
        functools.partial(_inproj_c_kernel, q_scale=ATTN_Q_SCALE),
        grid=(n // tm,),
        in_specs=[
            pl.BlockSpec((tm, d), lambda i: (i, 0)),
            pl.BlockSpec((1, 1, d), lambda i: (mod_base + cond_row(i) * 6 + 0, 0, 0)),
            pl.BlockSpec((1, 1, d), lambda i: (mod_base + cond_row(i) * 6 + 1, 0, 0)),
            pl.BlockSpec((d, n_in), lambda i: (0, 0), pipeline_mode=pl.Buffered(1)),
        ],
        out_specs=pl.BlockSpec((tm, n_in), lambda i: (i, 0)),
        out_shape=jax.ShapeDtypeStruct((n, n_in), BF16),
        compiler_params=_params("arbitrary"),
        name="inproj_c",
    )(xc, mod3, mod3, w_in)


def _na_bias_table(rel_bias):
    nheads, nrow, ncol = rel_bias.shape
    qc = np.arange(GRID_W)[:, None]
    kc = np.arange(GRID_W)[None, :]
    cstart = np.clip(qc - NA_COLS // 2, 0, GRID_W - NA_COLS)
    valid = (kc >= cstart) & (kc < cstart + NA_COLS)
    period = 2 * GRID_W - 1
    lead = GRID_W - NA_COLS - 1
    padded = jnp.pad(rel_bias, ((0, 0), (0, 0), (lead, period - lead - ncol)), constant_values=NEG)
    tiled = jnp.broadcast_to(padded[:, :, None, :], (nheads, nrow, GRID_W, period))
    skew = tiled.reshape(nheads, nrow, GRID_W * period)[:, :, :GRID_W * (period - 1)]
    toep = skew.reshape(nheads, nrow, GRID_W, period - 1)[:, :, :, GRID_W - 2:]
    toep = jnp.where(valid[None, None], toep * LOG2E, NEG)
    neg = jnp.full((nheads, 1, GRID_W, GRID_W), NEG, F32)
    ext = jnp.concatenate([neg, toep, neg], axis=1)
    return jnp.concatenate([ext[:, :2 * NA_ROWS], ext[:, 1:]], axis=-1)


def _natten_kernel(q_ref, k_ref, v_ref, kc_ref, vc_ref, bias_ref, o_ref, *, rows):
    nq = NA_QROWS * GRID_W
    nk = NA_KROWS * GRID_W
    left = lax.broadcasted_iota(jnp.int32, (GRID_W, 2 * GRID_W), 1) < GRID_W
    for blk, hh in [(b_, h_) for b_ in range(rows // NA_QROWS) for h_ in range(bias_ref.shape[0])]:
        hs = slice(hh * HEAD_DIM, (hh + 1) * HEAD_DIM)
        ks = min(max(NA_QROWS * blk - NA_ROWS // 2, 0), rows - NA_KROWS)
        q = q_ref[blk * nq:(blk + 1) * nq, hs]
        s_loc = _dot_t(q, k_ref[ks * GRID_W:ks * GRID_W + nk, hs])
        bias_rows = []
        for qi in range(NA_QROWS):
            qr = NA_QROWS * blk + qi
            r0 = min(max(qr - NA_ROWS // 2, 0), rows - NA_ROWS)
            slabs = []
            for j in range(NA_KROWS // 2):
                kr = ks + 2 * j
                ok0 = r0 <= kr < r0 + NA_ROWS
                ok1 = r0 <= kr + 1 < r0 + NA_ROWS
                if not (ok0 or ok1):
                    slabs.append(jnp.full((GRID_W, 2 * GRID_W), NEG, F32))
                    continue
                slab = bias_ref[hh, kr - qr + NA_ROWS]
                if not ok1:
                    slab = jnp.where(left, slab, NEG)
                elif not ok0:
                    slab = jnp.where(left, NEG, slab)
                slabs.append(slab)
            bias_rows.append(jnp.concatenate(slabs, axis=1))
        s_loc = s_loc + jnp.concatenate(bias_rows, axis=0)
        s_ctx = _dot_t(q, kc_ref[:, hs])
        m = jnp.maximum(jnp.max(s_loc, axis=-1, keepdims=True), jnp.max(s_ctx, axis=-1, keepdims=True))
        p_loc = jnp.exp2(s_loc - m)
        p_ctx = jnp.exp2(s_ctx - m)
        o = (_dot(p_loc.astype(BF16), _with_ones(v_ref[ks * GRID_W:ks * GRID_W + nk, hs]))
             + _dot(p_ctx.astype(BF16), _with_ones(vc_ref[:, hs])))
        o_ref[blk * nq:(blk + 1) * nq, hs] = (o[:, :HEAD_DIM] / o[:, HEAD_DIM:HEAD_DIM + 1]).astype(BF16)


def _natten(qkv, bias_tab, *, nbatch, seq, ctx_len, d):
    rows = seq // GRID_W
    nheads = d // HEAD_DIM
    assert rows % NA_QROWS == 0 and rows >= NA_KROWS
    ctx_blk0 = nbatch * seq // ctx_len
    hg = NA_HEADS_PER_STEP
    assert nheads % hg == 0
    ngroups = nheads // hg
    gw = hg * HEAD_DIM
    return pl.pallas_call(
        functools.partial(_natten_kernel, rows=rows),
        grid=(ngroups, nbatch),
        in_specs=[
            pl.BlockSpec((seq, gw), lambda h, b: (b, h)),
            pl.BlockSpec((seq, gw), lambda h, b: (b, ngroups + h)),
            pl.BlockSpec((seq, gw), lambda h, b: (b, 2 * ngroups + h)),
            pl.BlockSpec((ctx_len, gw), lambda h, b: (ctx_blk0 + b, ngroups + h)),
            pl.BlockSpec((ctx_len, gw), lambda h, b: (ctx_blk0 + b, 2 * ngroups + h)),
            pl.BlockSpec((hg,) + bias_tab.shape[1:], lambda h, b: (h, 0, 0, 0)),
        ],
        out_specs=pl.BlockSpec((seq, gw), lambda h, b: (b, h)),
        out_shape=jax.ShapeDtypeStruct((nbatch * seq, d), BF16),
        compiler_params=_params("arbitrary", "arbitrary"),
        name="natten",
    )(qkv, qkv, qkv, qkv, qkv, bias_tab)


def kernel(x, c, ctx, c_ctx, ada_w, ada_b, post_ln_g, post_ln_b, ab_w_in, ab_w_out, ab_q_gain, ab_k_gain,
           ab_dw_w, ab_dw_b, ab_norm_g, ab_norm_b, c_w_in, c_w_out, c_rel_bias, ffn_w_up, ffn_dw_w, ffn_dw_b,
           ffn_w_down):
    nbatch, seq, d = x.shape
    ctx_len = ctx.shape[1]
    depth = ada_w.shape[0]
    assert depth == 2 and nbatch + 1 <= COND_ROWS and nbatch * ctx_len <= seq
    assert seq % GRID_W == 0 and seq & (seq - 1) == 0 and ctx_len & (ctx_len - 1) == 0 and ctx_len <= seq
    nx, nc = nbatch * seq, nbatch * ctx_len
    alpha = (2 * depth) ** 0.25
    tm = _pick((seq, nc), (512, 256))
    n_lat_tiles = nx // tm
    geo = dict(tm=tm, n_lat_tiles=n_lat_tiles, seq=seq, nbatch=nbatch)

    cond = jnp.concatenate([c, c_ctx[None], jnp.zeros((COND_ROWS - nbatch - 1, d), F32)], axis=0)
    mod3 = _ada(cond, ada_w, ada_b).reshape(depth * COND_ROWS * 6, 1, d)
    x2d, c2d = x.reshape(nx, d), ctx.reshape(nc, d)
    ffn_w = (ffn_w_up.astype(BF16), ffn_dw_w, ffn_dw_b, jnp.swapaxes(ffn_w_down, 1, 2).astype(BF16))

    mod_base = 0
    qkv, glu = _inproj_ab(x2d, c2d, mod3, mod_base, ab_w_in[0].astype(BF16), ab_q_gain[0][None],
                          ab_k_gain[0][None], _rope_tables(seq, tm), **geo)
    attn = _gqa(qkv, nbatch=nbatch, seq=seq, ctx_len=ctx_len)
    conv = _conformer_conv(glu, ab_dw_w[0], ab_dw_b[0], ab_norm_g[0], ab_norm_b[0],
                           nbatch=nbatch, seq=seq, ctx_len=ctx_len)
    x1, h2 = _outproj([attn, conv], ab_w_out[0].astype(BF16), [x2d, c2d], mod3, mod_base, post_ln_g[0, 0],
                      post_ln_b[0, 0], n_out=nx + nc, alpha=alpha, **geo)
    xc = _ffn(h2, x1, 0, *ffn_w, mod3, mod_base, post_ln_g[0, 1], post_ln_b[0, 1], ctx_len=ctx_len, alpha=alpha,
              **geo)

    mod_base = COND_ROWS * 6
    qkv = _inproj_c(xc, mod3, mod_base, c_w_in[0].astype(BF16), **geo)
    attn = _natten(qkv, _na_bias_table(c_rel_bias[0]), nbatch=nbatch, seq=seq, ctx_len=ctx_len, d=d)
    x1, h2 = _outproj([attn], c_w_out[0].astype(BF16), [xc], mod3, mod_base, post_ln_g[1, 0], post_ln_b[1, 0],
                      n_out=nx, alpha=alpha, **geo)
    out = _ffn(h2, x1, 1, *ffn_w, mod3, mod_base, post_ln_g[1, 1], post_ln_b[1, 1], ctx_len=ctx_len, alpha=alpha,
               **geo)
    return out.reshape(nbatch, seq, d)
```

```python
import functools

import jax
import jax.numpy as jnp
import numpy as np
from jax import lax
from jax.experimental import pallas as pl
from jax.experimental.pallas import tpu as pltpu

F32 = jnp.float32
BF16 = jnp.bfloat16

HEAD_DIM = 128
GRID_W = 64
A_HEADS = 8
A_KV_HEADS = 2
A_GROUP = A_HEADS // A_KV_HEADS
ROPE_THETA = 10000.0
NA_ROWS = 8
NA_COLS = 16
LN_EPS = 1e-6
NEG = -1e30
LOG2E = 1.4426950408889634
ATTN_Q_SCALE = HEAD_DIM ** -0.5 * LOG2E

V7X_LANES = 128
V7X_F32_SUBLANES = 8
V7X_BF16_SUBLANES = 16
V7X_MXU_DIM = 256
V7X_VMEM_LIMIT = 56 * 1024 * 1024
PROJ_COLS = 2 * V7X_MXU_DIM
COND_ROWS = 8

NA_QROWS = 4
NA_HEADS_PER_STEP = 4
NA_KROWS = NA_QROWS + NA_ROWS


def _pick(n, candidates):
    for t in candidates:
        if all(v % t == 0 for v in n):
            return t
    raise ValueError(f"no tile in {candidates} divides {n}")


def _params(*sem):
    return pltpu.CompilerParams(dimension_semantics=sem, vmem_limit_bytes=V7X_VMEM_LIMIT)


def _layer_norm(r, g, b):
    mu = jnp.mean(r, axis=-1, keepdims=True)
    d = r - mu
    var = jnp.mean(d * d, axis=-1, keepdims=True)
    return d * lax.rsqrt(var + LN_EPS) * g + b


def _sigmoid(v):
    return 1.0 / (1.0 + jnp.exp(-v))


def _dot(a, b):
    return jnp.dot(a, b, preferred_element_type=F32)


def _dot_t(a, b):
    return lax.dot_general(a, b, (((1,), (1,)), ((), ())), preferred_element_type=F32)


def _ada_kernel(cond_ref, w_ref, b_ref, o_ref):
    cnd = cond_ref[...]
    s = (cnd * _sigmoid(cnd)).astype(BF16)
    o_ref[0] = _dot(s, w_ref[0].astype(BF16)) + b_ref[0]


def _ada(cond, ada_w, ada_b):
    depth, d, n6 = ada_w.shape
    tn = _pick((n6,), (1024, 512, 256, 128))
    return pl.pallas_call(
        _ada_kernel,
        grid=(depth, n6 // tn),
        in_specs=[
            pl.BlockSpec((COND_ROWS, d), lambda l, j: (0, 0)),
            pl.BlockSpec((1, d, tn), lambda l, j: (l, 0, j)),
            pl.BlockSpec((1, 1, tn), lambda l, j: (l, 0, j)),
        ],
        out_specs=pl.BlockSpec((1, COND_ROWS, tn), lambda l, j: (l, 0, j)),
        out_shape=jax.ShapeDtypeStruct((depth, COND_ROWS, n6), F32),
        compiler_params=_params("arbitrary", "arbitrary"),
        name="ada_mod",
    )(cond, ada_w, ada_b.reshape(depth, 1, n6))


def _inproj_ab_kernel(x_ref, c_ref, sh_ref, sc_ref, w_ref, qg_ref, kg_ref, rope_ref, qkv_ref, glu_ref, hb_ref, *,
                      q_scale, n_lat_tiles):
    is_lat = pl.program_id(0) < n_lat_tiles
    for src_ref, cond in ((x_ref, is_lat), (c_ref, jnp.logical_not(is_lat))):
        @pl.when(cond)
        def _():
            hb_ref[...] = (src_ref[...] * (1.0 + sc_ref[0]) + sh_ref[0]).astype(BF16)
    hb = hb_ref[...]
    cos, sin_lo, sin_hi = rope_ref[0], rope_ref[1], rope_ref[2]
    a_w = A_HEADS * HEAD_DIM
    kv_w = A_KV_HEADS * HEAD_DIM
    b_w = glu_ref.shape[1]

    def norm_rope(z, gain):
        zn = z * lax.rsqrt(jnp.mean(z * z, axis=-1, keepdims=True) + LN_EPS) * gain
        quarter = HEAD_DIM // 4
        return (zn * cos + pltpu.roll(zn, HEAD_DIM - quarter, 1) * sin_lo
                + pltpu.roll(zn, quarter, 1) * sin_hi)

    q_gain = qg_ref[...] * q_scale
    heads_per_dot = PROJ_COLS // HEAD_DIM
    for c0 in range(0, a_w, heads_per_dot * HEAD_DIM):
        z = _dot(hb, w_ref[:, c0:c0 + heads_per_dot * HEAD_DIM])
        for hh in range(heads_per_dot):
            cs = slice(c0 + hh * HEAD_DIM, c0 + (hh + 1) * HEAD_DIM)
            qkv_ref[:, cs] = norm_rope(z[:, hh * HEAD_DIM:(hh + 1) * HEAD_DIM], q_gain).astype(BF16)
    z = _dot(hb, w_ref[:, a_w:a_w + 2 * kv_w])
    for hh in range(A_KV_HEADS):
        cs = slice(a_w + hh * HEAD_DIM, a_w + (hh + 1) * HEAD_DIM)
        qkv_ref[:, cs] = norm_rope(z[:, hh * HEAD_DIM:(hh + 1) * HEAD_DIM], kg_ref[...]).astype(BF16)
    qkv_ref[:, a_w + kv_w:a_w + 2 * kv_w] = z[:, kv_w:].astype(BF16)
    u0 = a_w + 2 * kv_w
    glu_cols = PROJ_COLS if b_w % PROJ_COLS == 0 else b_w
    for c0 in range(0, b_w, glu_cols):
        u = _dot(hb, w_ref[:, u0 + c0:u0 + c0 + glu_cols])
        g = _dot(hb, w_ref[:, u0 + b_w + c0:u0 + b_w + c0 + glu_cols])
        glu_ref[:, c0:c0 + glu_cols] = u * _sigmoid(g)


def _inproj_ab(x2d, c2d, mod3, mod_base, w_in, q_gain, k_gain, rope, *, tm, n_lat_tiles, seq, nbatch):
    d = x2d.shape[1]
    n = x2d.shape[0] + c2d.shape[0]
    n_in = w_in.shape[1]
    a_w, kv_w = A_HEADS * HEAD_DIM, A_KV_HEADS * HEAD_DIM
    b_w = (n_in - a_w - 2 * kv_w) // 2
    tiles_per_seq = seq // tm

    def cond_row(i):
        return jnp.where(i < n_lat_tiles, (i * tm) // seq, nbatch)

    def rope_blk(i):
        return jnp.where(i < n_lat_tiles, i % tiles_per_seq, tiles_per_seq)

    return pl.pallas_call(
        functools.partial(_inproj_ab_kernel, q_scale=ATTN_Q_SCALE, n_lat_tiles=n_lat_tiles),
        grid=(n // tm,),
        in_specs=[
            pl.BlockSpec((tm, d), lambda i: (jnp.minimum(i, n_lat_tiles - 1), 0)),
            pl.BlockSpec((tm, d), lambda i: (jnp.maximum(i - n_lat_tiles, 0), 0)),
            pl.BlockSpec((1, 1, d), lambda i: (mod_base + cond_row(i) * 6 + 0, 0, 0)),
            pl.BlockSpec((1, 1, d), lambda i: (mod_base + cond_row(i) * 6 + 1, 0, 0)),
            pl.BlockSpec((d, n_in), lambda i: (0, 0), pipeline_mode=pl.Buffered(1)),
            pl.BlockSpec((1, HEAD_DIM), lambda i: (0, 0)),
            pl.BlockSpec((1, HEAD_DIM), lambda i: (0, 0)),
            pl.BlockSpec((3, tm, HEAD_DIM), lambda i: (0, rope_blk(i), 0)),
        ],
        out_specs=[
            pl.BlockSpec((tm, a_w + 2 * kv_w), lambda i: (i, 0)),
            pl.BlockSpec((tm, b_w), lambda i: (i, 0)),
        ],
        out_shape=[
            jax.ShapeDtypeStruct((n, a_w + 2 * kv_w), BF16),
            jax.ShapeDtypeStruct((n, b_w), F32),
        ],
        scratch_shapes=[pltpu.VMEM((tm, d), BF16)],
        compiler_params=_params("arbitrary"),
        name="inproj_ab",
    )(x2d, c2d, mod3, mod3, w_in, q_gain, k_gain, rope)


def _rope_tables(seq, tm):
    t = jnp.arange(seq)
    nfreq = HEAD_DIM // 4
    inv = ROPE_THETA ** (-jnp.arange(nfreq, dtype=F32) / nfreq)
    ang_r = (t // GRID_W).astype(F32)[:, None] * inv
    ang_c = (t % GRID_W).astype(F32)[:, None] * inv
    ang = jnp.concatenate([ang_r, ang_r, ang_c, ang_c], axis=-1)
    cos, sin = jnp.cos(ang), jnp.sin(ang)
    low = (np.arange(HEAD_DIM) // nfreq) % 2 == 0
    sin_lo = jnp.where(low, -sin, 0.0)
    sin_hi = jnp.where(low, 0.0, sin)
    tab = jnp.stack([cos, sin_lo, sin_hi])
    ident = jnp.stack([jnp.ones((tm, HEAD_DIM), F32), jnp.zeros((tm, HEAD_DIM), F32), jnp.zeros((tm, HEAD_DIM), F32)])
    return jnp.concatenate([tab, ident], axis=1)


def _with_ones(v):
    return jnp.concatenate([v, jnp.ones(v.shape, v.dtype)], axis=1)


def _scores(q, parts):
    return [_dot_t(q, k_ref[...]) for k_ref, _ in parts]


def _softmax_pv(ss, parts):
    m = functools.reduce(jnp.maximum, [jnp.max(s, axis=-1, keepdims=True) for s in ss])
    ps = [jnp.exp2(s - m) for s in ss]
    o = functools.reduce(jnp.add, [_dot(p.astype(BF16), _with_ones(v_ref[...])) for p, (_, v_ref) in zip(ps, parts)])
    return o[:, :HEAD_DIM] / o[:, HEAD_DIM:HEAD_DIM + 1]


def _gqa_kernel(q_ref, kx_ref, vx_ref, kc_ref, vc_ref, o_ref, *, n_lat_tiles):
    qi = pl.program_id(1)

    def run(parts):
        heads = [slice(g * HEAD_DIM, (g + 1) * HEAD_DIM) for g in range(A_GROUP)]
        ss = _scores(q_ref[:, heads[0]], parts)
        for g, cs in enumerate(heads):
            ss_next = _scores(q_ref[:, heads[g + 1]], parts) if g + 1 < A_GROUP else None
            o_ref[:, cs] = _softmax_pv(ss, parts).astype(BF16)
            ss = ss_next

    @pl.when(qi < n_lat_tiles)
    def _():
        run([(kx_ref, vx_ref), (kc_ref, vc_ref)])

    @pl.when(qi >= n_lat_tiles)
    def _():
        run([(kc_ref, vc_ref)])


def _gqa(qkv, *, nbatch, seq, ctx_len):
    n = qkv.shape[0]
    tq = _pick((seq, ctx_len), (256, 128))
    n_lat_tiles = nbatch * seq // tq
    lat_per_seq, ctx_per_seq = seq // tq, ctx_len // tq
    a_w = A_HEADS * HEAD_DIM
    gw = A_GROUP * HEAD_DIM
    k_col0 = a_w // HEAD_DIM
    v_col0 = k_col0 + A_KV_HEADS

    def batch(qi):
        return jnp.where(qi < n_lat_tiles, qi // lat_per_seq, (qi - n_lat_tiles) // ctx_per_seq)

    def lat_batch(qi):
        return jnp.minimum(qi // lat_per_seq, nbatch - 1)

    ctx_blk0 = nbatch * seq // ctx_len
    return pl.pallas_call(
        functools.partial(_gqa_kernel, n_lat_tiles=n_lat_tiles),
        grid=(A_KV_HEADS, n // tq),
        in_specs=[
            pl.BlockSpec((tq, gw), lambda h, qi: (qi, h)),
            pl.BlockSpec((seq, HEAD_DIM), lambda h, qi: (lat_batch(qi), k_col0 + h)),
            pl.BlockSpec((seq, HEAD_DIM), lambda h, qi: (lat_batch(qi), v_col0 + h)),
            pl.BlockSpec((ctx_len, HEAD_DIM), lambda h, qi: (ctx_blk0 + batch(qi), k_col0 + h)),
            pl.BlockSpec((ctx_len, HEAD_DIM), lambda h, qi: (ctx_blk0 + batch(qi), v_col0 + h)),
        ],
        out_specs=pl.BlockSpec((tq, gw), lambda h, qi: (qi, h)),
        out_shape=jax.ShapeDtypeStruct((n, a_w), BF16),
        compiler_params=_params("arbitrary", "arbitrary"),
        name="gqa_attn",
    )(qkv, qkv, qkv, qkv, qkv)


def _conv_kernel(x_ref, xp_ref, xn_ref, dw_ref, db_ref, g_ref, b_ref, o_ref, buf_ref, y_ref, shift_ref, *,
                 tt, halo, n_lat_tiles, lat_per_seq, ctx_per_seq):
    i = pl.program_id(0)
    taps = dw_ref.shape[0]
    pad = taps // 2
    idx = jnp.where(i < n_lat_tiles, i % lat_per_seq, (i - n_lat_tiles) % ctx_per_seq)
    per_seq = jnp.where(i < n_lat_tiles, lat_per_seq, ctx_per_seq)
    buf_ref[0:halo, :] = jnp.where(idx > 0, xp_ref[...], 0.0)
    buf_ref[halo:halo + tt, :] = x_ref[...]
    buf_ref[halo + tt:, :] = jnp.where(idx < per_seq - 1, xn_ref[...], 0.0)
    nchunk = x_ref.shape[1] // V7X_LANES

    rows = tt + 2 * halo

    def chunk(c, carry):
        cs = pl.ds(pl.multiple_of(c * V7X_LANES, V7X_LANES), V7X_LANES)
        xb = buf_ref[:, cs]
        for s in range(1, V7X_F32_SUBLANES):
            shift_ref[s] = pltpu.roll(xb, rows - s, 0)
        acc = jnp.broadcast_to(db_ref[:, cs], (tt, V7X_LANES))
        for k in range(taps):
            off = halo - pad + k
            base, s = off - off % V7X_F32_SUBLANES, off % V7X_F32_SUBLANES
            win = buf_ref[pl.ds(base, tt), cs] if s == 0 else shift_ref[s, base:base + tt, :]
            acc = acc + win * dw_ref[pl.ds(k, 1), cs]
        y_ref[:, cs] = acc
        return carry

    lax.fori_loop(0, nchunk, chunk, 0)
    yn = _layer_norm(y_ref[...], g_ref[...], b_ref[...])
    o_ref[...] = (yn * _sigmoid(yn)).astype(BF16)


def _conformer_conv(glu, dw_w, dw_b, n_g, n_b, *, nbatch, seq, ctx_len):
    n, cw = glu.shape
    tt = _pick((seq, ctx_len), (256, 128))
    halo = pl.cdiv(dw_w.shape[0] // 2, V7X_F32_SUBLANES) * V7X_F32_SUBLANES
    n_lat_tiles = nbatch * seq // tt
    hb = tt // halo
    nhalo = n // halo
    return pl.pallas_call(
        functools.partial(_conv_kernel, tt=tt, halo=halo, n_lat_tiles=n_lat_tiles,
                          lat_per_seq=seq // tt, ctx_per_seq=ctx_len // tt),
        grid=(n // tt,),
        in_specs=[
            pl.BlockSpec((tt, cw), lambda i: (i, 0)),
            pl.BlockSpec((halo, cw), lambda i: (jnp.maximum(i * hb - 1, 0), 0)),
            pl.BlockSpec((halo, cw), lambda i: (jnp.minimum((i + 1) * hb, nhalo - 1), 0)),
            pl.BlockSpec(dw_w.shape, lambda i: (0, 0)),
            pl.BlockSpec((1, cw), lambda i: (0, 0)),
            pl.BlockSpec((1, cw), lambda i: (0, 0)),
            pl.BlockSpec((1, cw), lambda i: (0, 0)),
        ],
        out_specs=pl.BlockSpec((tt, cw), lambda i: (i, 0)),
        out_shape=jax.ShapeDtypeStruct((n, cw), BF16),
        scratch_shapes=[pltpu.VMEM((tt + 2 * halo, cw), F32), pltpu.VMEM((tt, cw), F32),
                        pltpu.VMEM((V7X_F32_SUBLANES, tt + 2 * halo, V7X_LANES), F32)],
        compiler_params=_params("arbitrary"),
        name="conformer_conv",
    )(glu, glu, glu, dw_w, dw_b.reshape(1, cw), n_g.reshape(1, cw), n_b.reshape(1, cw))


def _outproj_kernel(*refs, n_parts, n_res, n_lat_tiles, alpha):
    part_refs = refs[:n_parts]
    res_refs = refs[n_parts + 1:n_parts + 1 + n_res]
    w_ref = refs[n_parts]
    gate_ref, sh_ref, sc_ref, lg_ref, lb_ref, x1_ref, h2_ref = refs[n_parts + 1 + n_res:]
    tm = x1_ref.shape[0]
    rb = tm // 2 if tm % (2 * V7X_BF16_SUBLANES) == 0 else tm

    def proj(r0):
        y = None
        k0 = 0
        for p_ref in part_refs:
            kw = p_ref.shape[1]
            t = _dot(p_ref[r0:r0 + rb, :], w_ref[k0:k0 + kw, :])
            y = t if y is None else y + t
            k0 += kw
        return y

    def finish(x_ref, r0, y):
        rs = slice(r0, r0 + rb)
        x1 = _layer_norm(alpha * x_ref[rs, :] + gate_ref[0] * y, lg_ref[...], lb_ref[...])
        x1_ref[rs, :] = x1
        h2_ref[rs, :] = (x1 * (1.0 + sc_ref[0]) + sh_ref[0]).astype(BF16)

    def body(x_ref):
        pending = proj(0)
        for r0 in range(0, tm, rb):
            upcoming = proj(r0 + rb) if r0 + rb < tm else None
            finish(x_ref, r0, pending)
            pending = upcoming

    if n_res == 1:
        body(res_refs[0])
    else:
        is_lat = pl.program_id(0) < n_lat_tiles
        pl.when(is_lat)(functools.partial(body, res_refs[0]))
        pl.when(jnp.logical_not(is_lat))(functools.partial(body, res_refs[1]))


def _outproj(parts, w_out, residuals, mod3, mod_base, ln_g, ln_b, *, n_out, tm, n_lat_tiles, seq, nbatch, alpha):
    d = residuals[0].shape[1]
    if len(residuals) == 1:
        res_specs = [pl.BlockSpec((tm, d), lambda i: (i, 0))]
    else:
        res_specs = [pl.BlockSpec((tm, d), lambda i: (jnp.minimum(i, n_lat_tiles - 1), 0)),
                     pl.BlockSpec((tm, d), lambda i: (jnp.maximum(i - n_lat_tiles, 0), 0))]

    def cond_row(i):
        return jnp.where(i < n_lat_tiles, (i * tm) // seq, nbatch)

    def mod_spec(k):
        return pl.BlockSpec((1, 1, d), lambda i: (mod_base + cond_row(i) * 6 + k, 0, 0))

    return pl.pallas_call(
        functools.partial(_outproj_kernel, n_parts=len(parts), n_res=len(residuals), n_lat_tiles=n_lat_tiles,
                          alpha=alpha),
        grid=(n_out // tm,),
        in_specs=[pl.BlockSpec((tm, p.shape[1]), lambda i: (i, 0)) for p in parts] + [
            pl.BlockSpec(w_out.shape, lambda i: (0, 0), pipeline_mode=pl.Buffered(1))] + res_specs + [
            mod_spec(2), mod_spec(3), mod_spec(4),
            pl.BlockSpec((1, d), lambda i: (0, 0)),
            pl.BlockSpec((1, d), lambda i: (0, 0)),
        ],
        out_specs=[pl.BlockSpec((tm, d), lambda i: (i, 0)), pl.BlockSpec((tm, d), lambda i: (i, 0))],
        out_shape=[jax.ShapeDtypeStruct((n_out, d), F32), jax.ShapeDtypeStruct((n_out, d), BF16)],
        compiler_params=_params("arbitrary"),
        name="outproj_ln",
    )(*parts, w_out, *residuals, mod3, mod3, mod3, ln_g.reshape(1, d), ln_b.reshape(1, d))


def _ffn_kernel(h_ref, hp_ref, hn_ref, wa_ref, wg_ref, dwa_ref, dwg_ref, dba_ref, dbg_ref, wd_ref,
                x_ref, gate_ref, lg_ref, lb_ref, o_ref,
                hext_ref, za0_ref, za1_ref, zg0_ref, zg1_ref, acc_ref, *,
                tm, halo, sub, nf, seq, ctx_len, n_lat_tiles, alpha):
    za_refs = (za0_ref, za1_ref)
    zg_refs = (zg0_ref, zg1_ref)
    i = pl.program_id(0)
    c = pl.program_id(1)
    is_lat = i < n_lat_tiles
    row = lax.broadcasted_iota(jnp.int32, (tm, 1), 0) + i * tm
    pos = row & (ctx_len - 1)
    has_prev = pos != 0
    has_next = pos != ctx_len - 1

    chunks = [slice(s0, s0 + sub) for s0 in range(0, wa_ref.shape[1], sub)]

    def up_proj(slot):
        za_refs[slot][...] = _dot(hext_ref[...], wa_ref[...])
        zg_refs[slot][...] = _dot(hext_ref[...], wg_ref[...])

    def conv(z_ref, dw_ref, db_ref, cs, masked):
        z = z_ref[:, cs]
        rows_ext = tm + 2 * halo
        z_prev = pltpu.roll(z, 1, 0)[halo:halo + tm]
        z_next = pltpu.roll(z, rows_ext - 1, 0)[halo:halo + tm]
        if masked:
            z_prev = jnp.where(has_prev, z_prev, 0.0)
            z_next = jnp.where(has_next, z_next, 0.0)
        return z_prev * dw_ref[0:1, cs] + z[halo:halo + tm] * dw_ref[1:2, cs] + z_next * dw_ref[2:3, cs] + db_ref[:, cs]

    def step(up_slot, down_slot, masked=False):
        if up_slot is not None:
            up_proj(up_slot)
        if down_slot is not None:
            contrib = None
            for cs in chunks:
                a = conv(za_refs[down_slot], dwa_ref, dba_ref, cs, masked)
                g = conv(zg_refs[down_slot], dwg_ref, dbg_ref, cs, masked)
                t = _dot_t(wd_ref[:, cs], (g * _sigmoid(g) * a).astype(BF16))
                contrib = t if contrib is None else contrib + t
            acc_ref[...] += contrib

    @pl.when(c == 0)
    def _():
        starts_seq = is_lat & ((i * tm) % seq == 0)
        ends_seq = is_lat & (((i + 1) * tm) % seq == 0)
        zero_halo = jnp.zeros(hp_ref.shape, hp_ref.dtype)
        hext_ref[0:halo, :] = jnp.where(starts_seq, zero_halo, hp_ref[...])
        hext_ref[halo:halo + tm, :] = h_ref[...]
        hext_ref[halo + tm:, :] = jnp.where(ends_seq, zero_halo, hn_ref[...])
        acc_ref[...] = jnp.zeros_like(acc_ref)
        step(0, None)

    for masked in (False, True):
        for slot in (0, 1):
            @pl.when((c >= 1) & (c < nf) & (c % 2 == slot) & (is_lat != masked))
            def _():
                step(slot, 1 - slot, masked)

        @pl.when((c == nf) & (is_lat != masked))
        def _():
            step(None, (nf - 1) % 2, masked)

    @pl.when(c == nf)
    def _():
        eb = min(tm, 128)
        for r0 in range(0, tm, eb):
            rs = slice(r0, r0 + eb)
            o_ref[rs, :] = _layer_norm(alpha * x_ref[rs, :] + gate_ref[0] * acc_ref[:, rs].T, lg_ref[...], lb_ref[...])


def _ffn(h2, x1, layer, w_up, dw_w, dw_b, w_down, mod3, mod_base, ln_g, ln_b, *, tm, n_lat_tiles, seq, ctx_len,
         nbatch, alpha):
    n_out, d = x1.shape
    depth, _, d_ff = w_down.shape
    tf = _pick((d_ff,), (512, 256, 128))
    nf = d_ff // tf
    halo = V7X_BF16_SUBLANES
    hb = tm // halo
    nhalo = h2.shape[0] // halo
    ntiles = n_out // tm
    assert dw_w.shape[1] == 3

    def cond_row(i):
        return jnp.where(i < n_lat_tiles, (i * tm) // seq, nbatch)

    dw_b3 = dw_b.reshape(depth, 1, 2 * d_ff)

    def up(c):
        return jnp.minimum(c, nf - 1)

    def down(c):
        return jnp.maximum(c - 1, 0)

    return pl.pallas_call(
        functools.partial(_ffn_kernel, tm=tm, halo=halo, sub=min(tf, V7X_MXU_DIM), nf=nf, seq=seq, ctx_len=ctx_len,
                          n_lat_tiles=n_lat_tiles, alpha=alpha),
        grid=(ntiles, nf + 1),
        in_specs=[
            pl.BlockSpec((tm, d), lambda i, c: (i, 0)),
            pl.BlockSpec((halo, d), lambda i, c: (jnp.maximum(i * hb - 1, 0), 0)),
            pl.BlockSpec((halo, d), lambda i, c: (jnp.minimum((i + 1) * hb, nhalo - 1), 0)),
            pl.BlockSpec((None, d, tf), lambda i, c: (layer, 0, up(c))),
            pl.BlockSpec((None, d, tf), lambda i, c: (layer, 0, nf + up(c))),
            pl.BlockSpec((None, 3, tf), lambda i, c: (layer, 0, down(c))),
            pl.BlockSpec((None, 3, tf), lambda i, c: (layer, 0, nf + down(c))),
            pl.BlockSpec((None, 1, tf), lambda i, c: (layer, 0, down(c))),
            pl.BlockSpec((None, 1, tf), lambda i, c: (layer, 0, nf + down(c))),
            pl.BlockSpec((None, d, tf), lambda i, c: (layer, 0, down(c))),
            pl.BlockSpec((tm, d), lambda i, c: (i, 0)),
            pl.BlockSpec((1, 1, d), lambda i, c: (mod_base + cond_row(i) * 6 + 5, 0, 0)),
            pl.BlockSpec((1, d), lambda i, c: (0, 0)),
            pl.BlockSpec((1, d), lambda i, c: (0, 0)),
        ],
        out_specs=pl.BlockSpec((tm, d), lambda i, c: (i, 0)),
        out_shape=jax.ShapeDtypeStruct((n_out, d), F32),
        scratch_shapes=([pltpu.VMEM((tm + 2 * halo, d), BF16)] + [pltpu.VMEM((tm + 2 * halo, tf), F32)] * 4
                        + [pltpu.VMEM((d, tm), F32)]),
        compiler_params=_params("arbitrary", "arbitrary"),
        name="conv_ffn",
    )(h2, h2, h2, w_up, w_up, dw_w, dw_w, dw_b3, dw_b3, w_down, x1, mod3, ln_g.reshape(1, d), ln_b.reshape(1, d))


def _inproj_c_kernel(x_ref, sh_ref, sc_ref, w_ref, o_ref, *, q_scale):
    hb = (x_ref[...] * (1.0 + sc_ref[0]) + sh_ref[0]).astype(BF16)
    d = x_ref.shape[1]
    cols = PROJ_COLS
    for c0 in range(0, o_ref.shape[1], cols):
        z = _dot(hb, w_ref[:, c0:c0 + cols])
        if c0 < d:
            z = z * q_scale
        o_ref[:, c0:c0 + cols] = z.astype(BF16)


def _inproj_c(xc, mod3, mod_base, w_in, *, tm, n_lat_tiles, seq, nbatch):
    n, d = xc.shape
    n_in = w_in.shape[1]
    assert d % PROJ_COLS == 0 and n_in == 3 * d

    def cond_row(i):
        return jnp.where(i < n_lat_tiles, (i * tm) // seq, nbatch)

    return pl.pallas_call(
        functools.partial(_inproj_c_kernel, q_scale=ATTN_Q_SCALE),
        grid=(n // tm,),
        in_specs=[
            pl.BlockSpec((tm, d), lambda i: (i, 0)),
            pl.BlockSpec((1, 1, d), lambda i: (mod_base + cond_row(i) * 6 + 0, 0, 0)),
            pl.BlockSpec((1, 1, d), lambda i: (mod_base + cond_row(i) * 6 + 1, 0, 0)),
            pl.BlockSpec((d, n_in), lambda i: (0, 0), pipeline_mode=pl.Buffered(1)),
        ],
        out_specs=pl.BlockSpec((tm, n_in), lambda i: (i, 0)),
        out_shape=jax.ShapeDtypeStruct((n, n_in), BF16),
        compiler_params=_params("arbitrary"),
        name="inproj_c",
    )(xc, mod3, mod3, w_in)


def _na_bias_table(rel_bias):
    nheads, nrow, ncol = rel_bias.shape
    qc = np.arange(GRID_W)[:, None]
    kc = np.arange(GRID_W)[None, :]
    cstart = np.clip(qc - NA_COLS // 2, 0, GRID_W - NA_COLS)
    valid = (kc >= cstart) & (kc < cstart + NA_COLS)
    period = 2 * GRID_W - 1
    lead = GRID_W - NA_COLS - 1
    padded = jnp.pad(rel_bias, ((0, 0), (0, 0), (lead, period - lead - ncol)), constant_values=NEG)
    tiled = jnp.broadcast_to(padded[:, :, None, :], (nheads, nrow, GRID_W, period))
    skew = tiled.reshape(nheads, nrow, GRID_W * period)[:, :, :GRID_W * (period - 1)]
    toep = skew.reshape(nheads, nrow, GRID_W, period - 1)[:, :, :, GRID_W - 2:]
    toep = jnp.where(valid[None, None], toep * LOG2E, NEG)
    neg = jnp.full((nheads, 1, GRID_W, GRID_W), NEG, F32)
    ext = jnp.concatenate([neg, toep, neg], axis=1)
    return jnp.concatenate([ext[:, :2 * NA_ROWS], ext[:, 1:]], axis=-1)


def _natten_kernel(q_ref, k_ref, v_ref, kc_ref, vc_ref, bias_ref, o_ref, *, rows):
    nq = NA_QROWS * GRID_W
    nk = NA_KROWS * GRID_W
    left = lax.broadcasted_iota(jnp.int32, (GRID_W, 2 * GRID_W), 1) < GRID_W
    for blk, hh in [(b_, h_) for b_ in range(rows // NA_QROWS) for h_ in range(bias_ref.shape[0])]:
        hs = slice(hh * HEAD_DIM, (hh + 1) * HEAD_DIM)
        ks = min(max(NA_QROWS * blk - NA_ROWS // 2, 0), rows - NA_KROWS)
        q = q_ref[blk * nq:(blk + 1) * nq, hs]
        s_loc = _dot_t(q, k_ref[ks * GRID_W:ks * GRID_W + nk, hs])
        bias_rows = []
        for qi in range(NA_QROWS):
            qr = NA_QROWS * blk + qi
            r0 = min(max(qr - NA_ROWS // 2, 0), rows - NA_ROWS)
            slabs = []
            for j in range(NA_KROWS // 2):
                kr = ks + 2 * j
                ok0 = r0 <= kr < r0 + NA_ROWS
                ok1 = r0 <= kr + 1 < r0 + NA_ROWS
                if not (ok0 or ok1):
                    slabs.append(jnp.full((GRID_W, 2 * GRID_W), NEG, F32))
                    continue
                slab = bias_ref[hh, kr - qr + NA_ROWS]
                if not ok1:
                    slab = jnp.where(left, slab, NEG)
                elif not ok0:
                    slab = jnp.where(left, NEG, slab)
                slabs.append(slab)
            bias_rows.append(jnp.concatenate(slabs, axis=1))
        s_loc = s_loc + jnp.concatenate(bias_rows, axis=0)
        s_ctx = _dot_t(q, kc_ref[:, hs])
        m = jnp.maximum(jnp.max(s_loc, axis=-1, keepdims=True), jnp.max(s_ctx, axis=-1, keepdims=True))
        p_loc = jnp.exp2(s_loc - m)
        p_ctx = jnp.exp2(s_ctx - m)
        o = (_dot(p_loc.astype(BF16), _with_ones(v_ref[ks * GRID_W:ks * GRID_W + nk, hs]))
             + _dot(p_ctx.astype(BF16), _with_ones(vc_ref[:, hs])))
        o_ref[blk * nq:(blk + 1) * nq, hs] = (o[:, :HEAD_DIM] / o[:, HEAD_DIM:HEAD_DIM + 1]).astype(BF16)


def _natten(qkv, bias_tab, *, nbatch, seq, ctx_len, d):
    rows = seq // GRID_W
    nheads = d // HEAD_DIM
    assert rows % NA_QROWS == 0 and rows >= NA_KROWS
    ctx_blk0 = nbatch * seq // ctx_len
    hg = NA_HEADS_PER_STEP
    assert nheads % hg == 0
    ngroups = nheads // hg
    gw = hg * HEAD_DIM
    return pl.pallas_call(
        functools.partial(_natten_kernel, rows=rows),
        grid=(ngroups, nbatch),
        in_specs=[
            pl.BlockSpec((seq, gw), lambda h, b: (b, h)),
            pl.BlockSpec((seq, gw), lambda h, b: (b, ngroups + h)),
            pl.BlockSpec((seq, gw), lambda h, b: (b, 2 * ngroups + h)),
            pl.BlockSpec((ctx_len, gw), lambda h, b: (ctx_blk0 + b, ngroups + h)),
            pl.BlockSpec((ctx_len, gw), lambda h, b: (ctx_blk0 + b, 2 * ngroups + h)),
            pl.BlockSpec((hg,) + bias_tab.shape[1:], lambda h, b: (h, 0, 0, 0)),
        ],
        out_specs=pl.BlockSpec((seq, gw), lambda h, b: (b, h)),
        out_shape=jax.ShapeDtypeStruct((nbatch * seq, d), BF16),
        compiler_params=_params("arbitrary", "arbitrary"),
        name="natten",
    )(qkv, qkv, qkv, qkv, qkv, bias_tab)


def kernel(x, c, ctx, c_ctx, ada_w, ada_b, post_ln_g, post_ln_b, ab_w_in, ab_w_out, ab_q_gain, ab_k_gain,
           ab_dw_w, ab_dw_b, ab_norm_g, ab_norm_b, c_w_in, c_w_out, c_rel_bias, ffn_w_up, ffn_dw_w, ffn_dw_b,
           ffn_w_down):
    nbatch, seq, d = x.shape
    ctx_len = ctx.shape[1]
    depth = ada_w.shape[0]
    assert depth == 2 and nbatch + 1 <= COND_ROWS and nbatch * ctx_len <= seq
    assert seq % GRID_W == 0 and seq & (seq - 1) == 0 and ctx_len & (ctx_len - 1) == 0 and ctx_len <= seq
    nx, nc = nbatch * seq, nbatch * ctx_len
    alpha = (2 * depth) ** 0.25
    tm = _pick((seq, nc), (512, 256))
    n_lat_tiles = nx // tm
    geo = dict(tm=tm, n_lat_tiles=n_lat_tiles, seq=seq, nbatch=nbatch)

    cond = jnp.concatenate([c, c_ctx[None], jnp.zeros((COND_ROWS - nbatch - 1, d), F32)], axis=0)
    mod3 = _ada(cond, ada_w, ada_b).reshape(depth * COND_ROWS * 6, 1, d)
    x2d, c2d = x.reshape(nx, d), ctx.reshape(nc, d)
    ffn_w = (ffn_w_up.astype(BF16), ffn_dw_w, ffn_dw_b, jnp.swapaxes(ffn_w_down, 1, 2).astype(BF16))

    mod_base = 0
    qkv, glu = _inproj_ab(x2d, c2d, mod3, mod_base, ab_w_in[0].astype(BF16), ab_q_gain[0][None],
                          ab_k_gain[0][None], _rope_tables(seq, tm), **geo)
    attn = _gqa(qkv, nbatch=nbatch, seq=seq, ctx_len=ctx_len)
    conv = _conformer_conv(glu, ab_dw_w[0], ab_dw_b[0], ab_norm_g[0], ab_norm_b[0],
                           nbatch=nbatch, seq=seq, ctx_len=ctx_len)
    x1, h2 = _outproj([attn, conv], ab_w_out[0].astype(BF16), [x2d, c2d], mod3, mod_base, post_ln_g[0, 0],
                      post_ln_b[0, 0], n_out=nx + nc, alpha=alpha, **geo)
    xc = _ffn(h2, x1, 0, *ffn_w, mod3, mod_base, post_ln_g[0, 1], post_ln_b[0, 1], ctx_len=ctx_len, alpha=alpha,
              **geo)

    mod_base = COND_ROWS * 6
    qkv = _inproj_c(xc, mod3, mod_base, c_w_in[0].astype(BF16), **geo)
    attn = _natten(qkv, _na_bias_table(c_rel_bias[0]), nbatch=nbatch, seq=seq, ctx_len=ctx_len, d=d)
    x1, h2 = _outproj([attn], c_w_out[0].astype(BF16), [xc], mod3, mod_base, post_ln_g[1, 0], post_ln_b[1, 0],
                      n_out=nx, alpha=alpha, **geo)
    out = _ffn(h2, x1, 1, *ffn_w, mod3, mod_base, post_ln_g[1, 1], post_ln_b[1, 1], ctx_len=ctx_len, alpha=alpha,
               **geo)
    return out.reshape(nbatch, seq, d)
```
